```python
import math
import jax
import jax.numpy as jnp
from jax import lax
import numpy as np

D_MODEL = 1024
BATCH = 8
SEQ = 2048
DEPTH = 2
DEC_BATCH = 32
DEC_SEQ = 1
PAST_LEN = 8192
PAGE_SIZE = 128

N_EVEN = (DEPTH + 1) // 2
N_ODD = DEPTH // 2

A_HEADS = 4
A_DK = 128
A_DV = 128
A_QK = A_HEADS * A_DK
A_WIDTH = A_HEADS * A_DV
A_CONV_CH = 2 * A_QK + A_WIDTH
CONV_W = 4
DELTA_CHUNK = 64

B_HEADS = 8
B_KV = 2
B_HPG = B_HEADS // B_KV
B_DH = 64
B_WIDTH = B_HEADS * B_DH
N_BRANCH = 3
B_KV_COLS = N_BRANCH * 2 * B_KV * B_DH
CMP_BLK = 32
SEL_BLK = 64
SEL_TOPK = 16
WINDOW = 512
SEL_QBLK = 64
WIN_QBLK = 128
B_SCALE = B_DH ** -0.5

EVEN_SIZES = (A_CONV_CH, A_HEADS, A_HEADS, A_WIDTH, B_WIDTH, B_KV_COLS, B_HEADS * N_BRANCH)
EVEN_IN = A_CONV_CH + 2 * A_HEADS + A_WIDTH + B_WIDTH + B_KV_COLS + B_HEADS * N_BRANCH

C_HEADS = 4
C_DK = 128
C_DV = 256
C_QK = C_HEADS * C_DK
C_WIDTH = C_HEADS * C_DV
C_LOWRANK = 16
GLA_TAU = 16.0
GLA_CHUNK = 16
ODD_SIZES = (C_QK, C_QK, C_WIDTH, C_LOWRANK, C_WIDTH)
ODD_IN = 2 * C_QK + 2 * C_WIDTH + C_LOWRANK

D_FF = -(-(8 * D_MODEL) // (3 * 256)) * 256

ALPHA = (2 * DEPTH) ** 0.25
BETA = (8 * DEPTH) ** -0.25
NEG = -1e30
FORCE = 1e9

kernel_name = 'hybrid_deltanet_nsa_gla_deepnorm_step'

f32 = jnp.float32


def split_cols(h, sizes):
    out, s = [], 0
    for n in sizes:
        out.append(h[..., s:s + n])
        s += n
    return out


def layer_norm(x, g, b, eps=1e-5):
    xf = x.astype(f32)
    mu = jnp.mean(xf, -1, keepdims=True)
    var = jnp.mean(jnp.square(xf - mu), -1, keepdims=True)
    return ((xf - mu) * lax.rsqrt(var + eps) * g.astype(f32) + b.astype(f32)).astype(x.dtype)


def rms_norm(x, g, eps=1e-6):
    xf = x.astype(f32)
    return xf * lax.rsqrt(jnp.mean(xf * xf, -1, keepdims=True) + eps) * g.astype(f32)


def l2_normalize(x, eps=1e-6):
    xf = x.astype(f32)
    return xf * lax.rsqrt(jnp.sum(xf * xf, -1, keepdims=True) + eps)


def masked_softmax(s, mask):
    p = jax.nn.softmax(jnp.where(mask, s, NEG), axis=-1)
    return jnp.where(mask, p, 0.0)


def causal_conv(x, buf, w):
    T = x.shape[1]
    xp = jnp.concatenate([buf, x], axis=1)
    y = xp[:, 0:T] * w[0]
    for j in range(1, CONV_W):
        y = y + xp[:, j:j + T] * w[j]
    return y, xp[:, -(CONV_W - 1):]


def to_chunks(a, C, pad):
    a = jnp.pad(a.astype(f32), [(0, 0), (0, pad)] + [(0, 0)] * (a.ndim - 2))
    B_, Tp = a.shape[:2]
    a = a.reshape((B_, Tp // C, C) + a.shape[2:])
    return jnp.moveaxis(a, 3, 1)


def from_chunks(o, T):
    N, B_, H, C, D = o.shape
    return jnp.transpose(o, (1, 0, 3, 2, 4)).reshape(B_, N * C, H, D)[:, :T]


def gated_delta_chunked(q, k, v, g, beta, S0):
    B_, T, H, _ = q.shape
    C = min(DELTA_CHUNK, T)
    pad = (-T) % C
    q, k, v, g, beta = [to_chunks(a, C, pad) for a in (q, k, v, g, beta)]
    q = q * (A_DK ** -0.5)
    gam = jnp.cumsum(g, axis=-1)
    causal = jnp.tril(jnp.ones((C, C), bool))
    strict = jnp.tril(jnp.ones((C, C), bool), -1)
    diff = gam[..., :, None] - gam[..., None, :]
    Ldec = jnp.where(causal, jnp.exp(jnp.where(causal, diff, 0.0)), 0.0)
    kb = k * beta[..., None]
    A = jnp.where(strict, jnp.einsum('bhncd,bhnsd->bhncs', kb, k) * Ldec, 0.0)
    eye = jnp.eye(C, dtype=f32)
    Tm = lax.linalg.triangular_solve(eye + A, jnp.broadcast_to(eye, A.shape), left_side=True, lower=True)
    u = jnp.einsum('bhncs,bhnse->bhnce', Tm, v * beta[..., None])
    w = jnp.einsum('bhncs,bhnsd->bhncd', Tm, kb * jnp.exp(gam)[..., None])
    qk = jnp.einsum('bhncd,bhnsd->bhncs', q, k) * Ldec
    qg = q * jnp.exp(gam)[..., None]
    kdec = k * jnp.exp(gam[..., -1:] - gam)[..., None]
    glast = jnp.exp(gam[..., -1])

    def step(S, xs):
        u_c, w_c, qk_c, qg_c, kd_c, gl_c = xs
        v_new = u_c - jnp.einsum('bhcd,bhde->bhce', w_c, S)
        o = jnp.einsum('bhcd,bhde->bhce', qg_c, S) + jnp.einsum('bhcs,bhse->bhce', qk_c, v_new)
        S = S * gl_c[..., None, None] + jnp.einsum('bhcd,bhce->bhde', kd_c, v_new)
        return S, o

    xs = tuple(jnp.moveaxis(a, 2, 0) for a in (u, w, qk, qg, kdec, glast))
    S, o = lax.scan(step, S0.astype(f32), xs)
    return from_chunks(o, T), S


def gla_chunked(q, k, v, log_a, S0):
    B_, T, H, _ = q.shape
    C = min(GLA_CHUNK, T)
    pad = (-T) % C
    q, k, v, la = [to_chunks(a, C, pad) for a in (q, k, v, log_a)]
    q = q * (C_DK ** -0.5)
    b = jnp.cumsum(la, axis=3)
    qt = q * jnp.exp(b)
    kt = k * jnp.exp(-b)
    causal = jnp.tril(jnp.ones((C, C), bool))
    att = jnp.where(causal, jnp.einsum('bhncd,bhnsd->bhncs', qt, kt), 0.0)
    intra = jnp.einsum('bhncs,bhnse->bhnce', att, v)
    kdec = k * jnp.exp(b[..., -1:, :] - b)
    glast = jnp.exp(b[..., -1, :])

    def step(S, xs):
        qt_c, kd_c, v_c, in_c, gl_c = xs
        o = jnp.einsum('bhcd,bhde->bhce', qt_c, S) + in_c
        S = S * gl_c[..., None] + jnp.einsum('bhcd,bhce->bhde', kd_c, v_c)
        return S, o

    xs = tuple(jnp.moveaxis(a, 2, 0) for a in (qt, kdec, v, intra, glast))
    S, o = lax.scan(step, S0.astype(f32), xs)
    return from_chunks(o, T), S


def nsa_compressed(q, qpos, kc, vc):
    B_, Tq = q.shape[:2]
    Tk = kc.shape[1]
    nc = Tk // CMP_BLK
    kcb = jnp.mean(kc[:, :nc * CMP_BLK].reshape(B_, nc, CMP_BLK, B_KV, B_DH), axis=2)
    vcb = jnp.mean(vc[:, :nc * CMP_BLK].reshape(B_, nc, CMP_BLK, B_KV, B_DH), axis=2)
    qg = q.reshape(B_, Tq, B_KV, B_HPG, B_DH)
    s = jnp.einsum('bqghd,bcgd->bghqc', qg, kcb) * B_SCALE
    mask = (jnp.arange(nc)[None, :] + 1) * CMP_BLK <= qpos[:, None] + 1
    p = masked_softmax(s, mask)
    o = jnp.einsum('bghqc,bcgd->bqghd', p, vcb).reshape(B_, Tq, B_HEADS, B_DH)
    return o, jnp.sum(p, axis=2)


def nsa_select_blocks(imp, qpos, Tk):
    B_, G, Tq, nc = imp.shape
    ns = -(-Tk // SEL_BLK)
    r = SEL_BLK // CMP_BLK
    imp = jnp.pad(imp, ((0, 0), (0, 0), (0, 0), (0, ns * r - nc)))
    imp = jnp.sum(imp.reshape(B_, G, Tq, ns, r), axis=-1)
    blk = jnp.arange(ns)[None, :]
    cur = (qpos // SEL_BLK)[:, None]
    valid = blk * SEL_BLK <= qpos[:, None]
    forced = (blk == 0) | (blk == cur) | (blk == cur - 1)
    score = jnp.where(forced, FORCE, jnp.where(valid, imp, NEG))
    _, idx = lax.top_k(score, min(SEL_TOPK, ns))
    return idx


def nsa_selected(q, qpos, ks, vs, idx):
    B_, Tk = ks.shape[:2]
    Tq = q.shape[1]
    ns = -(-Tk // SEL_BLK)
    padk = ns * SEL_BLK - Tk
    kb = jnp.pad(ks, ((0, 0), (0, padk), (0, 0), (0, 0))).reshape(B_, ns, SEL_BLK, B_KV, B_DH).transpose(0, 3, 1, 2, 4)
    vb = jnp.pad(vs, ((0, 0), (0, padk), (0, 0), (0, 0))).reshape(B_, ns, SEL_BLK, B_KV, B_DH).transpose(0, 3, 1, 2, 4)
    qb = SEL_QBLK if Tq % SEL_QBLK == 0 else Tq
    gather = jax.vmap(jax.vmap(lambda tbl, i: tbl[i]))

    def block(i):
        s0 = i * qb
        qi = lax.dynamic_slice_in_dim(q, s0, qb, 1).reshape(B_, qb, B_KV, B_HPG, B_DH)
        pi = lax.dynamic_slice_in_dim(qpos, s0, qb, 0)
        ii = lax.dynamic_slice_in_dim(idx, s0, qb, 2)
        kg = gather(kb, ii)
        vg = gather(vb, ii)
        nk = ii.shape[-1] * SEL_BLK
        s = jnp.einsum('bqghd,bgqksd->bghqks', qi, kg) * B_SCALE
        s = s.reshape(B_, B_KV, B_HPG, qb, nk)
        kpos = ii[..., None] * SEL_BLK + jnp.arange(SEL_BLK)
        mask = (kpos <= pi[None, None, :, None, None]).reshape(B_, B_KV, 1, qb, nk)
        p = masked_softmax(s, mask)
        o = jnp.einsum('bghqn,bgqnd->bqghd', p, vg.reshape(B_, B_KV, qb, nk, B_DH))
        return o.reshape(B_, qb, B_HEADS, B_DH)

    out = lax.map(block, jnp.arange(Tq // qb))
    return jnp.moveaxis(out, 0, 1).reshape(B_, Tq, B_HEADS, B_DH)


def window_attend(q, qpos, k, v, kpos):
    B_, Q = q.shape[:2]
    qg = q.reshape(B_, Q, B_KV, B_HPG, B_DH)
    s = jnp.einsum('bqghd,bkgd->bghqk', qg, k) * B_SCALE
    d = qpos[:, None] - kpos[None, :]
    mask = (d >= 0) & (d < WINDOW) & (kpos[None, :] >= 0)
    p = masked_softmax(s, mask)
    return jnp.einsum('bghqk,bkgd->bqghd', p, v).reshape(B_, Q, B_HEADS, B_DH)


def nsa_window_prompt(q, kw, vw):
    B_, T = q.shape[:2]
    qb = WIN_QBLK if T % WIN_QBLK == 0 else T
    span = WINDOW + qb
    kp = jnp.pad(kw, ((0, 0), (WINDOW, 0), (0, 0), (0, 0)))
    vp = jnp.pad(vw, ((0, 0), (WINDOW, 0), (0, 0), (0, 0)))

    def block(i):
        s0 = i * qb
        qi = lax.dynamic_slice_in_dim(q, s0, qb, 1)
        ki = lax.dynamic_slice_in_dim(kp, s0, span, 1)
        vi = lax.dynamic_slice_in_dim(vp, s0, span, 1)
        return window_attend(qi, s0 + jnp.arange(qb), ki, vi, s0 - WINDOW + jnp.arange(span))

    out = lax.map(block, jnp.arange(T // qb))
    return jnp.moveaxis(out, 0, 1).reshape(B_, T, B_HEADS, B_DH)


def nsa_mixer(q, kv_new, gates, hist, win_buf, pos0):
    B_, T = q.shape[:2]
    q = q.astype(f32)
    kv_new = kv_new.astype(f32)
    rows = kv_new[:, :, :2].reshape(B_, T, 4, B_KV, B_DH)
    full = rows if hist is None else jnp.concatenate([hist.astype(f32), rows], axis=1)
    Tk = full.shape[1]
    qpos = pos0 + jnp.arange(T)
    o_cmp, imp = nsa_compressed(q, qpos, full[:, :, 0], full[:, :, 1])
    idx = nsa_select_blocks(imp, qpos, Tk)
    o_sel = nsa_selected(q, qpos, full[:, :, 2], full[:, :, 3], idx)
    win_new = kv_new[:, :, 2]
    if win_buf is None:
        o_win = nsa_window_prompt(q, win_new[:, :, 0], win_new[:, :, 1])
        new_buf = win_new[:, -min(WINDOW, T):]
    else:
        wb = win_buf.shape[1]
        cat = jnp.concatenate([win_buf.astype(f32), win_new], axis=1)
        kpos = pos0 - wb + jnp.arange(wb + T)
        o_win = window_attend(q, qpos, cat[:, :, 0], cat[:, :, 1], kpos)
        new_buf = cat[:, -min(WINDOW, wb + T):]
    o = gates[..., 0:1] * o_cmp + gates[..., 1:2] * o_sel + gates[..., 2:3] * o_win
    return o, rows, new_buf


def even_mixer(x, w_in, conv_w, A_log, dt_bias, dn_g, w_out, conv_buf, S0, hist, win_buf, pos0):
    B_, T, _ = x.shape
    h = x @ w_in
    qkv_a, beta_a, a_a, gate_a, q_b, kv_b, g_b = split_cols(h, EVEN_SIZES)
    conv, new_conv = causal_conv(qkv_a, conv_buf.astype(x.dtype), conv_w)
    conv = jax.nn.silu(conv.astype(f32))
    qa, ka, va = split_cols(conv, (A_QK, A_QK, A_WIDTH))
    qa = l2_normalize(qa.reshape(B_, T, A_HEADS, A_DK))
    ka = l2_normalize(ka.reshape(B_, T, A_HEADS, A_DK))
    va = va.reshape(B_, T, A_HEADS, A_DV)
    beta = jax.nn.sigmoid(beta_a.astype(f32))
    g = -jnp.exp(A_log.astype(f32)) * jax.nn.softplus(a_a.astype(f32) + dt_bias.astype(f32))
    o_a, S_new = gated_delta_chunked(qa, ka, va, g, beta, S0)
    o_a = rms_norm(o_a, dn_g) * jax.nn.silu(gate_a.astype(f32).reshape(B_, T, A_HEADS, A_DV))
    o_b, rows, new_buf = nsa_mixer(q_b.reshape(B_, T, B_HEADS, B_DH),
                                   kv_b.reshape(B_, T, N_BRANCH, 2, B_KV, B_DH),
                                   jax.nn.sigmoid(g_b.astype(f32)).reshape(B_, T, B_HEADS, N_BRANCH),
                                   hist, win_buf, pos0)
    mixed = jnp.concatenate([o_a.reshape(B_, T, A_WIDTH), o_b.reshape(B_, T, B_WIDTH)], axis=-1).astype(x.dtype)
    dt = x.dtype
    return mixed @ w_out, rows.astype(dt), new_buf.astype(dt), new_conv.astype(dt), S_new.astype(dt)


def odd_mixer(x, w_in, w_g2, b_g2, gn_g, w_out, S0):
    B_, T, _ = x.shape
    h = (x @ w_in).astype(f32)
    q, k, v, g1, r = split_cols(h, ODD_SIZES)
    log_a = jax.nn.log_sigmoid(g1 @ w_g2.astype(f32) + b_g2.astype(f32)) / GLA_TAU
    o, S_new = gla_chunked(q.reshape(B_, T, C_HEADS, C_DK), k.reshape(B_, T, C_HEADS, C_DK),
                           v.reshape(B_, T, C_HEADS, C_DV), log_a.reshape(B_, T, C_HEADS, C_DK), S0)
    o = rms_norm(o, gn_g) * jax.nn.silu(r.reshape(B_, T, C_HEADS, C_DV))
    return o.reshape(B_, T, C_WIDTH).astype(x.dtype) @ w_out, S_new.astype(x.dtype)


def swiglu(x, wg, wu, wd):
    return (jax.nn.silu(x @ wg) * (x @ wu)) @ wd


def setup_inputs(seed: int = 0) -> dict:
    key = jax.random.key(seed)
    keys = iter(jax.random.split(key, 40))

    def nrm(shape, scale):
        return jax.random.normal(next(keys), shape, jnp.float32) * scale

    n_pages = PAST_LEN // PAGE_SIZE
    used = DEC_BATCH * n_pages
    n_pool = used + max(1, used // 4)
    page_table = jax.random.permutation(next(keys), n_pool)[:used].reshape(DEC_BATCH, n_pages).astype(jnp.int32)
    w_buf = min(WINDOW, PAST_LEN)
    dt = jnp.exp(jax.random.uniform(next(keys), (N_EVEN, A_HEADS), jnp.float32, math.log(1e-3), math.log(1e-1)))
    dt_bias = dt + jnp.log(-jnp.expm1(-dt))
    a_log = jnp.log(jax.random.uniform(next(keys), (N_EVEN, A_HEADS), jnp.float32, 1.0, 16.0))
    mix_even = A_WIDTH + B_WIDTH
    return {
        'x_prompt': nrm((BATCH, SEQ, D_MODEL), 1.0),
        'x_sample': nrm((DEC_BATCH, DEC_SEQ, D_MODEL), 1.0),
        'cache_nsa_kv': nrm((n_pool, N_EVEN, PAGE_SIZE, 4, B_KV, B_DH), 1.0),
        'state_nsa_win': nrm((N_EVEN, DEC_BATCH, w_buf, 2, B_KV, B_DH), 1.0),
        'state_delta_conv': nrm((N_EVEN, DEC_BATCH, CONV_W - 1, A_CONV_CH), 1.0),
        'state_delta_S': nrm((N_EVEN, DEC_BATCH, A_HEADS, A_DK, A_DV), 0.1),
        'state_gla_S': nrm((N_ODD, DEC_BATCH, C_HEADS, C_DK, C_DV), 1.0),
        'page_table': page_table,
        'w_in_even': nrm((N_EVEN, D_MODEL, EVEN_IN), D_MODEL ** -0.5),
        'conv_w_delta': nrm((N_EVEN, CONV_W, A_CONV_CH), CONV_W ** -0.5),
        'delta_A_log': a_log,
        'delta_dt_bias': dt_bias,
        'delta_norm_g': 1.0 + nrm((N_EVEN, A_DV), 0.02),
        'w_out_even': nrm((N_EVEN, mix_even, D_MODEL), BETA * mix_even ** -0.5),
        'w_in_odd': nrm((N_ODD, D_MODEL, ODD_IN), D_MODEL ** -0.5),
        'w_gla_gate2': nrm((N_ODD, C_LOWRANK, C_QK), C_LOWRANK ** -0.5),
        'b_gla_gate2': nrm((N_ODD, C_QK), 0.02),
        'gla_norm_g': 1.0 + nrm((N_ODD, C_DV), 0.02),
        'w_out_odd': nrm((N_ODD, C_WIDTH, D_MODEL), BETA * C_WIDTH ** -0.5),
        'w_ffn_gate': nrm((DEPTH, D_MODEL, D_FF), D_MODEL ** -0.5),
        'w_ffn_up': nrm((DEPTH, D_MODEL, D_FF), BETA * D_MODEL ** -0.5),
        'w_ffn_down': nrm((DEPTH, D_FF, D_MODEL), BETA * D_FF ** -0.5),
        'ln_g': 1.0 + nrm((DEPTH, 2, D_MODEL), 0.02),
        'ln_b': nrm((DEPTH, 2, D_MODEL), 0.02),
    }


def reference(x_prompt, x_sample, cache_nsa_kv, state_nsa_win, state_delta_conv, state_delta_S, state_gla_S,
              page_table, w_in_even, conv_w_delta, delta_A_log, delta_dt_bias, delta_norm_g, w_out_even,
              w_in_odd, w_gla_gate2, b_gla_gate2, gla_norm_g, w_out_odd, w_ffn_gate, w_ffn_up, w_ffn_down,
              ln_g, ln_b):
    past = page_table.shape[1] * PAGE_SIZE
    dec_b = x_sample.shape[0]
    bp = x_prompt.shape[0]
    xp, xs = x_prompt, x_sample
    p_rows, p_win, p_conv, p_dS, p_gS = [], [], [], [], []
    s_rows, s_win, s_conv, s_dS, s_gS = [], [], [], [], []
    for layer in range(DEPTH):
        li = layer // 2
        if layer % 2 == 0:
            wts = (w_in_even[li], conv_w_delta[li], delta_A_log[li], delta_dt_bias[li], delta_norm_g[li], w_out_even[li])
            conv0 = jnp.zeros((bp, CONV_W - 1, A_CONV_CH), xp.dtype)
            S0 = jnp.zeros((bp, A_HEADS, A_DK, A_DV), f32)
            yp, rows, win, conv, dS = even_mixer(xp, *wts, conv0, S0, None, None, 0)
            p_rows.append(rows); p_win.append(win); p_conv.append(conv); p_dS.append(dS)
            hist = cache_nsa_kv[page_table, li].reshape(dec_b, past, 4, B_KV, B_DH)
            ys, rows, win, conv, dS = even_mixer(xs, *wts, state_delta_conv[li], state_delta_S[li], hist,
                                                 state_nsa_win[li], past)
            s_rows.append(rows); s_win.append(win); s_conv.append(conv); s_dS.append(dS)
        else:
            wts = (w_in_odd[li], w_gla_gate2[li], b_gla_gate2[li], gla_norm_g[li], w_out_odd[li])
            yp, gS = odd_mixer(xp, *wts, jnp.zeros((bp, C_HEADS, C_DK, C_DV), f32))
            p_gS.append(gS)
            ys, gS = odd_mixer(xs, *wts, state_gla_S[li])
            s_gS.append(gS)
        xp = layer_norm(ALPHA * xp + yp, ln_g[layer, 0], ln_b[layer, 0])
        xs = layer_norm(ALPHA * xs + ys, ln_g[layer, 0], ln_b[layer, 0])
        xp = layer_norm(ALPHA * xp + swiglu(xp, w_ffn_gate[layer], w_ffn_up[layer], w_ffn_down[layer]), ln_g[layer, 1], ln_b[layer, 1])
        xs = layer_norm(ALPHA * xs + swiglu(xs, w_ffn_gate[layer], w_ffn_up[layer], w_ffn_down[layer]), ln_g[layer, 1], ln_b[layer, 1])
    return (xp, xs,
            jnp.stack(p_rows), jnp.stack(p_win), jnp.stack(p_conv), jnp.stack(p_dS), jnp.stack(p_gS),
            jnp.stack(s_rows), jnp.stack(s_win), jnp.stack(s_conv), jnp.stack(s_dS), jnp.stack(s_gS))
```

```python
import functools
import math

import jax
import jax.numpy as jnp
from jax import lax
from jax.experimental import pallas as pl
from jax.experimental.pallas import tpu as pltpu

f32 = jnp.float32
bf16 = jnp.bfloat16
HI = lax.Precision.HIGHEST

D_MODEL = 1024
DEPTH = 2
PAGE_SIZE = 128

A_HEADS = 4
A_DK = 128
A_DV = 128
A_QK = A_HEADS * A_DK
A_WIDTH = A_HEADS * A_DV
A_CONV_CH = 2 * A_QK + A_WIDTH
CONV_W = 4
DELTA_CHUNK = 64

B_HEADS = 8
B_KV = 2
B_HPG = B_HEADS // B_KV
B_DH = 64
B_WIDTH = B_HEADS * B_DH
N_BRANCH = 3
CMP_BLK = 32
SEL_BLK = 64
SEL_TOPK = 16
WINDOW = 512
B_SCALE = B_DH ** -0.5

C_HEADS = 4
C_DK = 128
C_DV = 256
C_QK = C_HEADS * C_DK
C_WIDTH = C_HEADS * C_DV
C_LOWRANK = 16
GLA_TAU = 16.0
GLA_CHUNK = 64
GLA_SUB = 16

D_FF = -(-(8 * D_MODEL) // (3 * 256)) * 256
ALPHA = (2 * DEPTH) ** 0.25
NEG = -1e30
FORCE = 1e9

LANES = 128
VMEM_LIMIT = 56 * 1024 * 1024


def _cparams(sem):
    return pltpu.CompilerParams(dimension_semantics=sem, vmem_limit_bytes=VMEM_LIMIT)


def _dot(a, b, precision=None):
    return jnp.dot(a, b, preferred_element_type=f32, precision=precision)


def _dot_nt(a, b, precision=None):
    return lax.dot_general(a, b, (((1,), (1,)), ((), ())), preferred_element_type=f32, precision=precision)


def _dot_tn(a, b, precision=None):
    return lax.dot_general(a, b, (((0,), (0,)), ((), ())), preferred_element_type=f32, precision=precision)


def _silu(x):
    return x * jax.nn.sigmoid(x)


def _layer_norm(x, g, b, eps=1e-5):
    mu = jnp.mean(x, -1, keepdims=True)
    xc = x - mu
    var = jnp.mean(xc * xc, -1, keepdims=True)
    return xc * lax.rsqrt(var + eps) * g + b


def _row_tile(m, pref):
    t = min(pref, m)
    assert m % t == 0
    return t


def _proj_kernel(x_ref, *refs):
    n = len(refs) // 2
    xb = x_ref[...].astype(bf16)
    for w_ref, o_ref in zip(refs[:n], refs[n:]):
        o_ref[...] = _dot(xb, w_ref[...])


def _proj(x, weights, tm=512):
    m, k = x.shape
    tm = _row_tile(m, tm)
    in_specs = [pl.BlockSpec((tm, k), lambda i: (i, 0))]
    in_specs += [pl.BlockSpec(w.shape, lambda i: (0, 0)) for w in weights]
    out_specs = [pl.BlockSpec((tm, w.shape[1]), lambda i: (i, 0)) for w in weights]
    out_shape = [jax.ShapeDtypeStruct((m, w.shape[1]), f32) for w in weights]
    return pl.pallas_call(
        _proj_kernel, grid=(m // tm,), in_specs=in_specs, out_specs=out_specs, out_shape=out_shape,
        compiler_params=_cparams(("parallel",)), name="in_proj")(x, *weights)


def _outproj_ln_kernel(*refs, n_parts):
    parts = refs[:n_parts]
    ws = refs[n_parts:2 * n_parts]
    x_ref, g_ref, b_ref, o_ref = refs[2 * n_parts:]
    y = None
    for p_ref, w_ref in zip(parts, ws):
        d = _dot(p_ref[...].astype(bf16), w_ref[...])
        y = d if y is None else y + d
    o_ref[...] = _layer_norm(ALPHA * x_ref[...] + y, g_ref[...], b_ref[...])


def _outproj_ln(parts, weights, x, g, b, tm=512):
    m, d = x.shape
    tm = _row_tile(m, tm)
    n = len(parts)
    in_specs = [pl.BlockSpec((tm, p.shape[1]), lambda i: (i, 0)) for p in parts]
    in_specs += [pl.BlockSpec(w.shape, lambda i: (0, 0)) for w in weights]
    in_specs += [pl.BlockSpec((tm, d), lambda i: (i, 0)),
                 pl.BlockSpec((1, d), lambda i: (0, 0)), pl.BlockSpec((1, d), lambda i: (0, 0))]
    return pl.pallas_call(
        functools.partial(_outproj_ln_kernel, n_parts=n), grid=(m // tm,), in_specs=in_specs,
        out_specs=pl.BlockSpec((tm, d), lambda i: (i, 0)), out_shape=jax.ShapeDtypeStruct((m, d), f32),
        compiler_params=_cparams(("parallel",)), name="out_proj_ln")(*parts, *weights, x, g, b)


FFN_COLS = 256


def _ffn_ln_kernel(x_ref, wg_ref, wu_ref, wd_ref, g_ref, b_ref, o_ref):
    x = x_ref[...]
    xb = x.astype(bf16)
    acc = jnp.zeros(x.shape, f32)
    for c in range(0, D_FF, FFN_COLS):
        hg = _dot(xb, wg_ref[:, c:c + FFN_COLS])
        hu = _dot(xb, wu_ref[:, c:c + FFN_COLS])
        h = (_silu(hg) * hu).astype(bf16)
        acc = acc + _dot(h, wd_ref[c:c + FFN_COLS, :])
    o_ref[...] = _layer_norm(ALPHA * x + acc, g_ref[...], b_ref[...])


def _ffn_ln(x, wg, wu, wd, g, b, tm=256):
    m, d = x.shape
    tm = _row_tile(m, tm)
    full = lambda a: pl.BlockSpec(a.shape, lambda i: (0, 0))
    return pl.pallas_call(
        _ffn_ln_kernel, grid=(m // tm,),
        in_specs=[pl.BlockSpec((tm, d), lambda i: (i, 0)), full(wg), full(wu), full(wd), full(g), full(b)],
        out_specs=pl.BlockSpec((tm, d), lambda i: (i, 0)), out_shape=jax.ShapeDtypeStruct((m, d), f32),
        compiler_params=_cparams(("parallel",)), name="ffn_ln")(x, wg, wu, wd, g, b)


DELTA_INV_BLK = 16


def _tri_masks(c):
    r = lax.broadcasted_iota(jnp.int32, (c, c), 0)
    s = lax.broadcasted_iota(jnp.int32, (c, c), 1)
    return r, s


def _unit_lower_inverse(a, r, s):
    c = a.shape[0]
    eye = (r == s).astype(f32)
    same_blk = (r // DELTA_INV_BLK) == (s // DELTA_INV_BLK)
    d = jnp.where(same_blk, a, 0.0)
    low = a - d
    nb = -d
    p = eye + nb
    pw = nb
    k = 2
    while k < DELTA_INV_BLK:
        pw = _dot(pw, pw, HI)
        p = p + _dot(p, pw, HI)
        k *= 2
    n = _dot(p, low, HI)
    q = eye - n
    pw = n
    k = 2
    while k < c // DELTA_INV_BLK:
        pw = _dot(pw, pw, HI)
        q = q + _dot(q, pw, HI)
        k *= 2
    return _dot(q, p, HI)


def _delta_kernel(qkv_ref, small_ref, gate_ref, cbuf_ref, cw_ref, alog_ref, dtb_ref, dng_ref, s0_ref,
                  o_ref, s_ref, xbuf, ybuf, tail, *, tt, t_valid, t_total):
    t = pl.program_id(1)
    c = DELTA_CHUNK

    @pl.when(t == 0)
    def _():
        s_ref[...] = s0_ref[...]
        tail[...] = cbuf_ref[...]

    xbuf[0:8, :] = tail[...]
    xbuf[8:8 + tt, :] = qkv_ref[...]
    tail[...] = xbuf[tt:tt + 8, :]
    y = xbuf[5:5 + tt, :] * cw_ref[0:1, :]
    for j in range(1, CONV_W):
        y = y + xbuf[5 + j:5 + j + tt, :] * cw_ref[j:j + 1, :]
    ybuf[...] = _silu(y)

    r, s = _tri_masks(c)
    causal = r >= s
    strict = r > s
    lower_ones = causal.astype(f32)
    upper_ones = (r <= s).astype(f32)
    all_ones = jnp.ones((c, c), f32)
    alog = alog_ref[...]
    dtb = dtb_ref[...]
    dng = dng_ref[...]

    def chunk_body(ci, carry):
        r0 = pl.multiple_of(ci * c, c)
        sm = small_ref[pl.ds(r0, c), :]
        beta_all = jax.nn.sigmoid(sm)
        g_all = -jnp.exp(alog) * jax.nn.softplus(sm + dtb)
        if t_valid < t_total:
            rows = t * tt + r0 + lax.broadcasted_iota(jnp.int32, (c, 1), 0)
            live = rows < t_valid
        for h in range(A_HEADS):
            qh = ybuf[pl.ds(r0, c), h * A_DK:(h + 1) * A_DK]
            kh = ybuf[pl.ds(r0, c), A_QK + h * A_DK:A_QK + (h + 1) * A_DK]
            vh = ybuf[pl.ds(r0, c), 2 * A_QK + h * A_DV:2 * A_QK + (h + 1) * A_DV]
            qh = qh * lax.rsqrt(jnp.sum(qh * qh, -1, keepdims=True) + 1e-6) * (A_DK ** -0.5)
            kh = kh * lax.rsqrt(jnp.sum(kh * kh, -1, keepdims=True) + 1e-6)
            beta = beta_all[:, h:h + 1]
            g = g_all[:, A_HEADS + h:A_HEADS + h + 1]
            if t_valid < t_total:
                qh = jnp.where(live, qh, 0.0)
                kh = jnp.where(live, kh, 0.0)
                vh = jnp.where(live, vh, 0.0)
                beta = jnp.where(live, beta, 0.0)
                g = jnp.where(live, g, 0.0)
            gb = jnp.broadcast_to(g, (c, A_DK))
            gam_c = _dot(lower_ones, gb, HI)
            gam_r = _dot(all_ones, gb[:, :c] * upper_ones, HI)
            ldec = jnp.where(causal, jnp.exp(jnp.where(causal, gam_c[:, :c] - gam_r, 0.0)), 0.0)
            kb = kh * beta
            a = jnp.where(strict, _dot_nt(kb, kh) * ldec, 0.0)
            tm = _unit_lower_inverse(a, r, s)
            eg = jnp.exp(gam_c)
            u = _dot(tm, vh * beta)
            w = _dot(tm, kb * eg)
            qk = _dot_nt(qh, kh) * ldec
            qg = qh * eg
            g_last = gam_c[c - 1:c, :]
            kdec = kh * jnp.exp(g_last - gam_c)
            gl = jnp.exp(g_last)
            st = s_ref[h]
            v_new = u - _dot(w, st)
            o = _dot(qg, st) + _dot(qk, v_new)
            s_ref[h] = st * gl + _dot_tn(kdec, v_new)
            o = o * lax.rsqrt(jnp.mean(o * o, -1, keepdims=True) + 1e-6) * dng
            gt = gate_ref[pl.ds(r0, c), h * A_DV:(h + 1) * A_DV]
            o_ref[pl.ds(r0, c), h * A_DV:(h + 1) * A_DV] = o * _silu(gt)
        return carry

    lax.fori_loop(0, tt // c, chunk_body, 0)


def _delta(qkv, small, gate, conv_buf8, conv_w, alog, dtb, dng, s0, *, t_valid, tt=512):
    b, t_total, _ = qkv.shape
    tt = _row_tile(t_total, tt)
    nt = t_total // tt
    row = lambda w: pl.BlockSpec((None, tt, w), lambda i, j: (i, j, 0))
    full2 = lambda a: pl.BlockSpec(a.shape, lambda i, j: (0, 0))
    st_spec = pl.BlockSpec((None, A_HEADS, A_DK, A_DV), lambda i, j: (i, 0, 0, 0))
    kern = functools.partial(_delta_kernel, tt=tt, t_valid=t_valid, t_total=t_total)
    return pl.pallas_call(
        kern, grid=(b, nt),
        in_specs=[row(A_CONV_CH), row(LANES), row(A_WIDTH),
                  pl.BlockSpec((None, 8, A_CONV_CH), lambda i, j: (i, 0, 0)),
                  full2(conv_w), full2(alog), full2(dtb), full2(dng), st_spec],
        out_specs=[row(A_WIDTH), st_spec],
        out_shape=[jax.ShapeDtypeStruct((b, t_total, A_WIDTH), f32),
                   jax.ShapeDtypeStruct((b, A_HEADS, A_DK, A_DV), f32)],
        scratch_shapes=[pltpu.VMEM((tt + 8, A_CONV_CH), f32), pltpu.VMEM((tt, A_CONV_CH), f32),
                        pltpu.VMEM((8, A_CONV_CH), f32)],
        compiler_params=_cparams(("parallel", "arbitrary")), name="delta_mixer",
    )(qkv, small, gate, conv_buf8, conv_w, alog, dtb, dng, s0)


def _gla_kernel(q_ref, k_ref, v_ref, r_ref, g1_ref, wg2_ref, bg2_ref, gng_ref, s0_ref,
                o_ref, s_ref, st_ref, la_ref, *, tt, t_valid, t_total):
    t = pl.program_id(1)
    nt = pl.num_programs(1)
    c = GLA_CHUNK

    @pl.when(t == 0)
    def _():
        for h in range(C_HEADS):
            st_ref[h] = s0_ref[h].T

    la = jax.nn.log_sigmoid(_dot(g1_ref[...].astype(bf16), wg2_ref[...]) + bg2_ref[...]) / GLA_TAU
    if t_valid < t_total:
        rows = t * tt + lax.broadcasted_iota(jnp.int32, (tt, 1), 0)
        la = jnp.where(rows < t_valid, la, 0.0)
    la_ref[...] = la

    r, s = _tri_masks(c)
    lower_ones = (r >= s).astype(f32)
    rs = lax.broadcasted_iota(jnp.int32, (GLA_SUB, c), 0)
    ss = lax.broadcasted_iota(jnp.int32, (GLA_SUB, c), 1)
    krow = lax.broadcasted_iota(jnp.int32, (c, 1), 0)
    gng = gng_ref[...]

    def chunk_body(ci, carry):
        r0 = pl.multiple_of(ci * c, c)
        for h in range(C_HEADS):
            qh = q_ref[pl.ds(r0, c), h * C_DK:(h + 1) * C_DK] * (C_DK ** -0.5)
            kh = k_ref[pl.ds(r0, c), h * C_DK:(h + 1) * C_DK]
            vh = v_ref[pl.ds(r0, c), h * C_DV:(h + 1) * C_DV]
            bc = _dot(lower_ones, la_ref[pl.ds(r0, c), h * C_DK:(h + 1) * C_DK], HI)
            st = st_ref[h]
            inter = _dot_nt(qh * jnp.exp(bc), st)
            intra = []
            for a in range(c // GLA_SUB):
                lo, hi = a * GLA_SUB, (a + 1) * GLA_SUB
                bref = bc[lo - 1:lo, :] if a > 0 else jnp.zeros((1, C_DK), f32)
                qa = qh[lo:hi, :] * jnp.exp(bc[lo:hi, :] - bref)
                seen = krow < hi
                ka = jnp.where(seen, kh * jnp.exp(jnp.where(seen, bref - bc, 0.0)), 0.0)
                att = jnp.where(rs + lo >= ss, _dot_nt(qa, ka), 0.0)
                intra.append(_dot(att, vh))
            o = inter + jnp.concatenate(intra, axis=0)
            b_last = bc[c - 1:c, :]
            kdec = kh * jnp.exp(b_last - bc)
            st_ref[h] = st * jnp.exp(b_last) + _dot_tn(vh, kdec)
            o = o * lax.rsqrt(jnp.mean(o * o, -1, keepdims=True) + 1e-6) * gng
            rh = r_ref[pl.ds(r0, c), h * C_DV:(h + 1) * C_DV]
            o_ref[pl.ds(r0, c), h * C_DV:(h + 1) * C_DV] = o * _silu(rh)
        return carry

    lax.fori_loop(0, tt // c, chunk_body, 0)

    @pl.when(t == nt - 1)
    def _():
        for h in range(C_HEADS):
            s_ref[h] = st_ref[h].T


def _gla(q, k, v, r, g1, wg2, bg2, gng, s0, *, t_valid, tt=512):
    b, t_total, _ = q.shape
    tt = _row_tile(t_total, tt)
    nt = t_total // tt
    row = lambda w: pl.BlockSpec((None, tt, w), lambda i, j: (i, j, 0))
    full2 = lambda a: pl.BlockSpec(a.shape, lambda i, j: (0, 0))
    st_spec = pl.BlockSpec((None, C_HEADS, C_DK, C_DV), lambda i, j: (i, 0, 0, 0))
    kern = functools.partial(_gla_kernel, tt=tt, t_valid=t_valid, t_total=t_total)
    return pl.pallas_call(
        kern, grid=(b, nt),
        in_specs=[row(C_QK), row(C_QK), row(C_WIDTH), row(C_WIDTH), row(LANES),
                  full2(wg2), full2(bg2), full2(gng), st_spec],
        out_specs=[row(C_WIDTH), st_spec],
        out_shape=[jax.ShapeDtypeStruct((b, t_total, C_WIDTH), f32),
                   jax.ShapeDtypeStruct((b, C_HEADS, C_DK, C_DV), f32)],
        scratch_shapes=[pltpu.VMEM((C_HEADS, C_DV, C_DK), f32), pltpu.VMEM((tt, C_QK), f32)],
        compiler_params=_cparams(("parallel", "arbitrary")), name="gla_mixer",
    )(q, k, v, r, g1, wg2, bg2, gng, s0)


GB_LANE0 = 2 * A_HEADS


def _masked_softmax(s, mask):
    sm = jnp.where(mask, s, NEG)
    e = jnp.exp(sm - jnp.max(sm, -1, keepdims=True))
    p = e / jnp.sum(e, -1, keepdims=True)
    return jnp.where(mask, p, 0.0)


def _topk_mask(score, n_cand, k):
    lane = lax.broadcasted_iota(jnp.int32, score.shape, score.ndim - 1)
    rank = jnp.zeros(score.shape, jnp.int32)
    for i in range(n_cand):
        si = score[..., i:i + 1]
        ahead = (si > score) | ((si == score) & (i < lane))
        rank = rank + ahead.astype(jnp.int32)
    return ((rank < k) & (lane < n_cand)).astype(f32)


def _block_scores(imp, qpos, n_blk):
    blk = lax.broadcasted_iota(jnp.int32, imp.shape, imp.ndim - 1)
    cur = qpos // SEL_BLK
    valid = blk * SEL_BLK <= qpos
    forced = (blk == 0) | (blk == cur) | (blk == cur - 1)
    return jnp.where(forced, FORCE, jnp.where(valid, imp, NEG))


def _split_cmp_index(col, half):
    return jnp.where(col < half, 2 * col, 2 * (col - half) + 1)


NSA_TQ = 128


def _nsa_prompt_kernel(q_ref, rows_ref, win_ref, small_ref, e_ref, o_ref, cb_ref, selx_ref, *, t_total):
    i = pl.program_id(1)
    tq = NSA_TQ
    n_sel = t_total // SEL_BLK
    half = n_sel
    hq = B_HPG * tq

    @pl.when(i == 0)
    def _():
        x = rows_ref[:, 0:2 * B_KV * B_DH].reshape(n_sel, SEL_BLK, 2 * B_KV * B_DH)
        cb_ref[0:half, :] = jnp.sum(x[:, :CMP_BLK, :], axis=1) * (1.0 / CMP_BLK)
        cb_ref[half:2 * half, :] = jnp.sum(x[:, CMP_BLK:, :], axis=1) * (1.0 / CMP_BLK)

    qpos = i * tq + lax.broadcasted_iota(jnp.int32, (tq, 1), 0)
    qpos4 = jnp.concatenate([qpos] * B_HPG, axis=0)
    gates = jax.nn.sigmoid(small_ref[...])
    kcol = lax.broadcasted_iota(jnp.int32, (1, tq), 1)
    outs = []
    for g in range(B_KV):
        q4 = jnp.concatenate(
            [q_ref[:, (g * B_HPG + h) * B_DH:(g * B_HPG + h + 1) * B_DH] for h in range(B_HPG)], axis=0) * B_SCALE
        kc = cb_ref[:, g * B_DH:(g + 1) * B_DH]
        vc = cb_ref[:, (B_KV + g) * B_DH:(B_KV + g + 1) * B_DH]
        s = _dot_nt(q4, kc)
        cidx = _split_cmp_index(lax.broadcasted_iota(jnp.int32, (1, 2 * half), 1), half)
        p = _masked_softmax(s, (cidx + 1) * CMP_BLK <= qpos4 + 1)
        o_cmp = _dot(p, vc)
        imp = p[0:tq]
        for h in range(1, B_HPG):
            imp = imp + p[h * tq:(h + 1) * tq]
        imp = imp[:, :half] + imp[:, half:]
        sel = _topk_mask(_block_scores(imp, qpos, n_sel), n_sel, min(SEL_TOPK, n_sel))
        selx_ref[...] = _dot(sel.astype(bf16), e_ref[...])

        def attend(j, carry, kv_ref, k_lane, v_lane, mask_fn):
            m, l, acc = carry
            k0 = pl.multiple_of(j * tq, tq)
            k = kv_ref[pl.ds(k0, tq), k_lane:k_lane + B_DH]
            v = kv_ref[pl.ds(k0, tq), v_lane:v_lane + B_DH]
            sc = _dot_nt(q4, k)
            mask = mask_fn(k0)
            mask4 = jnp.concatenate([mask] * B_HPG, axis=0)
            sc = jnp.where(mask4, sc, NEG)
            m_new = jnp.maximum(m, jnp.max(sc, -1, keepdims=True))
            pe = jnp.where(mask4, jnp.exp(sc - m_new), 0.0)
            alpha = jnp.exp(m - m_new)
            l = alpha * l + jnp.sum(pe, -1, keepdims=True)
            acc = alpha * acc + _dot(pe, v)
            return m_new, l, acc

        init = (jnp.full((hq, 1), NEG, f32), jnp.zeros((hq, 1), f32), jnp.zeros((hq, B_DH), f32))

        def sel_mask(k0):
            return (selx_ref[:, pl.ds(k0, tq)] > 0.5) & (k0 + kcol <= qpos)

        def win_mask(k0):
            d = qpos - (k0 + kcol)
            return (d >= 0) & (d < WINDOW)

        k_sel, v_sel = (2 * B_KV + g) * B_DH, (3 * B_KV + g) * B_DH
        _, l_s, acc_s = lax.fori_loop(
            0, i + 1, functools.partial(attend, kv_ref=rows_ref, k_lane=k_sel, v_lane=v_sel, mask_fn=sel_mask), init)
        o_sel = acc_s / l_s
        k_win, v_win = g * B_DH, (B_KV + g) * B_DH
        _, l_w, acc_w = lax.fori_loop(
            jnp.maximum(i - WINDOW // tq, 0), i + 1,
            functools.partial(attend, kv_ref=win_ref, k_lane=k_win, v_lane=v_win, mask_fn=win_mask), init)
        o_win = acc_w / l_w
        for h in range(B_HPG):
            hh = g * B_HPG + h
            c0 = GB_LANE0 + hh * N_BRANCH
            rs = slice(h * tq, (h + 1) * tq)
            outs.append(gates[:, c0:c0 + 1] * o_cmp[rs] + gates[:, c0 + 1:c0 + 2] * o_sel[rs]
                        + gates[:, c0 + 2:c0 + 3] * o_win[rs])
    o_ref[...] = jnp.concatenate(outs, axis=1)


def _nsa_prompt(qb, rows, win, small, expand):
    b, t_total, _ = qb.shape
    assert t_total % NSA_TQ == 0 and t_total % SEL_BLK == 0 and WINDOW % NSA_TQ == 0
    nq = t_total // NSA_TQ
    n_sel = t_total // SEL_BLK
    tile = lambda w: pl.BlockSpec((None, NSA_TQ, w), lambda bi, i: (bi, i, 0))
    seq = lambda w: pl.BlockSpec((None, t_total, w), lambda bi, i: (bi, 0, 0))
    return pl.pallas_call(
        functools.partial(_nsa_prompt_kernel, t_total=t_total), grid=(b, nq),
        in_specs=[tile(B_WIDTH), seq(4 * B_KV * B_DH), seq(2 * B_KV * B_DH), tile(LANES),
                  pl.BlockSpec(expand.shape, lambda bi, i: (0, 0))],
        out_specs=tile(B_WIDTH), out_shape=jax.ShapeDtypeStruct((b, t_total, B_WIDTH), f32),
        scratch_shapes=[pltpu.VMEM((2 * n_sel, 2 * B_KV * B_DH), f32), pltpu.VMEM((NSA_TQ, t_total), f32)],
        compiler_params=_cparams(("parallel", "arbitrary")), name="nsa_prompt",
    )(qb, rows, win, small, expand)


def _nsa_decode_kernel(pt_ref, q_ref, rnew_ref, wnew_ref, small_ref, wbuf_ref, e_ref, cache_ref,
                       o_ref, pages, sem, *, n_pages, past):
    b = pl.program_id(0)
    nb = pl.num_programs(0)
    slot = b % 2
    n_hist_blk = past // SEL_BLK
    n_sel = -(-(past + 1) // SEL_BLK)
    half = n_hist_blk
    lanes_sel = 2 * half
    wb = wbuf_ref.shape[0]

    def page_copy(bb, p, sl):
        return pltpu.make_async_copy(cache_ref.at[pt_ref[bb, p]], pages.at[sl, p], sem.at[sl])

    def fetch(bb, sl):
        for p in range(n_pages):
            page_copy(bb, p, sl).start()

    @pl.when(b == 0)
    def _():
        fetch(0, 0)

    @pl.when(b + 1 < nb)
    def _():
        fetch(b + 1, 1 - slot)

    for p in range(n_pages):
        page_copy(b, p, slot).wait()

    hist = pages.at[slot]
    qpos = past
    row8 = lax.broadcasted_iota(jnp.int32, (B_HEADS, 1), 0)
    q8 = jnp.concatenate([q_ref[:, h * B_DH:(h + 1) * B_DH] for h in range(B_HEADS)], axis=0) * B_SCALE
    rnew = rnew_ref[...]
    wnew = wnew_ref[...]
    o_cmp, o_sel, o_win = [], [], []
    for g in range(B_KV):
        in_g = (row8 // B_HPG) == g
        xk = hist[:, :, g * B_DH:(g + 1) * B_DH].reshape(n_hist_blk, SEL_BLK, B_DH)
        xv = hist[:, :, (B_KV + g) * B_DH:(B_KV + g + 1) * B_DH].reshape(n_hist_blk, SEL_BLK, B_DH)
        inv = 1.0 / CMP_BLK
        kc = jnp.concatenate([jnp.sum(xk[:, :CMP_BLK], axis=1) * inv, jnp.sum(xk[:, CMP_BLK:], axis=1) * inv], axis=0)
        vc = jnp.concatenate([jnp.sum(xv[:, :CMP_BLK], axis=1) * inv, jnp.sum(xv[:, CMP_BLK:], axis=1) * inv], axis=0)
        s = _dot_nt(q8, kc)
        cidx = _split_cmp_index(lax.broadcasted_iota(jnp.int32, (1, lanes_sel), 1), half)
        p = _masked_softmax(s, (cidx + 1) * CMP_BLK <= qpos + 1)
        o_cmp.append(_dot(p, vc))
        imp = jnp.sum(jnp.where(in_g, p, 0.0), axis=0, keepdims=True)
        imp = imp[:, :half] + imp[:, half:]
        imp = jnp.concatenate([imp, jnp.zeros((1, lanes_sel - half), f32)], axis=1)
        sel = _topk_mask(_block_scores(imp, qpos, n_sel), n_sel, min(SEL_TOPK, n_sel))
        selx = _dot(jnp.broadcast_to(sel[:, :half], (B_HEADS, half)).astype(bf16), e_ref[...])
        new_live = sel[:, n_hist_blk:n_hist_blk + 1] > 0.5
        ks = hist[:, :, (2 * B_KV + g) * B_DH:(2 * B_KV + g + 1) * B_DH].reshape(past, B_DH)
        vs = hist[:, :, (3 * B_KV + g) * B_DH:(3 * B_KV + g + 1) * B_DH].reshape(past, B_DH)
        k_new = rnew[:, (2 * B_KV + g) * B_DH:(2 * B_KV + g + 1) * B_DH]
        v_new = rnew[:, (3 * B_KV + g) * B_DH:(3 * B_KV + g + 1) * B_DH]
        sh = jnp.where(selx > 0.5, _dot_nt(q8, ks), NEG)
        sn = jnp.where(new_live, jnp.sum(q8 * k_new, -1, keepdims=True), NEG)
        m = jnp.maximum(jnp.max(sh, -1, keepdims=True), sn)
        eh = jnp.where(selx > 0.5, jnp.exp(sh - m), 0.0)
        en = jnp.where(new_live, jnp.exp(sn - m), 0.0)
        o_sel.append((_dot(eh, vs) + en * v_new) / (jnp.sum(eh, -1, keepdims=True) + en))
        kw = wbuf_ref[:, g * B_DH:(g + 1) * B_DH]
        vw = wbuf_ref[:, (B_KV + g) * B_DH:(B_KV + g + 1) * B_DH]
        kw_new = wnew[:, g * B_DH:(g + 1) * B_DH]
        vw_new = wnew[:, (B_KV + g) * B_DH:(B_KV + g + 1) * B_DH]
        kpos = qpos - wb + lax.broadcasted_iota(jnp.int32, (1, wb), 1)
        d = qpos - kpos
        wmask = (d >= 0) & (d < WINDOW) & (kpos >= 0)
        sw = jnp.where(wmask, _dot_nt(q8, kw), NEG)
        swn = jnp.sum(q8 * kw_new, -1, keepdims=True)
        m = jnp.maximum(jnp.max(sw, -1, keepdims=True), swn)
        ew = jnp.where(wmask, jnp.exp(sw - m), 0.0)
        ewn = jnp.exp(swn - m)
        o_win.append((_dot(ew, vw) + ewn * vw_new) / (jnp.sum(ew, -1, keepdims=True) + ewn))
    lower = row8 < B_HPG
    oc = jnp.where(lower, o_cmp[0], o_cmp[1])
    os_ = jnp.where(lower, o_sel[0], o_sel[1])
    ow = jnp.where(lower, o_win[0], o_win[1])
    gates = jax.nn.sigmoid(small_ref[...])
    outs = []
    for hh in range(B_HEADS):
        c0 = GB_LANE0 + hh * N_BRANCH
        outs.append(gates[:, c0:c0 + 1] * oc[hh:hh + 1] + gates[:, c0 + 1:c0 + 2] * os_[hh:hh + 1]
                    + gates[:, c0 + 2:c0 + 3] * ow[hh:hh + 1])
    o_ref[...] = jnp.concatenate(outs, axis=1)


def _nsa_decode(page_table, qb, rows_new, win_new, small, win_buf, expand, cache):
    nb, n_pages = page_table.shape
    past = n_pages * PAGE_SIZE
    row_w = 4 * B_KV * B_DH
    one = lambda w: pl.BlockSpec((None, 1, w), lambda i, pt: (i, 0, 0))
    grid_spec = pltpu.PrefetchScalarGridSpec(
        num_scalar_prefetch=1, grid=(nb,),
        in_specs=[one(B_WIDTH), one(row_w), one(2 * B_KV * B_DH), one(LANES),
                  pl.BlockSpec((None,) + win_buf.shape[1:], lambda i, pt: (i, 0, 0)),
                  pl.BlockSpec(expand.shape, lambda i, pt: (0, 0)),
                  pl.BlockSpec(memory_space=pl.ANY)],
        out_specs=one(B_WIDTH),
        scratch_shapes=[pltpu.VMEM((2, n_pages, PAGE_SIZE, row_w), f32), pltpu.SemaphoreType.DMA((2,))])
    return pl.pallas_call(
        functools.partial(_nsa_decode_kernel, n_pages=n_pages, past=past), grid_spec=grid_spec,
        out_shape=jax.ShapeDtypeStruct((nb, 1, B_WIDTH), f32),
        compiler_params=_cparams(("arbitrary",)), name="nsa_decode",
    )(page_table, qb, rows_new, win_new, small, win_buf, expand, cache)


def _block_expand(n_blk, n_pos):
    return (jnp.arange(n_pos)[None, :] // SEL_BLK == jnp.arange(n_blk)[:, None]).astype(bf16)


SAMPLE_PAD = DELTA_CHUNK


def _even_weights(w_in):
    o = 0
    cols = {}
    for name, n in (("qkv", A_CONV_CH), ("beta", A_HEADS), ("a", A_HEADS), ("gate", A_WIDTH), ("qb", B_WIDTH),
                    ("kv", N_BRANCH * 2 * B_KV * B_DH), ("gb", B_HEADS * N_BRANCH)):
        cols[name] = w_in[:, o:o + n]
        o += n
    n_rows = 4 * B_KV * B_DH
    small = jnp.concatenate([cols["beta"], cols["a"], cols["gb"]], axis=1)
    small = jnp.pad(small, ((0, 0), (0, LANES - small.shape[1])))
    ws = [cols["qkv"], cols["gate"], cols["qb"], cols["kv"][:, :n_rows], cols["kv"][:, n_rows:], small]
    return [w.astype(bf16) for w in ws]


def _odd_weights(w_in):
    o = 0
    ws = []
    for n in (C_QK, C_QK, C_WIDTH, C_LOWRANK, C_WIDTH):
        ws.append(w_in[:, o:o + n])
        o += n
    ws[3] = jnp.pad(ws[3], ((0, 0), (0, LANES - C_LOWRANK)))
    return [w.astype(bf16) for w in ws]


def _lane_param(v, lane0):
    return jnp.zeros((1, LANES), f32).at[0, lane0:lane0 + v.shape[0]].set(v.astype(f32))


def _pad_rows(a, nb):
    return jnp.pad(a.reshape(nb, 1, -1), ((0, 0), (0, SAMPLE_PAD - 1), (0, 0)))


def kernel(x_prompt, x_sample, cache_nsa_kv, state_nsa_win, state_delta_conv, state_delta_S, state_gla_S,
           page_table, w_in_even, conv_w_delta, delta_A_log, delta_dt_bias, delta_norm_g, w_out_even,
           w_in_odd, w_gla_gate2, b_gla_gate2, gla_norm_g, w_out_odd, w_ffn_gate, w_ffn_up, w_ffn_down,
           ln_g, ln_b):
    bp, t, d = x_prompt.shape
    bs = x_sample.shape[0]
    assert x_sample.shape[1] == 1 and w_in_even.shape[0] == 1 and w_in_odd.shape[0] == 1
    n_pages = page_table.shape[1]
    past = n_pages * PAGE_SIZE
    xp = x_prompt.reshape(bp * t, d)
    xs = x_sample.reshape(bs, d)
    ln = lambda layer, j: (ln_g[layer, j].reshape(1, d), ln_b[layer, j].reshape(1, d))
    ffn_w = lambda layer: (w_ffn_gate[layer].astype(bf16), w_ffn_up[layer].astype(bf16),
                           w_ffn_down[layer].astype(bf16))

    ws = _even_weights(w_in_even[0])
    conv_w = conv_w_delta[0]
    alog = _lane_param(delta_A_log[0], A_HEADS)
    dtb = _lane_param(delta_dt_bias[0], A_HEADS)
    dng = delta_norm_g[0].reshape(1, A_DV)
    wo = w_out_even[0].astype(bf16)
    wo_parts = [wo[:A_WIDTH], wo[A_WIDTH:]]

    qkv_p, gate_p, qb_p, rows_p, win_p, small_p = _proj(xp, ws)
    r3 = lambda a: a.reshape(bp, t, -1)
    o_a_p, ds_p = _delta(r3(qkv_p), r3(small_p), r3(gate_p), jnp.zeros((bp, 8, A_CONV_CH), f32), conv_w,
                         alog, dtb, dng, jnp.zeros((bp, A_HEADS, A_DK, A_DV), f32), t_valid=t)
    o_b_p = _nsa_prompt(r3(qb_p), r3(rows_p), r3(win_p), r3(small_p), _block_expand(t // SEL_BLK, t))
    xp = _outproj_ln([o_a_p.reshape(bp * t, -1), o_b_p.reshape(bp * t, -1)], wo_parts, xp, *ln(0, 0))
    xp = _ffn_ln(xp, *ffn_w(0), *ln(0, 1))

    qkv_s, gate_s, qb_s, rows_s, win_s, small_s = _proj(xs, ws)
    conv_s = state_delta_conv[0]
    o_a_s, ds_s = _delta(_pad_rows(qkv_s, bs), _pad_rows(small_s, bs), _pad_rows(gate_s, bs),
                         jnp.pad(conv_s, ((0, 0), (8 - (CONV_W - 1), 0), (0, 0))), conv_w, alog, dtb, dng,
                         state_delta_S[0], t_valid=1)
    win_buf = state_nsa_win[0]
    wb = win_buf.shape[1]
    r1 = lambda a: a.reshape(bs, 1, -1)
    o_b_s = _nsa_decode(page_table, r1(qb_s), r1(rows_s), r1(win_s), r1(small_s), win_buf.reshape(bs, wb, -1),
                        _block_expand(past // SEL_BLK, past),
                        cache_nsa_kv.reshape(cache_nsa_kv.shape[0], PAGE_SIZE, -1))
    xs = _outproj_ln([o_a_s[:, 0], o_b_s.reshape(bs, -1)], wo_parts, xs, *ln(0, 0))
    xs = _ffn_ln(xs, *ffn_w(0), *ln(0, 1))

    kvd = (B_KV, B_DH)
    nsa_rows_p = rows_p.reshape((1, bp, t, 4) + kvd)
    nsa_win_p = r3(win_p)[:, -min(WINDOW, t):].reshape((1, bp, min(WINDOW, t), 2) + kvd)
    delta_conv_p = jnp.concatenate([jnp.zeros((bp, CONV_W - 1, A_CONV_CH), f32), r3(qkv_p)], axis=1)[:, -(CONV_W - 1):][None]
    nsa_rows_s = rows_s.reshape((1, bs, 1, 4) + kvd)
    win_cat = jnp.concatenate([win_buf, win_s.reshape((bs, 1, 2) + kvd)], axis=1)
    nsa_win_s = win_cat[:, -min(WINDOW, wb + 1):][None]
    delta_conv_s = jnp.concatenate([conv_s, qkv_s[:, None, :]], axis=1)[:, -(CONV_W - 1):][None]

    wq, wk, wv, wg1, wr = _odd_weights(w_in_odd[0])
    wg2 = jnp.pad(w_gla_gate2[0], ((0, LANES - C_LOWRANK), (0, 0))).astype(bf16)
    bg2 = b_gla_gate2[0].reshape(1, C_QK)
    gng = gla_norm_g[0].reshape(1, C_DV)
    wo1 = [w_out_odd[0].astype(bf16)]

    q_p, k_p, v_p, g1_p, rr_p = _proj(xp, [wq, wk, wv, wg1, wr])
    o_c_p, gs_p = _gla(r3(q_p), r3(k_p), r3(v_p), r3(rr_p), r3(g1_p), wg2, bg2, gng,
                       jnp.zeros((bp, C_HEADS, C_DK, C_DV), f32), t_valid=t)
    xp = _outproj_ln([o_c_p.reshape(bp * t, -1)], wo1, xp, *ln(1, 0))
    xp = _ffn_ln(xp, *ffn_w(1), *ln(1, 1))

    q_s, k_s, v_s, g1_s, rr_s = _proj(xs, [wq, wk, wv, wg1, wr])
    o_c_s, gs_s = _gla(_pad_rows(q_s, bs), _pad_rows(k_s, bs), _pad_rows(v_s, bs), _pad_rows(rr_s, bs),
                       _pad_rows(g1_s, bs), wg2, bg2, gng, state_gla_S[0], t_valid=1)
    xs = _outproj_ln([o_c_s[:, 0]], wo1, xs, *ln(1, 0))
    xs = _ffn_ln(xs, *ffn_w(1), *ln(1, 1))

    return (xp.reshape(bp, t, d), xs.reshape(bs, 1, d),
            nsa_rows_p, nsa_win_p, delta_conv_p, ds_p[None], gs_p[None],
            nsa_rows_s, nsa_win_s, delta_conv_s, ds_s[None], gs_s[None])
```

```python
import functools
import math

import jax
import jax.numpy as jnp
from jax import lax
from jax.experimental import pallas as pl
from jax.experimental.pallas import tpu as pltpu

f32 = jnp.float32
bf16 = jnp.bfloat16
HI = lax.Precision.HIGHEST

D_MODEL = 1024
DEPTH = 2
PAGE_SIZE = 128

A_HEADS = 4
A_DK = 128
A_DV = 128
A_QK = A_HEADS * A_DK
A_WIDTH = A_HEADS * A_DV
A_CONV_CH = 2 * A_QK + A_WIDTH
CONV_W = 4
DELTA_CHUNK = 64

B_HEADS = 8
B_KV = 2
B_HPG = B_HEADS // B_KV
B_DH = 64
B_WIDTH = B_HEADS * B_DH
N_BRANCH = 3
CMP_BLK = 32
SEL_BLK = 64
SEL_TOPK = 16
WINDOW = 512
B_SCALE = B_DH ** -0.5

C_HEADS = 4
C_DK = 128
C_DV = 256
C_QK = C_HEADS * C_DK
C_WIDTH = C_HEADS * C_DV
C_LOWRANK = 16
GLA_TAU = 16.0
GLA_CHUNK = 64
GLA_SUB = 16

D_FF = -(-(8 * D_MODEL) // (3 * 256)) * 256
ALPHA = (2 * DEPTH) ** 0.25
NEG = -1e30
FORCE = 1e9

LANES = 128
VMEM_LIMIT = 56 * 1024 * 1024


def _cparams(sem):
    return pltpu.CompilerParams(dimension_semantics=sem, vmem_limit_bytes=VMEM_LIMIT)


def _dot(a, b, precision=None):
    return jnp.dot(a, b, preferred_element_type=f32, precision=precision)


def _dot_nt(a, b, precision=None):
    return lax.dot_general(a, b, (((1,), (1,)), ((), ())), preferred_element_type=f32, precision=precision)


def _dot_tn(a, b, precision=None):
    return lax.dot_general(a, b, (((0,), (0,)), ((), ())), preferred_element_type=f32, precision=precision)


def _silu(x):
    return x * jax.nn.sigmoid(x)


def _layer_norm(x, g, b, eps=1e-5):
    mu = jnp.mean(x, -1, keepdims=True)
    xc = x - mu
    var = jnp.mean(xc * xc, -1, keepdims=True)
    return xc * lax.rsqrt(var + eps) * g + b


def _row_tile(m, pref):
    t = min(pref, m)
    assert m % t == 0
    return t


def _proj_kernel(x_ref, *refs):
    n = len(refs) // 2
    xb = x_ref[...].astype(bf16)
    for w_ref, o_ref in zip(refs[:n], refs[n:]):
        o_ref[...] = _dot(xb, w_ref[...])


def _proj(x, weights, tm=512):
    m, k = x.shape
    tm = _row_tile(m, tm)
    in_specs = [pl.BlockSpec((tm, k), lambda i: (i, 0))]
    in_specs += [pl.BlockSpec(w.shape, lambda i: (0, 0)) for w in weights]
    out_specs = [pl.BlockSpec((tm, w.shape[1]), lambda i: (i, 0)) for w in weights]
    out_shape = [jax.ShapeDtypeStruct((m, w.shape[1]), f32) for w in weights]
    return pl.pallas_call(
        _proj_kernel, grid=(m // tm,), in_specs=in_specs, out_specs=out_specs, out_shape=out_shape,
        compiler_params=_cparams(("parallel",)), name="in_proj")(x, *weights)


def _outproj_ln_kernel(*refs, n_parts):
    parts = refs[:n_parts]
    ws = refs[n_parts:2 * n_parts]
    x_ref, g_ref, b_ref, o_ref = refs[2 * n_parts:]
    y = None
    for p_ref, w_ref in zip(parts, ws):
        d = _dot(p_ref[...].astype(bf16), w_ref[...])
        y = d if y is None else y + d
    o_ref[...] = _layer_norm(ALPHA * x_ref[...] + y, g_ref[...], b_ref[...])


def _outproj_ln(parts, weights, x, g, b, tm=512):
    m, d = x.shape
    tm = _row_tile(m, tm)
    n = len(parts)
    in_specs = [pl.BlockSpec((tm, p.shape[1]), lambda i: (i, 0)) for p in parts]
    in_specs += [pl.BlockSpec(w.shape, lambda i: (0, 0)) for w in weights]
    in_specs += [pl.BlockSpec((tm, d), lambda i: (i, 0)),
                 pl.BlockSpec((1, d), lambda i: (0, 0)), pl.BlockSpec((1, d), lambda i: (0, 0))]
    return pl.pallas_call(
        functools.partial(_outproj_ln_kernel, n_parts=n), grid=(m // tm,), in_specs=in_specs,
        out_specs=pl.BlockSpec((tm, d), lambda i: (i, 0)), out_shape=jax.ShapeDtypeStruct((m, d), f32),
        compiler_params=_cparams(("parallel",)), name="out_proj_ln")(*parts, *weights, x, g, b)


FFN_COLS = 256


def _ffn_ln_kernel(x_ref, wg_ref, wu_ref, wd_ref, g_ref, b_ref, o_ref):
    x = x_ref[...]
    xb = x.astype(bf16)
    acc = jnp.zeros(x.shape, f32)
    for c in range(0, D_FF, FFN_COLS):
        hg = _dot(xb, wg_ref[:, c:c + FFN_COLS])
        hu = _dot(xb, wu_ref[:, c:c + FFN_COLS])
        h = (_silu(hg) * hu).astype(bf16)
        acc = acc + _dot(h, wd_ref[c:c + FFN_COLS, :])
    o_ref[...] = _layer_norm(ALPHA * x + acc, g_ref[...], b_ref[...])


def _ffn_ln(x, wg, wu, wd, g, b, tm=256):
    m, d = x.shape
    tm = _row_tile(m, tm)
    full = lambda a: pl.BlockSpec(a.shape, lambda i: (0, 0))
    return pl.pallas_call(
        _ffn_ln_kernel, grid=(m // tm,),
        in_specs=[pl.BlockSpec((tm, d), lambda i: (i, 0)), full(wg), full(wu), full(wd), full(g), full(b)],
        out_specs=pl.BlockSpec((tm, d), lambda i: (i, 0)), out_shape=jax.ShapeDtypeStruct((m, d), f32),
        compiler_params=_cparams(("parallel",)), name="ffn_ln")(x, wg, wu, wd, g, b)


DELTA_INV_BLK = 16


def _tri_masks(c):
    r = lax.broadcasted_iota(jnp.int32, (c, c), 0)
    s = lax.broadcasted_iota(jnp.int32, (c, c), 1)
    return r, s


def _split2(x):
    hi = x.astype(bf16)
    return hi, (x - hi.astype(f32)).astype(bf16)


def _split3(x):
    hi = x.astype(bf16)
    rest = x - hi.astype(f32)
    mid = rest.astype(bf16)
    return hi, mid, (rest - mid.astype(f32)).astype(bf16)


def _dot_x3(a, b):
    ah, al = _split2(a)
    bh, bl = _split2(b)
    return _dot(ah, bh) + (_dot(ah, bl) + _dot(al, bh))


def _dot_ones(a_ones, b):
    a16 = a_ones.astype(bf16)
    b1, b2, b3 = _split3(b)
    return _dot(a16, b1) + (_dot(a16, b2) + _dot(a16, b3))


def _unit_lower_inverses(mats, r, s):
    c = mats[0].shape[0]
    eye = (r == s).astype(f32)
    same_blk = (r // DELTA_INV_BLK) == (s // DELTA_INV_BLK)
    diag = [jnp.where(same_blk, a, 0.0) for a in mats]
    low = [a - d for a, d in zip(mats, diag)]
    pw = [-d for d in diag]
    p = [eye + x for x in pw]
    k = 2
    while k < DELTA_INV_BLK:
        pw = [_dot_x3(x, x) for x in pw]
        p = [pi + _dot_x3(pi, x) for pi, x in zip(p, pw)]
        k *= 2
    pw = [_dot_x3(pi, lo) for pi, lo in zip(p, low)]
    q = [eye - x for x in pw]
    k = 2
    while k < c // DELTA_INV_BLK:
        pw = [_dot_x3(x, x) for x in pw]
        q = [qi + _dot_x3(qi, x) for qi, x in zip(q, pw)]
        k *= 2
    return [_dot_x3(qi, pi) for qi, pi in zip(q, p)]


DELTA_PREP_CHUNKS = 2


def _delta_kernel(qkv_ref, small_ref, gate_ref, cbuf_ref, cw_ref, alog_ref, dtb_ref, dng_ref, s0_ref,
                  o_ref, s_ref, xbuf, ybuf, tail, u_ref, w_ref, qg_ref, kd_ref, qk_ref, gl_ref,
                  *, tt, t_valid, t_total):
    t = pl.program_id(1)
    c = DELTA_CHUNK
    n_chunks = tt // c
    cpi = math.gcd(DELTA_PREP_CHUNKS, n_chunks)

    @pl.when(t == 0)
    def _():
        s_ref[...] = s0_ref[...]
        tail[...] = cbuf_ref[...]

    xbuf[0:8, :] = tail[...]
    xbuf[8:8 + tt, :] = qkv_ref[...]
    tail[...] = xbuf[tt:tt + 8, :]

    def conv_body(ci, carry):
        r0 = pl.multiple_of(ci * c, c)
        for col in range(0, A_CONV_CH, A_QK):
            cols = slice(col, col + A_QK)
            x = xbuf[pl.ds(r0, c + 8), cols]
            first = 8 - (CONV_W - 1)
            y = x[first:first + c] * cw_ref[0:1, cols]
            for j in range(1, CONV_W):
                y = y + x[first + j:first + j + c] * cw_ref[j:j + 1, cols]
            ybuf[pl.ds(r0, c), cols] = _silu(y)
        return carry

    lax.fori_loop(0, n_chunks, conv_body, 0)

    r, s = _tri_masks(c)
    causal = r >= s
    strict = r > s
    lower_ones = causal.astype(f32)
    upper4 =jnp.concatenate([(r <= s).astype(f32)] * A_HEADS, axis=1)
    all_ones = jnp.ones((c, c), f32)
    alog = alog_ref[...]
    dtb = dtb_ref[...]
    dng = dng_ref[...]

    def prep_body(ci, carry):
        items = []
        for cc in range(cpi):
            r0 = pl.multiple_of((ci * cpi + cc) * c, c)
            sm = small_ref[pl.ds(r0, c), :]
            beta_all = jax.nn.sigmoid(sm)
            g_all = -jnp.exp(alog) * jax.nn.softplus(sm + dtb)
            if t_valid < t_total:
                live = t * tt + r0 + lax.broadcasted_iota(jnp.int32, (c, 1), 0) < t_valid
                beta_all = jnp.where(live, beta_all, 0.0)
                g_all = jnp.where(live, g_all, 0.0)
            gam_c_all = _dot_ones(lower_ones, g_all)
            g_rows = jnp.concatenate(
                [jnp.broadcast_to(g_all[:, A_HEADS + h:A_HEADS + h + 1], (c, c)) for h in range(A_HEADS)], axis=1)
            gam_r_all = _dot_ones(all_ones, g_rows * upper4)
            for h in range(A_HEADS):
                qh = ybuf[pl.ds(r0, c), h * A_DK:(h + 1) * A_DK]
                kh = ybuf[pl.ds(r0, c), A_QK + h * A_DK:A_QK + (h + 1) * A_DK]
                vh = ybuf[pl.ds(r0, c), 2 * A_QK + h * A_DV:2 * A_QK + (h + 1) * A_DV]
                qh = qh * lax.rsqrt(jnp.sum(qh * qh, -1, keepdims=True) + 1e-6) * (A_DK ** -0.5)
                kh = kh * lax.rsqrt(jnp.sum(kh * kh, -1, keepdims=True) + 1e-6)
                if t_valid < t_total:
                    qh = jnp.where(live, qh, 0.0)
                    kh = jnp.where(live, kh, 0.0)
                    vh = jnp.where(live, vh, 0.0)
                beta = beta_all[:, h:h + 1]
                gam_c = jnp.broadcast_to(gam_c_all[:, A_HEADS + h:A_HEADS + h + 1], (c, A_DK))
                diff = gam_c[:, :c] - gam_r_all[:, h * c:(h + 1) * c]
                ldec = jnp.where(causal, jnp.exp(jnp.where(causal, diff, 0.0)), 0.0)
                items.append(dict(r0=r0, h=h, q=qh, k=kh, v=vh, beta=beta, gam=gam_c, ldec=ldec, kb=kh * beta))
        amats = [jnp.where(strict, _dot_nt(it["kb"], it["k"]) * it["ldec"], 0.0) for it in items]
        tms = _unit_lower_inverses(amats, r, s)
        for it, tm in zip(items, tms):
            r0, h = it["r0"], it["h"]
            cols = slice(h * A_DK, (h + 1) * A_DK)
            eg = jnp.exp(it["gam"])
            g_last = it["gam"][c - 1:c, :]
            u_ref[pl.ds(r0, c), cols] = _dot(tm, it["v"] * it["beta"])
            w_ref[pl.ds(r0, c), cols] = _dot(tm, it["kb"] * eg)
            qg_ref[pl.ds(r0, c), cols] = it["q"] * eg
            kd_ref[pl.ds(r0, c), cols] = it["k"] * jnp.exp(g_last - it["gam"])
            qk_ref[pl.ds(r0, c), h * c:(h + 1) * c] = _dot_nt(it["q"], it["k"]) * it["ldec"]
            gl_ref[pl.ds(r0, 1), cols] = jnp.exp(g_last)
        return carry

    lax.fori_loop(0, n_chunks // cpi, prep_body, 0)

    def scan_body(ci, carry):
        r0 = pl.multiple_of(ci * c, c)
        for h in range(A_HEADS):
            cols = slice(h * A_DK, (h + 1) * A_DK)
            st = s_ref[h]
            v_new = u_ref[pl.ds(r0, c), cols] - _dot(w_ref[pl.ds(r0, c), cols], st)
            o = _dot(qg_ref[pl.ds(r0, c), cols], st) + _dot(qk_ref[pl.ds(r0, c), h * c:(h + 1) * c], v_new)
            s_ref[h] = st * gl_ref[pl.ds(r0, 1), cols] + _dot_tn(kd_ref[pl.ds(r0, c), cols], v_new)
            o = o * lax.rsqrt(jnp.mean(o * o, -1, keepdims=True) + 1e-6) * dng
            gt = gate_ref[pl.ds(r0, c), h * A_DV:(h + 1) * A_DV]
            o_ref[pl.ds(r0, c), h * A_DV:(h + 1) * A_DV] = o * _silu(gt)
        return carry

    lax.fori_loop(0, n_chunks, scan_body, 0)


def _delta(qkv, small, gate, conv_buf8, conv_w, alog, dtb, dng, s0, *, t_valid, tt=512):
    b, t_total, _ = qkv.shape
    tt = _row_tile(t_total, tt)
    nt = t_total // tt
    row = lambda w: pl.BlockSpec((None, tt, w), lambda i, j: (i, j, 0))
    full2 = lambda a: pl.BlockSpec(a.shape, lambda i, j: (0, 0))
    st_spec = pl.BlockSpec((None, A_HEADS, A_DK, A_DV), lambda i, j: (i, 0, 0, 0))
    kern = functools.partial(_delta_kernel, tt=tt, t_valid=t_valid, t_total=t_total)
    return pl.pallas_call(
        kern, grid=(b, nt),
        in_specs=[row(A_CONV_CH), row(LANES), row(A_WIDTH),
                  pl.BlockSpec((None, 8, A_CONV_CH), lambda i, j: (i, 0, 0)),
                  full2(conv_w), full2(alog), full2(dtb), full2(dng), st_spec],
        out_specs=[row(A_WIDTH), st_spec],
        out_shape=[jax.ShapeDtypeStruct((b, t_total, A_WIDTH), f32),
                   jax.ShapeDtypeStruct((b, A_HEADS, A_DK, A_DV), f32)],
        scratch_shapes=[pltpu.VMEM((tt + 8, A_CONV_CH), f32), pltpu.VMEM((tt, A_CONV_CH), f32),
                        pltpu.VMEM((8, A_CONV_CH), f32)]
        + [pltpu.VMEM((tt, A_QK), f32)] * 4
        + [pltpu.VMEM((tt, A_HEADS * DELTA_CHUNK), f32), pltpu.VMEM((tt, A_QK), f32)],
        compiler_params=_cparams(("parallel", "arbitrary")), name="delta_mixer",
    )(qkv, small, gate, conv_buf8, conv_w, alog, dtb, dng, s0)


def _gla_kernel(q_ref, k_ref, v_ref, r_ref, g1_ref, wg2_ref, bg2_ref, gng_ref, s0_ref,
                o_ref, s_ref, st_ref, la_ref, qe_ref, gl_ref, ds_ref, *, tt, t_valid, t_total):
    t = pl.program_id(1)
    nt = pl.num_programs(1)
    c = GLA_CHUNK
    heads = range(C_HEADS)
    kcols = [slice(h * C_DK, (h + 1) * C_DK) for h in heads]
    vcols = [slice(h * C_DV, (h + 1) * C_DV) for h in heads]

    @pl.when(t == 0)
    def _():
        for h in range(C_HEADS):
            st_ref[h] = s0_ref[h].T

    la = jax.nn.log_sigmoid(_dot(g1_ref[...].astype(bf16), wg2_ref[...]) + bg2_ref[...]) / GLA_TAU
    if t_valid < t_total:
        rows = t * tt + lax.broadcasted_iota(jnp.int32, (tt, 1), 0)
        la = jnp.where(rows < t_valid, la, 0.0)
    la_ref[...] = la

    r, s = _tri_masks(c)
    lower_ones = (r >= s).astype(f32)
    rs = lax.broadcasted_iota(jnp.int32, (GLA_SUB, c), 0)
    ss = lax.broadcasted_iota(jnp.int32, (GLA_SUB, c), 1)
    krow = lax.broadcasted_iota(jnp.int32, (c, 1), 0)
    gng = gng_ref[...]

    def prep_body(ci, carry):
        r0 = pl.multiple_of(ci * c, c)
        rows = pl.ds(r0, c)
        bc_all = _dot_ones(lower_ones, la_ref[rows, :])
        qs = [q_ref[rows, kcols[h]] * (C_DK ** -0.5) for h in heads]
        ks = [k_ref[rows, kcols[h]] for h in heads]
        vs = [v_ref[rows, vcols[h]] for h in heads]
        bcs = [bc_all[:, kcols[h]] for h in heads]
        for h in heads:
            qe_ref[rows, kcols[h]] = qs[h] * jnp.exp(bcs[h])
        for a in range(c // GLA_SUB):
            lo, hi = a * GLA_SUB, (a + 1) * GLA_SUB
            seen = krow < hi
            brefs = [bcs[h][lo - 1:lo, :] if a > 0 else jnp.zeros((1, C_DK), f32) for h in heads]
            qa = [qs[h][lo:hi, :] * jnp.exp(bcs[h][lo:hi, :] - brefs[h]) for h in heads]
            ka = [jnp.where(seen, ks[h] * jnp.exp(jnp.where(seen, brefs[h] - bcs[h], 0.0)), 0.0) for h in heads]
            att = [jnp.where(rs + lo >= ss, _dot_nt(qa[h], ka[h]), 0.0) for h in heads]
            for h in heads:
                o_ref[pl.ds(r0 + lo, GLA_SUB), vcols[h]] = _dot(att[h], vs[h])
        for h in heads:
            b_last = bcs[h][c - 1:c, :]
            gl_ref[pl.ds(r0, 1), kcols[h]] = jnp.exp(b_last)
            ds_ref[ci, h] = _dot_tn(vs[h], ks[h] * jnp.exp(b_last - bcs[h]))
        return carry

    lax.fori_loop(0, tt // c, prep_body, 0)

    def scan_body(ci, carry):
        r0 = pl.multiple_of(ci * c, c)
        rows = pl.ds(r0, c)
        for h in heads:
            st = st_ref[h]
            o = _dot_nt(qe_ref[rows, kcols[h]], st) + o_ref[rows, vcols[h]]
            st_ref[h] = st * gl_ref[pl.ds(r0, 1), kcols[h]] + ds_ref[ci, h]
            o = o * lax.rsqrt(jnp.mean(o * o, -1, keepdims=True) + 1e-6) * gng
            o_ref[rows, vcols[h]] = o * _silu(r_ref[rows, vcols[h]])
        return carry

    lax.fori_loop(0, tt // c, scan_body, 0)

    @pl.when(t == nt - 1)
    def _():
        for h in range(C_HEADS):
            s_ref[h] = st_ref[h].T


def _gla(q, k, v, r, g1, wg2, bg2, gng, s0, *, t_valid, tt=512):
    b, t_total, _ = q.shape
    tt = _row_tile(t_total, tt)
    nt = t_total // tt
    row = lambda w: pl.BlockSpec((None, tt, w), lambda i, j: (i, j, 0))
    full2 = lambda a: pl.BlockSpec(a.shape, lambda i, j: (0, 0))
    st_spec = pl.BlockSpec((None, C_HEADS, C_DK, C_DV), lambda i, j: (i, 0, 0, 0))
    kern = functools.partial(_gla_kernel, tt=tt, t_valid=t_valid, t_total=t_total)
    return pl.pallas_call(
        kern, grid=(b, nt),
        in_specs=[row(C_QK), row(C_QK), row(C_WIDTH), row(C_WIDTH), row(LANES),
                  full2(wg2), full2(bg2), full2(gng), st_spec],
        out_specs=[row(C_WIDTH), st_spec],
        out_shape=[jax.ShapeDtypeStruct((b, t_total, C_WIDTH), f32),
                   jax.ShapeDtypeStruct((b, C_HEADS, C_DK, C_DV), f32)],
        scratch_shapes=[pltpu.VMEM((C_HEADS, C_DV, C_DK), f32), pltpu.VMEM((tt, C_QK), f32),
                        pltpu.VMEM((tt, C_QK), f32), pltpu.VMEM((tt, C_QK), f32),
                        pltpu.VMEM((tt // GLA_CHUNK, C_HEADS, C_DV, C_DK), f32)],
        compiler_params=_cparams(("parallel", "arbitrary")), name="gla_mixer",
    )(q, k, v, r, g1, wg2, bg2, gng, s0)


GB_LANE0 = 2 * A_HEADS


def _masked_softmax(s, mask):
    sm = jnp.where(mask, s, NEG)
    e = jnp.exp(sm - jnp.max(sm, -1, keepdims=True))
    p = e / jnp.sum(e, -1, keepdims=True)
    return jnp.where(mask, p, 0.0)


def _topk_mask(score, n_cand, k):
    lane = lax.broadcasted_iota(jnp.int32, score.shape, score.ndim - 1)
    rank = jnp.zeros(score.shape, jnp.int32)
    for i in range(n_cand):
        si = score[..., i:i + 1]
        ahead = (si > score) | ((si == score) & (i < lane))
        rank = rank + ahead.astype(jnp.int32)
    return ((rank < k) & (lane < n_cand)).astype(f32)


def _block_scores(imp, qpos, n_blk):
    blk = lax.broadcasted_iota(jnp.int32, imp.shape, imp.ndim - 1)
    cur = qpos // SEL_BLK
    valid = blk * SEL_BLK <= qpos
    forced = (blk == 0) | (blk == cur) | (blk == cur - 1)
    return jnp.where(forced, FORCE, jnp.where(valid, imp, NEG))


def _split_cmp_index(col, half):
    return jnp.where(col < half, 2 * col, 2 * (col - half) + 1)


NSA_TQ = 128
NSA_TK = 512


def _topk_mask_rows(score, k):
    n = score.shape[0]
    row = lax.broadcasted_iota(jnp.int32, score.shape, 0)
    rank = jnp.zeros(score.shape, jnp.int32)
    for i in range(n):
        si = score[i:i + 1, :]
        ahead = (si > score) | ((si == score) & (i < row))
        rank = rank + ahead.astype(jnp.int32)
    return (rank < k).astype(f32)


def _nsa_prompt_kernel(q_ref, rows_ref, win_ref, small_ref, o_ref, cb_ref, sel_ref, *, t_total):
    i = pl.program_id(1)
    tq = NSA_TQ
    tk = min(NSA_TK, t_total)
    tw = min(WINDOW + tq, t_total)
    n_sel = t_total // SEL_BLK
    half = n_sel
    hq = B_HPG * tq
    blk_per_tile = tk // SEL_BLK

    @pl.when(i == 0)
    def _():
        x = rows_ref[:, 0:2 * B_KV * B_DH].reshape(n_sel, SEL_BLK, 2 * B_KV * B_DH)
        cb_ref[0:half, :] = jnp.sum(x[:, :CMP_BLK, :], axis=1) * (1.0 / CMP_BLK)
        cb_ref[half:2 * half, :] = jnp.sum(x[:, CMP_BLK:, :], axis=1) * (1.0 / CMP_BLK)

    qpos = i * tq + lax.broadcasted_iota(jnp.int32, (1, tq), 1)
    qpos4 = jnp.concatenate([qpos] * B_HPG, axis=1)
    gates_t = jax.nn.sigmoid(small_ref[...].T)
    krow = lax.broadcasted_iota(jnp.int32, (tk, 1), 0)
    wrow = lax.broadcasted_iota(jnp.int32, (tw, 1), 0)
    n_causal = (i * tq + tq - 1) // tk + 1
    groups = range(B_KV)
    q4 = [jnp.concatenate([q_ref[:, (g * B_HPG + h) * B_DH:(g * B_HPG + h + 1) * B_DH] for h in range(B_HPG)],
                          axis=0) * B_SCALE for g in groups]

    o_cmp = []
    for g in groups:
        kc = cb_ref[:, g * B_DH:(g + 1) * B_DH]
        vc = cb_ref[:, (B_KV + g) * B_DH:(B_KV + g + 1) * B_DH]
        st = _dot_nt(kc, q4[g])
        cidx = _split_cmp_index(lax.broadcasted_iota(jnp.int32, (2 * half, 1), 0), half)
        cmask = (cidx + 1) * CMP_BLK <= qpos4 + 1
        sm = jnp.where(cmask, st, NEG)
        e = jnp.exp(sm - jnp.max(sm, axis=0, keepdims=True))
        p = jnp.where(cmask, e / jnp.sum(e, axis=0, keepdims=True), 0.0)
        o_cmp.append(_dot_tn(vc, p))
        imp = p[:, 0:tq]
        for h in range(1, B_HPG):
            imp = imp + p[:, h * tq:(h + 1) * tq]
        imp = imp[:half] + imp[half:]
        blk = lax.broadcasted_iota(jnp.int32, (n_sel, 1), 0)
        cur = qpos // SEL_BLK
        forced = (blk == 0) | (blk == cur) | (blk == cur - 1)
        score = jnp.where(forced, FORCE, jnp.where(blk * SEL_BLK <= qpos, imp, NEG))
        sel_ref[g] = _topk_mask_rows(score, min(SEL_TOPK, n_sel))

    def attend(qg, k, v, live, carry):
        m, l, acc = carry
        bias = jnp.where(live, 0.0, NEG)
        sc = _dot_nt(k, qg) + jnp.concatenate([bias] * B_HPG, axis=1)
        m_new = jnp.maximum(m, jnp.max(sc, axis=0, keepdims=True))
        pe = jnp.exp(sc - m_new)
        alpha = jnp.exp(m - m_new)
        l = alpha * l + jnp.sum(pe, axis=0, keepdims=True)
        acc = alpha * acc + _dot_tn(v, pe)
        return m_new, l, acc

    init = (jnp.full((1, hq), NEG, f32), jnp.zeros((1, hq), f32), jnp.zeros((B_DH, hq), f32))

    def sel_step(j, carries):
        k0 = pl.multiple_of(j * tk, tk)
        causal = k0 + krow <= qpos
        out = []
        for g in groups:
            flags = sel_ref[g, pl.ds(pl.multiple_of(j * blk_per_tile, blk_per_tile), blk_per_tile), :]
            flags = jnp.concatenate(
                [jnp.broadcast_to(flags[u:u + 1, :], (SEL_BLK, tq)) for u in range(blk_per_tile)], axis=0)
            kl, vl = (2 * B_KV + g) * B_DH, (3 * B_KV + g) * B_DH
            out.append(attend(q4[g], rows_ref[pl.ds(k0, tk), kl:kl + B_DH], rows_ref[pl.ds(k0, tk), vl:vl + B_DH],
                              (flags > 0.5) & causal, carries[g]))
        return tuple(out)

    sel_out = lax.fori_loop(0, n_causal, sel_step, (init,) * B_KV)
    w0 = pl.multiple_of(jnp.maximum((i + 1) * tq - tw, 0), tq)
    d = qpos - (w0 + wrow)
    in_window = (d >= 0) & (d < WINDOW)
    out_rows = []
    for g in groups:
        kl, vl = g * B_DH, (B_KV + g) * B_DH
        _, l_w, acc_w = attend(q4[g], win_ref[pl.ds(w0, tw), kl:kl + B_DH], win_ref[pl.ds(w0, tw), vl:vl + B_DH],
                               in_window, init)
        o_win = acc_w / l_w
        o_sel = sel_out[g][2] / sel_out[g][1]
        for h in range(B_HPG):
            c0 = GB_LANE0 + (g * B_HPG + h) * N_BRANCH
            ls = slice(h * tq, (h + 1) * tq)
            out_rows.append(gates_t[c0:c0 + 1, :] * o_cmp[g][:, ls] + gates_t[c0 + 1:c0 + 2, :] * o_sel[:, ls]
                            + gates_t[c0 + 2:c0 + 3, :] * o_win[:, ls])
    o_ref[...] = jnp.concatenate(out_rows, axis=0).T


def _nsa_prompt(qb, rows, win, small):
    b, t_total, _ = qb.shape
    tk = min(NSA_TK, t_total)
    assert t_total % tk == 0 and tk % NSA_TQ == 0 and tk % SEL_BLK == 0 and WINDOW % NSA_TQ == 0
    nq = t_total // NSA_TQ
    n_sel = t_total // SEL_BLK
    tile = lambda w: pl.BlockSpec((None, NSA_TQ, w), lambda bi, i: (bi, i, 0))
    seq = lambda w: pl.BlockSpec((None, t_total, w), lambda bi, i: (bi, 0, 0))
    return pl.pallas_call(
        functools.partial(_nsa_prompt_kernel, t_total=t_total), grid=(b, nq),
        in_specs=[tile(B_WIDTH), seq(4 * B_KV * B_DH), seq(2 * B_KV * B_DH), tile(LANES)],
        out_specs=tile(B_WIDTH), out_shape=jax.ShapeDtypeStruct((b, t_total, B_WIDTH), f32),
        scratch_shapes=[pltpu.VMEM((2 * n_sel, 2 * B_KV * B_DH), f32), pltpu.VMEM((B_KV, n_sel, NSA_TQ), f32)],
        compiler_params=_cparams(("parallel", "arbitrary")), name="nsa_prompt",
    )(qb, rows, win, small)


def _nsa_decode_kernel(pt_ref, q_ref, rnew_ref, wnew_ref, small_ref, wbuf_ref, e_ref, cache_ref,
                       o_ref, pages, sem, *, n_pages, past):
    b = pl.program_id(0)
    nb = pl.num_programs(0)
    slot = b % 2
    n_hist_blk = past // SEL_BLK
    n_sel = -(-(past + 1) // SEL_BLK)
    half = n_hist_blk
    lanes_sel = 2 * half
    wb = wbuf_ref.shape[0]

    def page_copy(bb, p, sl):
        return pltpu.make_async_copy(cache_ref.at[pt_ref[bb, p]], pages.at[sl, p], sem.at[sl])

    def fetch(bb, sl):
        for p in range(n_pages):
            page_copy(bb, p, sl).start()

    @pl.when(b == 0)
    def _():
        fetch(0, 0)

    @pl.when(b + 1 < nb)
    def _():
        fetch(b + 1, 1 - slot)

    for p in range(n_pages):
        page_copy(b, p, slot).wait()

    hist = pages.at[slot]
    qpos = past
    row8 = lax.broadcasted_iota(jnp.int32, (B_HEADS, 1), 0)
    q8 = jnp.concatenate([q_ref[:, h * B_DH:(h + 1) * B_DH] for h in range(B_HEADS)], axis=0) * B_SCALE
    rnew = rnew_ref[...]
    wnew = wnew_ref[...]
    o_cmp, o_sel, o_win = [], [], []
    for g in range(B_KV):
        in_g = (row8 // B_HPG) == g
        xk = hist[:, :, g * B_DH:(g + 1) * B_DH].reshape(n_hist_blk, SEL_BLK, B_DH)
        xv = hist[:, :, (B_KV + g) * B_DH:(B_KV + g + 1) * B_DH].reshape(n_hist_blk, SEL_BLK, B_DH)
        inv = 1.0 / CMP_BLK
        kc = jnp.concatenate([jnp.sum(xk[:, :CMP_BLK], axis=1) * inv, jnp.sum(xk[:, CMP_BLK:], axis=1) * inv], axis=0)
        vc = jnp.concatenate([jnp.sum(xv[:, :CMP_BLK], axis=1) * inv, jnp.sum(xv[:, CMP_BLK:], axis=1) * inv], axis=0)
        s = _dot_nt(q8, kc)
        cidx = _split_cmp_index(lax.broadcasted_iota(jnp.int32, (1, lanes_sel), 1), half)
        p = _masked_softmax(s, (cidx + 1) * CMP_BLK <= qpos + 1)
        o_cmp.append(_dot(p, vc))
        imp = jnp.sum(jnp.where(in_g, p, 0.0), axis=0, keepdims=True)
        imp = imp[:, :half] + imp[:, half:]
        imp = jnp.concatenate([imp, jnp.zeros((1, lanes_sel - half), f32)], axis=1)
        sel = _topk_mask(_block_scores(imp, qpos, n_sel), n_sel, min(SEL_TOPK, n_sel))
        selx = _dot(jnp.broadcast_to(sel[:, :half], (B_HEADS, half)).astype(bf16), e_ref[...])
        new_live = sel[:, n_hist_blk:n_hist_blk + 1] > 0.5
        ks = hist[:, :, (2 * B_KV + g) * B_DH:(2 * B_KV + g + 1) * B_DH].reshape(past, B_DH)
        vs = hist[:, :, (3 * B_KV + g) * B_DH:(3 * B_KV + g + 1) * B_DH].reshape(past, B_DH)
        k_new = rnew[:, (2 * B_KV + g) * B_DH:(2 * B_KV + g + 1) * B_DH]
        v_new = rnew[:, (3 * B_KV + g) * B_DH:(3 * B_KV + g + 1) * B_DH]
        sh = jnp.where(selx > 0.5, _dot_nt(q8, ks), NEG)
        sn = jnp.where(new_live, jnp.sum(q8 * k_new, -1, keepdims=True), NEG)
        m = jnp.maximum(jnp.max(sh, -1, keepdims=True), sn)
        eh = jnp.where(selx > 0.5, jnp.exp(sh - m), 0.0)
        en = jnp.where(new_live, jnp.exp(sn - m), 0.0)
        o_sel.append((_dot(eh, vs) + en * v_new) / (jnp.sum(eh, -1, keepdims=True) + en))
        kw = wbuf_ref[:, g * B_DH:(g + 1) * B_DH]
        vw = wbuf_ref[:, (B_KV + g) * B_DH:(B_KV + g + 1) * B_DH]
        kw_new = wnew[:, g * B_DH:(g + 1) * B_DH]
        vw_new = wnew[:, (B_KV + g) * B_DH:(B_KV + g + 1) * B_DH]
        kpos = qpos - wb + lax.broadcasted_iota(jnp.int32, (1, wb), 1)
        d = qpos - kpos
        wmask = (d >= 0) & (d < WINDOW) & (kpos >= 0)
        sw = jnp.where(wmask, _dot_nt(q8, kw), NEG)
        swn = jnp.sum(q8 * kw_new, -1, keepdims=True)
        m = jnp.maximum(jnp.max(sw, -1, keepdims=True), swn)
        ew = jnp.where(wmask, jnp.exp(sw - m), 0.0)
        ewn = jnp.exp(swn - m)
        o_win.append((_dot(ew, vw) + ewn * vw_new) / (jnp.sum(ew, -1, keepdims=True) + ewn))
    lower = row8 < B_HPG
    oc = jnp.where(lower, o_cmp[0], o_cmp[1])
    os_ = jnp.where(lower, o_sel[0], o_sel[1])
    ow = jnp.where(lower, o_win[0], o_win[1])
    gates = jax.nn.sigmoid(small_ref[...])
    outs = []
    for hh in range(B_HEADS):
        c0 = GB_LANE0 + hh * N_BRANCH
        outs.append(gates[:, c0:c0 + 1] * oc[hh:hh + 1] + gates[:, c0 + 1:c0 + 2] * os_[hh:hh + 1]
                    + gates[:, c0 + 2:c0 + 3] * ow[hh:hh + 1])
    o_ref[...] = jnp.concatenate(outs, axis=1)


def _nsa_decode(page_table, qb, rows_new, win_new, small, win_buf, expand, cache):
    nb, n_pages = page_table.shape
    past = n_pages * PAGE_SIZE
    row_w = 4 * B_KV * B_DH
    one = lambda w: pl.BlockSpec((None, 1, w), lambda i, pt: (i, 0, 0))
    grid_spec = pltpu.PrefetchScalarGridSpec(
        num_scalar_prefetch=1, grid=(nb,),
        in_specs=[one(B_WIDTH), one(row_w), one(2 * B_KV * B_DH), one(LANES),
                  pl.BlockSpec((None,) + win_buf.shape[1:], lambda i, pt: (i, 0, 0)),
                  pl.BlockSpec(expand.shape, lambda i, pt: (0, 0)),
                  pl.BlockSpec(memory_space=pl.ANY)],
        out_specs=one(B_WIDTH),
        scratch_shapes=[pltpu.VMEM((2, n_pages, PAGE_SIZE, row_w), f32), pltpu.SemaphoreType.DMA((2,))])
    return pl.pallas_call(
        functools.partial(_nsa_decode_kernel, n_pages=n_pages, past=past), grid_spec=grid_spec,
        out_shape=jax.ShapeDtypeStruct((nb, 1, B_WIDTH), f32),
        compiler_params=_cparams(("arbitrary",)), name="nsa_decode",
    )(page_table, qb, rows_new, win_new, small, win_buf, expand, cache)


def _block_expand(n_blk, n_pos):
    return (jnp.arange(n_pos)[None, :] // SEL_BLK == jnp.arange(n_blk)[:, None]).astype(bf16)


SAMPLE_PAD = DELTA_CHUNK


def _even_weights(w_in):
    o = 0
    cols = {}
    for name, n in (("qkv", A_CONV_CH), ("beta", A_HEADS), ("a", A_HEADS), ("gate", A_WIDTH), ("qb", B_WIDTH),
                    ("kv", N_BRANCH * 2 * B_KV * B_DH), ("gb", B_HEADS * N_BRANCH)):
        cols[name] = w_in[:, o:o + n]
        o += n
    n_rows = 4 * B_KV * B_DH
    small = jnp.concatenate([cols["beta"], cols["a"], cols["gb"]], axis=1)
    small = jnp.pad(small, ((0, 0), (0, LANES - small.shape[1])))
    ws = [cols["qkv"], cols["gate"], cols["qb"], cols["kv"][:, :n_rows], cols["kv"][:, n_rows:], small]
    return [w.astype(bf16) for w in ws]


def _odd_weights(w_in):
    o = 0
    ws = []
    for n in (C_QK, C_QK, C_WIDTH, C_LOWRANK, C_WIDTH):
        ws.append(w_in[:, o:o + n])
        o += n
    ws[3] = jnp.pad(ws[3], ((0, 0), (0, LANES - C_LOWRANK)))
    return [w.astype(bf16) for w in ws]


def _lane_param(v, lane0):
    return jnp.zeros((1, LANES), f32).at[0, lane0:lane0 + v.shape[0]].set(v.astype(f32))


def _pad_rows(a, nb):
    return jnp.pad(a.reshape(nb, 1, -1), ((0, 0), (0, SAMPLE_PAD - 1), (0, 0)))


def kernel(x_prompt, x_sample, cache_nsa_kv, state_nsa_win, state_delta_conv, state_delta_S, state_gla_S,
           page_table, w_in_even, conv_w_delta, delta_A_log, delta_dt_bias, delta_norm_g, w_out_even,
           w_in_odd, w_gla_gate2, b_gla_gate2, gla_norm_g, w_out_odd, w_ffn_gate, w_ffn_up, w_ffn_down,
           ln_g, ln_b):
    bp, t, d = x_prompt.shape
    bs = x_sample.shape[0]
    assert x_sample.shape[1] == 1 and w_in_even.shape[0] == 1 and w_in_odd.shape[0] == 1
    n_pages = page_table.shape[1]
    past = n_pages * PAGE_SIZE
    xp = x_prompt.reshape(bp * t, d)
    xs = x_sample.reshape(bs, d)
    ln = lambda layer, j: (ln_g[layer, j].reshape(1, d), ln_b[layer, j].reshape(1, d))
    ffn_w = lambda layer: (w_ffn_gate[layer].astype(bf16), w_ffn_up[layer].astype(bf16),
                           w_ffn_down[layer].astype(bf16))

    ws = _even_weights(w_in_even[0])
    conv_w = conv_w_delta[0]
    alog = _lane_param(delta_A_log[0], A_HEADS)
    dtb = _lane_param(delta_dt_bias[0], A_HEADS)
    dng = delta_norm_g[0].reshape(1, A_DV)
    wo = w_out_even[0].astype(bf16)
    wo_parts = [wo[:A_WIDTH], wo[A_WIDTH:]]

    qkv_p, gate_p, qb_p, rows_p, win_p, small_p = _proj(xp, ws)
    r3 = lambda a: a.reshape(bp, t, -1)
    o_a_p, ds_p = _delta(r3(qkv_p), r3(small_p), r3(gate_p), jnp.zeros((bp, 8, A_CONV_CH), f32), conv_w,
                         alog, dtb, dng, jnp.zeros((bp, A_HEADS, A_DK, A_DV), f32), t_valid=t)
    o_b_p = _nsa_prompt(r3(qb_p), r3(rows_p), r3(win_p), r3(small_p))
    xp = _outproj_ln([o_a_p.reshape(bp * t, -1), o_b_p.reshape(bp * t, -1)], wo_parts, xp, *ln(0, 0))
    xp = _ffn_ln(xp, *ffn_w(0), *ln(0, 1))

    qkv_s, gate_s, qb_s, rows_s, win_s, small_s = _proj(xs, ws)
    conv_s = state_delta_conv[0]
    o_a_s, ds_s = _delta(_pad_rows(qkv_s, bs), _pad_rows(small_s, bs), _pad_rows(gate_s, bs),
                         jnp.pad(conv_s, ((0, 0), (8 - (CONV_W - 1), 0), (0, 0))), conv_w, alog, dtb, dng,
                         state_delta_S[0], t_valid=1)
    win_buf = state_nsa_win[0]
    wb = win_buf.shape[1]
    r1 = lambda a: a.reshape(bs, 1, -1)
    o_b_s = _nsa_decode(page_table, r1(qb_s), r1(rows_s), r1(win_s), r1(small_s), win_buf.reshape(bs, wb, -1),
                        _block_expand(past // SEL_BLK, past),
                        cache_nsa_kv.reshape(cache_nsa_kv.shape[0], PAGE_SIZE, -1))
    xs = _outproj_ln([o_a_s[:, 0], o_b_s.reshape(bs, -1)], wo_parts, xs, *ln(0, 0))
    xs = _ffn_ln(xs, *ffn_w(0), *ln(0, 1))

    kvd = (B_KV, B_DH)
    nsa_rows_p = rows_p.reshape((1, bp, t, 4) + kvd)
    nsa_win_p = r3(win_p)[:, -min(WINDOW, t):].reshape((1, bp, min(WINDOW, t), 2) + kvd)
    delta_conv_p = jnp.concatenate([jnp.zeros((bp, CONV_W - 1, A_CONV_CH), f32), r3(qkv_p)], axis=1)[:, -(CONV_W - 1):][None]
    nsa_rows_s = rows_s.reshape((1, bs, 1, 4) + kvd)
    win_cat = jnp.concatenate([win_buf, win_s.reshape((bs, 1, 2) + kvd)], axis=1)
    nsa_win_s = win_cat[:, -min(WINDOW, wb + 1):][None]
    delta_conv_s = jnp.concatenate([conv_s, qkv_s[:, None, :]], axis=1)[:, -(CONV_W - 1):][None]

    wq, wk, wv, wg1, wr = _odd_weights(w_in_odd[0])
    wg2 = jnp.pad(w_gla_gate2[0], ((0, LANES - C_LOWRANK), (0, 0))).astype(bf16)
    bg2 = b_gla_gate2[0].reshape(1, C_QK)
    gng = gla_norm_g[0].reshape(1, C_DV)
    wo1 = [w_out_odd[0].astype(bf16)]

    q_p, k_p, v_p, g1_p, rr_p = _proj(xp, [wq, wk, wv, wg1, wr])
    o_c_p, gs_p = _gla(r3(q_p), r3(k_p), r3(v_p), r3(rr_p), r3(g1_p), wg2, bg2, gng,
                       jnp.zeros((bp, C_HEADS, C_DK, C_DV), f32), t_valid=t)
    xp = _outproj_ln([o_c_p.reshape(bp * t, -1)], wo1, xp, *ln(1, 0))
    xp = _ffn_ln(xp, *ffn_w(1), *ln(1, 1))

    q_s, k_s, v_s, g1_s, rr_s = _proj(xs, [wq, wk, wv, wg1, wr])
    o_c_s, gs_s = _gla(_pad_rows(q_s, bs), _pad_rows(k_s, bs), _pad_rows(v_s, bs), _pad_rows(rr_s, bs),
                       _pad_rows(g1_s, bs), wg2, bg2, gng, state_gla_S[0], t_valid=1)
    xs = _outproj_ln([o_c_s[:, 0]], wo1, xs, *ln(1, 0))
    xs = _ffn_ln(xs, *ffn_w(1), *ln(1, 1))

    return (xp.reshape(bp, t, d), xs.reshape(bs, 1, d),
            nsa_rows_p, nsa_win_p, delta_conv_p, ds_p[None], gs_p[None],
            nsa_rows_s, nsa_win_s, delta_conv_s, ds_s[None], gs_s[None])
```

```python
import functools
import math

import jax
import jax.numpy as jnp
from jax import lax
from jax.experimental import pallas as pl
from jax.experimental.pallas import tpu as pltpu

f32 = jnp.float32
bf16 = jnp.bfloat16
HI = lax.Precision.HIGHEST

D_MODEL = 1024
DEPTH = 2
PAGE_SIZE = 128

A_HEADS = 4
A_DK = 128
A_DV = 128
A_QK = A_HEADS * A_DK
A_WIDTH = A_HEADS * A_DV
A_CONV_CH = 2 * A_QK + A_WIDTH
CONV_W = 4
DELTA_CHUNK = 64

B_HEADS = 8
B_KV = 2
B_HPG = B_HEADS // B_KV
B_DH = 64
B_WIDTH = B_HEADS * B_DH
N_BRANCH = 3
CMP_BLK = 32
SEL_BLK = 64
SEL_TOPK = 16
WINDOW = 512
B_SCALE = B_DH ** -0.5

C_HEADS = 4
C_DK = 128
C_DV = 256
C_QK = C_HEADS * C_DK
C_WIDTH = C_HEADS * C_DV
C_LOWRANK = 16
GLA_TAU = 16.0
GLA_CHUNK = 64
GLA_SUB = 16

D_FF = -(-(8 * D_MODEL) // (3 * 256)) * 256
ALPHA = (2 * DEPTH) ** 0.25
NEG = -1e30
FORCE = 1e9

LANES = 128
VMEM_LIMIT = 56 * 1024 * 1024


def _cparams(sem):
    return pltpu.CompilerParams(dimension_semantics=sem, vmem_limit_bytes=VMEM_LIMIT)


def _dot(a, b, precision=None):
    return jnp.dot(a, b, preferred_element_type=f32, precision=precision)


def _dot_nt(a, b, precision=None):
    return lax.dot_general(a, b, (((1,), (1,)), ((), ())), preferred_element_type=f32, precision=precision)


def _dot_tn(a, b, precision=None):
    return lax.dot_general(a, b, (((0,), (0,)), ((), ())), preferred_element_type=f32, precision=precision)


def _silu(x):
    return x * jax.nn.sigmoid(x)


def _layer_norm(x, g, b, eps=1e-5):
    mu = jnp.mean(x, -1, keepdims=True)
    xc = x - mu
    var = jnp.mean(xc * xc, -1, keepdims=True)
    return xc * lax.rsqrt(var + eps) * g + b


def _row_tile(m, pref):
    t = min(pref, m)
    assert m % t == 0
    return t


def _proj_kernel(x_ref, *refs, n, planes_of):
    xb = x_ref[...].astype(bf16)
    tm = x_ref.shape[0]
    for idx, (w_ref, o_ref) in enumerate(zip(refs[:n], refs[n:2 * n])):
        val = _dot(xb, w_ref[...])
        o_ref[...] = val
        if idx == planes_of:
            p_ref = refs[2 * n]
            n_planes = val.shape[1] // B_DH
            for j in range(n_planes):
                p_ref[pl.ds(j, tm, stride=n_planes), :] = val[:, j * B_DH:(j + 1) * B_DH]


def _proj(x, weights, tm=512, planes_of=None):
    m, k = x.shape
    tm = _row_tile(m, tm)
    in_specs = [pl.BlockSpec((tm, k), lambda i: (i, 0))]
    in_specs += [pl.BlockSpec(w.shape, lambda i: (0, 0)) for w in weights]
    out_specs = [pl.BlockSpec((tm, w.shape[1]), lambda i: (i, 0)) for w in weights]
    out_shape = [jax.ShapeDtypeStruct((m, w.shape[1]), f32) for w in weights]
    if planes_of is not None:
        n_planes = weights[planes_of].shape[1] // B_DH
        out_specs.append(pl.BlockSpec((tm * n_planes, B_DH), lambda i: (i, 0)))
        out_shape.append(jax.ShapeDtypeStruct((m * n_planes, B_DH), f32))
    return pl.pallas_call(
        functools.partial(_proj_kernel, n=len(weights), planes_of=planes_of),
        grid=(m // tm,), in_specs=in_specs, out_specs=out_specs, out_shape=out_shape,
        compiler_params=_cparams(("parallel",)), name="in_proj")(x, *weights)


def _outproj_ln_kernel(*refs, n_parts):
    parts = refs[:n_parts]
    ws = refs[n_parts:2 * n_parts]
    x_ref, g_ref, b_ref, o_ref = refs[2 * n_parts:]
    y = None
    for p_ref, w_ref in zip(parts, ws):
        d = _dot(p_ref[...].astype(bf16), w_ref[...])
        y = d if y is None else y + d
    o_ref[...] = _layer_norm(ALPHA * x_ref[...] + y, g_ref[...], b_ref[...])


def _outproj_ln(parts, weights, x, g, b, tm=512):
    m, d = x.shape
    tm = _row_tile(m, tm)
    n = len(parts)
    in_specs = [pl.BlockSpec((tm, p.shape[1]), lambda i: (i, 0)) for p in parts]
    in_specs += [pl.BlockSpec(w.shape, lambda i: (0, 0)) for w in weights]
    in_specs += [pl.BlockSpec((tm, d), lambda i: (i, 0)),
                 pl.BlockSpec((1, d), lambda i: (0, 0)), pl.BlockSpec((1, d), lambda i: (0, 0))]
    return pl.pallas_call(
        functools.partial(_outproj_ln_kernel, n_parts=n), grid=(m // tm,), in_specs=in_specs,
        out_specs=pl.BlockSpec((tm, d), lambda i: (i, 0)), out_shape=jax.ShapeDtypeStruct((m, d), f32),
        compiler_params=_cparams(("parallel",)), name="out_proj_ln")(*parts, *weights, x, g, b)


FFN_COLS = 256


def _ffn_ln_kernel(x_ref, wg_ref, wu_ref, wd_ref, g_ref, b_ref, o_ref):
    x = x_ref[...]
    xb = x.astype(bf16)
    acc = jnp.zeros(x.shape, f32)
    for c in range(0, D_FF, FFN_COLS):
        hg = _dot(xb, wg_ref[:, c:c + FFN_COLS])
        hu = _dot(xb, wu_ref[:, c:c + FFN_COLS])
        h = (_silu(hg) * hu).astype(bf16)
        acc = acc + _dot(h, wd_ref[c:c + FFN_COLS, :])
    o_ref[...] = _layer_norm(ALPHA * x + acc, g_ref[...], b_ref[...])


def _ffn_ln(x, wg, wu, wd, g, b, tm=256):
    m, d = x.shape
    tm = _row_tile(m, tm)
    full = lambda a: pl.BlockSpec(a.shape, lambda i: (0, 0))
    return pl.pallas_call(
        _ffn_ln_kernel, grid=(m // tm,),
        in_specs=[pl.BlockSpec((tm, d), lambda i: (i, 0)), full(wg), full(wu), full(wd), full(g), full(b)],
        out_specs=pl.BlockSpec((tm, d), lambda i: (i, 0)), out_shape=jax.ShapeDtypeStruct((m, d), f32),
        compiler_params=_cparams(("parallel",)), name="ffn_ln")(x, wg, wu, wd, g, b)


DELTA_INV_BLK = 16


def _tri_masks(c):
    r = lax.broadcasted_iota(jnp.int32, (c, c), 0)
    s = lax.broadcasted_iota(jnp.int32, (c, c), 1)
    return r, s


def _split2(x):
    hi = x.astype(bf16)
    return hi, (x - hi.astype(f32)).astype(bf16)


def _split3(x):
    hi = x.astype(bf16)
    rest = x - hi.astype(f32)
    mid = rest.astype(bf16)
    return hi, mid, (rest - mid.astype(f32)).astype(bf16)


def _dot_x3(a, b):
    ah, al = _split2(a)
    bh, bl = _split2(b)
    return _dot(ah, bh) + (_dot(ah, bl) + _dot(al, bh))


def _dot_ones(a_ones, b):
    a16 = a_ones.astype(bf16)
    b1, b2, b3 = _split3(b)
    return _dot(a16, b1) + (_dot(a16, b2) + _dot(a16, b3))


def _unit_lower_inverses(mats, r, s):
    c = mats[0].shape[0]
    eye = (r == s).astype(f32)
    same_blk = (r // DELTA_INV_BLK) == (s // DELTA_INV_BLK)
    diag = [jnp.where(same_blk, a, 0.0) for a in mats]
    low = [a - d for a, d in zip(mats, diag)]
    pw = [-d for d in diag]
    p = [eye + x for x in pw]
    k = 2
    while k < DELTA_INV_BLK:
        pw = [_dot_x3(x, x) for x in pw]
        p = [pi + _dot_x3(pi, x) for pi, x in zip(p, pw)]
        k *= 2
    pw = [_dot_x3(pi, lo) for pi, lo in zip(p, low)]
    q = [eye - x for x in pw]
    k = 2
    while k < c // DELTA_INV_BLK:
        pw = [_dot_x3(x, x) for x in pw]
        q = [qi + _dot_x3(qi, x) for qi, x in zip(q, pw)]
        k *= 2
    return [_dot_x3(qi, pi) for qi, pi in zip(q, p)]


DELTA_PREP_CHUNKS = 2


def _delta_kernel(qkv_ref, small_ref, gate_ref, cbuf_ref, cw_ref, alog_ref, dtb_ref, dng_ref, s0_ref,
                  o_ref, s_ref, xbuf, ybuf, tail, u_ref, w_ref, qg_ref, kd_ref, qk_ref, gl_ref,
                  *, tt, t_valid, t_total):
    t = pl.program_id(1)
    c = DELTA_CHUNK
    n_chunks = tt // c
    cpi = math.gcd(DELTA_PREP_CHUNKS, n_chunks)

    @pl.when(t == 0)
    def _():
        s_ref[...] = s0_ref[...]
        tail[...] = cbuf_ref[...]

    xbuf[0:8, :] = tail[...]
    xbuf[8:8 + tt, :] = qkv_ref[...]
    tail[...] = xbuf[tt:tt + 8, :]

    def conv_body(ci, carry):
        r0 = pl.multiple_of(ci * c, c)
        for col in range(0, A_CONV_CH, A_QK):
            cols = slice(col, col + A_QK)
            x = xbuf[pl.ds(r0, c + 8), cols]
            first = 8 - (CONV_W - 1)
            y = x[first:first + c] * cw_ref[0:1, cols]
            for j in range(1, CONV_W):
                y = y + x[first + j:first + j + c] * cw_ref[j:j + 1, cols]
            ybuf[pl.ds(r0, c), cols] = _silu(y)
        return carry

    lax.fori_loop(0, n_chunks, conv_body, 0)

    r, s = _tri_masks(c)
    causal = r >= s
    strict = r > s
    lower_ones = causal.astype(f32)
    upper4 =jnp.concatenate([(r <= s).astype(f32)] * A_HEADS, axis=1)
    all_ones = jnp.ones((c, c), f32)
    alog = alog_ref[...]
    dtb = dtb_ref[...]
    dng = dng_ref[...]

    def prep_body(ci, carry):
        items = []
        for cc in range(cpi):
            r0 = pl.multiple_of((ci * cpi + cc) * c, c)
            sm = small_ref[pl.ds(r0, c), :]
            beta_all = jax.nn.sigmoid(sm)
            g_all = -jnp.exp(alog) * jax.nn.softplus(sm + dtb)
            if t_valid < t_total:
                live = t * tt + r0 + lax.broadcasted_iota(jnp.int32, (c, 1), 0) < t_valid
                beta_all = jnp.where(live, beta_all, 0.0)
                g_all = jnp.where(live, g_all, 0.0)
            gam_c_all = _dot_ones(lower_ones, g_all)
            g_rows = jnp.concatenate(
                [jnp.broadcast_to(g_all[:, A_HEADS + h:A_HEADS + h + 1], (c, c)) for h in range(A_HEADS)], axis=1)
            gam_r_all = _dot_ones(all_ones, g_rows * upper4)
            for h in range(A_HEADS):
                qh = ybuf[pl.ds(r0, c), h * A_DK:(h + 1) * A_DK]
                kh = ybuf[pl.ds(r0, c), A_QK + h * A_DK:A_QK + (h + 1) * A_DK]
                vh = ybuf[pl.ds(r0, c), 2 * A_QK + h * A_DV:2 * A_QK + (h + 1) * A_DV]
                qh = qh * lax.rsqrt(jnp.sum(qh * qh, -1, keepdims=True) + 1e-6) * (A_DK ** -0.5)
                kh = kh * lax.rsqrt(jnp.sum(kh * kh, -1, keepdims=True) + 1e-6)
                if t_valid < t_total:
                    qh = jnp.where(live, qh, 0.0)
                    kh = jnp.where(live, kh, 0.0)
                    vh = jnp.where(live, vh, 0.0)
                beta = beta_all[:, h:h + 1]
                gam_c = jnp.broadcast_to(gam_c_all[:, A_HEADS + h:A_HEADS + h + 1], (c, A_DK))
                diff = gam_c[:, :c] - gam_r_all[:, h * c:(h + 1) * c]
                ldec = jnp.where(causal, jnp.exp(jnp.where(causal, diff, 0.0)), 0.0)
                items.append(dict(r0=r0, h=h, q=qh, k=kh, v=vh, beta=beta, gam=gam_c, ldec=ldec, kb=kh * beta))
        amats = [jnp.where(strict, _dot_nt(it["kb"], it["k"]) * it["ldec"], 0.0) for it in items]
        tms = _unit_lower_inverses(amats, r, s)
        for it, tm in zip(items, tms):
            r0, h = it["r0"], it["h"]
            cols = slice(h * A_DK, (h + 1) * A_DK)
            eg = jnp.exp(it["gam"])
            g_last = it["gam"][c - 1:c, :]
            u_ref[pl.ds(r0, c), cols] = _dot(tm, it["v"] * it["beta"])
            w_ref[pl.ds(r0, c), cols] = _dot(tm, it["kb"] * eg)
            qg_ref[pl.ds(r0, c), cols] = it["q"] * eg
            kd_ref[pl.ds(r0, c), cols] = it["k"] * jnp.exp(g_last - it["gam"])
            qk_ref[pl.ds(r0, c), h * c:(h + 1) * c] = _dot_nt(it["q"], it["k"]) * it["ldec"]
            gl_ref[pl.ds(r0, 1), cols] = jnp.exp(g_last)
        return carry

    lax.fori_loop(0, n_chunks // cpi, prep_body, 0)

    def scan_body(ci, carry):
        r0 = pl.multiple_of(ci * c, c)
        for h in range(A_HEADS):
            cols = slice(h * A_DK, (h + 1) * A_DK)
            st = s_ref[h]
            v_new = u_ref[pl.ds(r0, c), cols] - _dot(w_ref[pl.ds(r0, c), cols], st)
            o = _dot(qg_ref[pl.ds(r0, c), cols], st) + _dot(qk_ref[pl.ds(r0, c), h * c:(h + 1) * c], v_new)
            s_ref[h] = st * gl_ref[pl.ds(r0, 1), cols] + _dot_tn(kd_ref[pl.ds(r0, c), cols], v_new)
            o = o * lax.rsqrt(jnp.mean(o * o, -1, keepdims=True) + 1e-6) * dng
            gt = gate_ref[pl.ds(r0, c), h * A_DV:(h + 1) * A_DV]
            o_ref[pl.ds(r0, c), h * A_DV:(h + 1) * A_DV] = o * _silu(gt)
        return carry

    lax.fori_loop(0, n_chunks, scan_body, 0)


def _delta(qkv, small, gate, conv_buf8, conv_w, alog, dtb, dng, s0, *, t_valid, tt=512):
    b, t_total, _ = qkv.shape
    tt = _row_tile(t_total, tt)
    nt = t_total // tt
    row = lambda w: pl.BlockSpec((None, tt, w), lambda i, j: (i, j, 0))
    full2 = lambda a: pl.BlockSpec(a.shape, lambda i, j: (0, 0))
    st_spec = pl.BlockSpec((None, A_HEADS, A_DK, A_DV), lambda i, j: (i, 0, 0, 0))
    kern = functools.partial(_delta_kernel, tt=tt, t_valid=t_valid, t_total=t_total)
    return pl.pallas_call(
        kern, grid=(b, nt),
        in_specs=[row(A_CONV_CH), row(LANES), row(A_WIDTH),
                  pl.BlockSpec((None, 8, A_CONV_CH), lambda i, j: (i, 0, 0)),
                  full2(conv_w), full2(alog), full2(dtb), full2(dng), st_spec],
        out_specs=[row(A_WIDTH), st_spec],
        out_shape=[jax.ShapeDtypeStruct((b, t_total, A_WIDTH), f32),
                   jax.ShapeDtypeStruct((b, A_HEADS, A_DK, A_DV), f32)],
        scratch_shapes=[pltpu.VMEM((tt + 8, A_CONV_CH), f32), pltpu.VMEM((tt, A_CONV_CH), f32),
                        pltpu.VMEM((8, A_CONV_CH), f32)]
        + [pltpu.VMEM((tt, A_QK), f32)] * 4
        + [pltpu.VMEM((tt, A_HEADS * DELTA_CHUNK), f32), pltpu.VMEM((tt, A_QK), f32)],
        compiler_params=_cparams(("parallel", "arbitrary")), name="delta_mixer",
    )(qkv, small, gate, conv_buf8, conv_w, alog, dtb, dng, s0)


def _gla_kernel(q_ref, k_ref, v_ref, r_ref, g1_ref, wg2_ref, bg2_ref, gng_ref, s0_ref,
                o_ref, s_ref, st_ref, la_ref, qe_ref, gl_ref, ds_ref, *, tt, t_valid, t_total):
    t = pl.program_id(1)
    nt = pl.num_programs(1)
    c = GLA_CHUNK
    heads = range(C_HEADS)
    kcols = [slice(h * C_DK, (h + 1) * C_DK) for h in heads]
    vcols = [slice(h * C_DV, (h + 1) * C_DV) for h in heads]

    @pl.when(t == 0)
    def _():
        for h in range(C_HEADS):
            st_ref[h] = s0_ref[h].T

    la = jax.nn.log_sigmoid(_dot(g1_ref[...].astype(bf16), wg2_ref[...]) + bg2_ref[...]) / GLA_TAU
    if t_valid < t_total:
        rows = t * tt + lax.broadcasted_iota(jnp.int32, (tt, 1), 0)
        la = jnp.where(rows < t_valid, la, 0.0)
    la_ref[...] = la

    r, s = _tri_masks(c)
    lower_ones = (r >= s).astype(f32)
    rs = lax.broadcasted_iota(jnp.int32, (GLA_SUB, c), 0)
    ss = lax.broadcasted_iota(jnp.int32, (GLA_SUB, c), 1)
    krow = lax.broadcasted_iota(jnp.int32, (c, 1), 0)
    gng = gng_ref[...]

    def prep_body(ci, carry):
        r0 = pl.multiple_of(ci * c, c)
        rows = pl.ds(r0, c)
        bc_all = _dot_ones(lower_ones, la_ref[rows, :])
        qs = [q_ref[rows, kcols[h]] * (C_DK ** -0.5) for h in heads]
        ks = [k_ref[rows, kcols[h]] for h in heads]
        vs = [v_ref[rows, vcols[h]] for h in heads]
        bcs = [bc_all[:, kcols[h]] for h in heads]
        for h in heads:
            qe_ref[rows, kcols[h]] = qs[h] * jnp.exp(bcs[h])
        for a in range(c // GLA_SUB):
            lo, hi = a * GLA_SUB, (a + 1) * GLA_SUB
            seen = krow < hi
            brefs = [bcs[h][lo - 1:lo, :] if a > 0 else jnp.zeros((1, C_DK), f32) for h in heads]
            qa = [qs[h][lo:hi, :] * jnp.exp(bcs[h][lo:hi, :] - brefs[h]) for h in heads]
            ka = [jnp.where(seen, ks[h] * jnp.exp(jnp.where(seen, brefs[h] - bcs[h], 0.0)), 0.0) for h in heads]
            att = [jnp.where(rs + lo >= ss, _dot_nt(qa[h], ka[h]), 0.0) for h in heads]
            for h in heads:
                o_ref[pl.ds(r0 + lo, GLA_SUB), vcols[h]] = _dot(att[h], vs[h])
        for h in heads:
            b_last = bcs[h][c - 1:c, :]
            gl_ref[pl.ds(r0, 1), kcols[h]] = jnp.exp(b_last)
            ds_ref[ci, h] = _dot_tn(vs[h], ks[h] * jnp.exp(b_last - bcs[h]))
        return carry

    lax.fori_loop(0, tt // c, prep_body, 0)

    def scan_body(ci, carry):
        r0 = pl.multiple_of(ci * c, c)
        rows = pl.ds(r0, c)
        for h in heads:
            st = st_ref[h]
            o = _dot_nt(qe_ref[rows, kcols[h]], st) + o_ref[rows, vcols[h]]
            st_ref[h] = st * gl_ref[pl.ds(r0, 1), kcols[h]] + ds_ref[ci, h]
            o = o * lax.rsqrt(jnp.mean(o * o, -1, keepdims=True) + 1e-6) * gng
            o_ref[rows, vcols[h]] = o * _silu(r_ref[rows, vcols[h]])
        return carry

    lax.fori_loop(0, tt // c, scan_body, 0)

    @pl.when(t == nt - 1)
    def _():
        for h in range(C_HEADS):
            s_ref[h] = st_ref[h].T


def _gla(q, k, v, r, g1, wg2, bg2, gng, s0, *, t_valid, tt=512):
    b, t_total, _ = q.shape
    tt = _row_tile(t_total, tt)
    nt = t_total // tt
    row = lambda w: pl.BlockSpec((None, tt, w), lambda i, j: (i, j, 0))
    full2 = lambda a: pl.BlockSpec(a.shape, lambda i, j: (0, 0))
    st_spec = pl.BlockSpec((None, C_HEADS, C_DK, C_DV), lambda i, j: (i, 0, 0, 0))
    kern = functools.partial(_gla_kernel, tt=tt, t_valid=t_valid, t_total=t_total)
    return pl.pallas_call(
        kern, grid=(b, nt),
        in_specs=[row(C_QK), row(C_QK), row(C_WIDTH), row(C_WIDTH), row(LANES),
                  full2(wg2), full2(bg2), full2(gng), st_spec],
        out_specs=[row(C_WIDTH), st_spec],
        out_shape=[jax.ShapeDtypeStruct((b, t_total, C_WIDTH), f32),
                   jax.ShapeDtypeStruct((b, C_HEADS, C_DK, C_DV), f32)],
        scratch_shapes=[pltpu.VMEM((C_HEADS, C_DV, C_DK), f32), pltpu.VMEM((tt, C_QK), f32),
                        pltpu.VMEM((tt, C_QK), f32), pltpu.VMEM((tt, C_QK), f32),
                        pltpu.VMEM((tt // GLA_CHUNK, C_HEADS, C_DV, C_DK), f32)],
        compiler_params=_cparams(("parallel", "arbitrary")), name="gla_mixer",
    )(q, k, v, r, g1, wg2, bg2, gng, s0)


GB_LANE0 = 2 * A_HEADS


def _masked_softmax(s, mask):
    sm = jnp.where(mask, s, NEG)
    e = jnp.exp(sm - jnp.max(sm, -1, keepdims=True))
    p = e / jnp.sum(e, -1, keepdims=True)
    return jnp.where(mask, p, 0.0)


def _topk_rank(score, n_cand):
    lane = lax.broadcasted_iota(jnp.int32, score.shape, score.ndim - 1)
    rank = jnp.zeros(score.shape, jnp.int32)
    for i in range(n_cand):
        si = score[..., i:i + 1]
        ahead = (si > score) | ((si == score) & (i < lane))
        rank = rank + ahead.astype(jnp.int32)
    return rank


def _block_scores(imp, qpos, n_blk):
    blk = lax.broadcasted_iota(jnp.int32, imp.shape, imp.ndim - 1)
    cur = qpos // SEL_BLK
    valid = blk * SEL_BLK <= qpos
    forced = (blk == 0) | (blk == cur) | (blk == cur - 1)
    return jnp.where(forced, FORCE, jnp.where(valid, imp, NEG))


def _split_cmp_index(col, half):
    return jnp.where(col < half, 2 * col, 2 * (col - half) + 1)


NSA_TQ = 128
NSA_TK = 512


def _topk_mask_rows(score, k):
    n = score.shape[0]
    row = lax.broadcasted_iota(jnp.int32, score.shape, 0)
    rank = jnp.zeros(score.shape, jnp.int32)
    for i in range(n):
        si = score[i:i + 1, :]
        ahead = (si > score) | ((si == score) & (i < row))
        rank = rank + ahead.astype(jnp.int32)
    return (rank < k).astype(f32)


def _nsa_prompt_kernel(q_ref, rows_ref, win_ref, small_ref, o_ref, cb_ref, sel_ref, *, t_total):
    i = pl.program_id(1)
    tq = NSA_TQ
    tk = min(NSA_TK, t_total)
    tw = min(WINDOW + tq, t_total)
    n_sel = t_total // SEL_BLK
    half = n_sel
    hq = B_HPG * tq
    blk_per_tile = tk // SEL_BLK

    @pl.when(i == 0)
    def _():
        x = rows_ref[:, 0:2 * B_KV * B_DH].reshape(n_sel, SEL_BLK, 2 * B_KV * B_DH)
        cb_ref[0:half, :] = jnp.sum(x[:, :CMP_BLK, :], axis=1) * (1.0 / CMP_BLK)
        cb_ref[half:2 * half, :] = jnp.sum(x[:, CMP_BLK:, :], axis=1) * (1.0 / CMP_BLK)

    qpos = i * tq + lax.broadcasted_iota(jnp.int32, (1, tq), 1)
    qpos4 = jnp.concatenate([qpos] * B_HPG, axis=1)
    gates_t = jax.nn.sigmoid(small_ref[...].T)
    krow = lax.broadcasted_iota(jnp.int32, (tk, 1), 0)
    wrow = lax.broadcasted_iota(jnp.int32, (tw, 1), 0)
    n_causal = (i * tq + tq - 1) // tk + 1
    groups = range(B_KV)
    q4 = [jnp.concatenate([q_ref[:, (g * B_HPG + h) * B_DH:(g * B_HPG + h + 1) * B_DH] for h in range(B_HPG)],
                          axis=0) * B_SCALE for g in groups]

    o_cmp = []
    for g in groups:
        kc = cb_ref[:, g * B_DH:(g + 1) * B_DH]
        vc = cb_ref[:, (B_KV + g) * B_DH:(B_KV + g + 1) * B_DH]
        st = _dot_nt(kc, q4[g])
        cidx = _split_cmp_index(lax.broadcasted_iota(jnp.int32, (2 * half, 1), 0), half)
        cmask = (cidx + 1) * CMP_BLK <= qpos4 + 1
        sm = jnp.where(cmask, st, NEG)
        e = jnp.exp(sm - jnp.max(sm, axis=0, keepdims=True))
        p = jnp.where(cmask, e / jnp.sum(e, axis=0, keepdims=True), 0.0)
        o_cmp.append(_dot_tn(vc, p))
        imp = p[:, 0:tq]
        for h in range(1, B_HPG):
            imp = imp + p[:, h * tq:(h + 1) * tq]
        imp = imp[:half] + imp[half:]
        blk = lax.broadcasted_iota(jnp.int32, (n_sel, 1), 0)
        cur = qpos // SEL_BLK
        forced = (blk == 0) | (blk == cur) | (blk == cur - 1)
        score = jnp.where(forced, FORCE, jnp.where(blk * SEL_BLK <= qpos, imp, NEG))
        sel_ref[g] = _topk_mask_rows(score, min(SEL_TOPK, n_sel))

    def attend(qg, k, v, live, carry):
        m, l, acc = carry
        bias = jnp.where(live, 0.0, NEG)
        sc = _dot_nt(k, qg) + jnp.concatenate([bias] * B_HPG, axis=1)
        m_new = jnp.maximum(m, jnp.max(sc, axis=0, keepdims=True))
        pe = jnp.exp(sc - m_new)
        alpha = jnp.exp(m - m_new)
        l = alpha * l + jnp.sum(pe, axis=0, keepdims=True)
        acc = alpha * acc + _dot_tn(v, pe)
        return m_new, l, acc

    init = (jnp.full((1, hq), NEG, f32), jnp.zeros((1, hq), f32), jnp.zeros((B_DH, hq), f32))

    def sel_step(j, carries):
        k0 = pl.multiple_of(j * tk, tk)
        causal = k0 + krow <= qpos
        out = []
        for g in groups:
            flags = sel_ref[g, pl.ds(pl.multiple_of(j * blk_per_tile, blk_per_tile), blk_per_tile), :]
            flags = jnp.concatenate(
                [jnp.broadcast_to(flags[u:u + 1, :], (SEL_BLK, tq)) for u in range(blk_per_tile)], axis=0)
            kl, vl = (2 * B_KV + g) * B_DH, (3 * B_KV + g) * B_DH
            out.append(attend(q4[g], rows_ref[pl.ds(k0, tk), kl:kl + B_DH], rows_ref[pl.ds(k0, tk), vl:vl + B_DH],
                              (flags > 0.5) & causal, carries[g]))
        return tuple(out)

    sel_out = lax.fori_loop(0, n_causal, sel_step, (init,) * B_KV)
    w0 = pl.multiple_of(jnp.maximum((i + 1) * tq - tw, 0), tq)
    d = qpos - (w0 + wrow)
    in_window = (d >= 0) & (d < WINDOW)
    out_rows = []
    for g in groups:
        kl, vl = g * B_DH, (B_KV + g) * B_DH
        _, l_w, acc_w = attend(q4[g], win_ref[pl.ds(w0, tw), kl:kl + B_DH], win_ref[pl.ds(w0, tw), vl:vl + B_DH],
                               in_window, init)
        o_win = acc_w / l_w
        o_sel = sel_out[g][2] / sel_out[g][1]
        for h in range(B_HPG):
            c0 = GB_LANE0 + (g * B_HPG + h) * N_BRANCH
            ls = slice(h * tq, (h + 1) * tq)
            out_rows.append(gates_t[c0:c0 + 1, :] * o_cmp[g][:, ls] + gates_t[c0 + 1:c0 + 2, :] * o_sel[:, ls]
                            + gates_t[c0 + 2:c0 + 3, :] * o_win[:, ls])
    o_ref[...] = jnp.concatenate(out_rows, axis=0).T


def _nsa_prompt(qb, rows, win, small):
    b, t_total, _ = qb.shape
    tk = min(NSA_TK, t_total)
    assert t_total % tk == 0 and tk % NSA_TQ == 0 and tk % SEL_BLK == 0 and WINDOW % NSA_TQ == 0
    nq = t_total // NSA_TQ
    n_sel = t_total // SEL_BLK
    tile = lambda w: pl.BlockSpec((None, NSA_TQ, w), lambda bi, i: (bi, i, 0))
    seq = lambda w: pl.BlockSpec((None, t_total, w), lambda bi, i: (bi, 0, 0))
    return pl.pallas_call(
        functools.partial(_nsa_prompt_kernel, t_total=t_total), grid=(b, nq),
        in_specs=[tile(B_WIDTH), seq(4 * B_KV * B_DH), seq(2 * B_KV * B_DH), tile(LANES)],
        out_specs=tile(B_WIDTH), out_shape=jax.ShapeDtypeStruct((b, t_total, B_WIDTH), f32),
        scratch_shapes=[pltpu.VMEM((2 * n_sel, 2 * B_KV * B_DH), f32), pltpu.VMEM((B_KV, n_sel, NSA_TQ), f32)],
        compiler_params=_cparams(("parallel", "arbitrary")), name="nsa_prompt",
    )(qb, rows, win, small)


N_PLANES = 4 * B_KV
PAGE_ROWS = PAGE_SIZE * N_PLANES
DECODE_PAGES = 8


def _plane_rows(ref, plane, first_tok, n_tok, tok_stride=1):
    return ref[pl.ds(first_tok * N_PLANES + plane, n_tok, stride=tok_stride * N_PLANES), :]


def _nsa_decode_cmp_kernel(pt_ref, q_ref, wnew_ref, wbuf_ref, cache_ref, ocmp_ref, owin_ref, idx_ref,
                           pages, cm_ref, sem, *, n_pages, past):
    s = pl.program_id(0)
    ns = pl.num_programs(0)
    grp = n_pages // DECODE_PAGES
    pg = s % grp
    slot = s % 2
    n_hist_blk = past // SEL_BLK
    n_sel = -(-(past + 1) // SEL_BLK)
    half = n_hist_blk
    lanes_sel = 2 * half
    wb = wbuf_ref.shape[0]
    blk_per_step = DECODE_PAGES * PAGE_SIZE // CMP_BLK

    def page_copy(step, p, sl):
        row0 = pl.multiple_of(pt_ref[step * DECODE_PAGES + p] * PAGE_ROWS, PAGE_ROWS)
        return pltpu.make_async_copy(cache_ref.at[pl.ds(row0, PAGE_ROWS)], pages.at[sl, p], sem.at[sl])

    def fetch(step, sl):
        for p in range(DECODE_PAGES):
            page_copy(step, p, sl).start()

    @pl.when(s == 0)
    def _():
        fetch(0, 0)

    @pl.when(s + 1 < ns)
    def _():
        fetch(s + 1, 1 - slot)

    for p in range(DECODE_PAGES):
        page_copy(s, p, slot).wait()

    x = pages[slot].reshape(blk_per_step, CMP_BLK, N_PLANES, B_DH)
    means = jnp.sum(x, axis=1) * (1.0 / CMP_BLK)
    cm_ref[pl.ds(pl.multiple_of(pg * blk_per_step * N_PLANES, N_PLANES), blk_per_step * N_PLANES), :] = (
        means.reshape(blk_per_step * N_PLANES, B_DH))

    @pl.when(pg == grp - 1)
    def _():
        qpos = past
        row8 = lax.broadcasted_iota(jnp.int32, (B_HEADS, 1), 0)
        q8 = jnp.concatenate([q_ref[:, h * B_DH:(h + 1) * B_DH] for h in range(B_HEADS)], axis=0) * B_SCALE
        wnew = wnew_ref[...]
        lane = lax.broadcasted_iota(jnp.int32, (1, lanes_sel), 1)
        o_cmp, o_win, idx_rows = [], [], []
        for g in range(B_KV):
            in_g = (row8 // B_HPG) == g
            kc = jnp.concatenate([_plane_rows(cm_ref, g, 0, half, 2), _plane_rows(cm_ref, g, 1, half, 2)], axis=0)
            vc = jnp.concatenate([_plane_rows(cm_ref, B_KV + g, 0, half, 2),
                                  _plane_rows(cm_ref, B_KV + g, 1, half, 2)], axis=0)
            sc = _dot_nt(q8, kc)
            cidx = _split_cmp_index(lane, half)
            p = _masked_softmax(sc, (cidx + 1) * CMP_BLK <= qpos + 1)
            o_cmp.append(_dot(p, vc))
            imp = jnp.sum(jnp.where(in_g, p, 0.0), axis=0, keepdims=True)
            imp = imp[:, :half] + imp[:, half:]
            imp = jnp.concatenate([imp, jnp.zeros((1, lanes_sel - half), f32)], axis=1)
            rank = _topk_rank(_block_scores(imp, qpos, n_sel), n_sel)
            lane_f = lane.astype(f32)
            out_lane = lax.broadcasted_iota(jnp.int32, (1, LANES), 1)
            picked = jnp.zeros((1, LANES), f32)
            for r in range(min(SEL_TOPK, n_sel)):
                blk_r = jnp.sum(jnp.where((rank == r) & (lane < n_sel), lane_f, 0.0), axis=-1, keepdims=True)
                picked = picked + jnp.where(out_lane == r, blk_r, 0.0)
            idx_rows.append(picked)
            kw = wbuf_ref[:, g * B_DH:(g + 1) * B_DH]
            vw = wbuf_ref[:, (B_KV + g) * B_DH:(B_KV + g + 1) * B_DH]
            kw_new = wnew[:, g * B_DH:(g + 1) * B_DH]
            vw_new = wnew[:, (B_KV + g) * B_DH:(B_KV + g + 1) * B_DH]
            kpos = qpos - wb + lax.broadcasted_iota(jnp.int32, (1, wb), 1)
            d = qpos - kpos
            wmask = (d >= 0) & (d < WINDOW) & (kpos >= 0)
            sw = jnp.where(wmask, _dot_nt(q8, kw), NEG)
            swn = jnp.sum(q8 * kw_new, -1, keepdims=True)
            m = jnp.maximum(jnp.max(sw, -1, keepdims=True), swn)
            ew = jnp.where(wmask, jnp.exp(sw - m), 0.0)
            ewn = jnp.exp(swn - m)
            o_win.append((_dot(ew, vw) + ewn * vw_new) / (jnp.sum(ew, -1, keepdims=True) + ewn))
        lower = row8 < B_HPG
        ocmp_ref[...] = jnp.where(lower, o_cmp[0], o_cmp[1])
        owin_ref[...] = jnp.where(lower, o_win[0], o_win[1])
        idx_ref[...] = jnp.concatenate(idx_rows + [jnp.zeros((8 - B_KV, LANES), f32)], axis=0).astype(jnp.int32)


def _nsa_decode_sel_kernel(pt_ref, sel_ref, q_ref, rnew_ref, small_ref, ocmp_ref, owin_ref, cache_ref,
                           o_ref, blocks, sem, *, past):
    b = pl.program_id(0)
    nb = pl.num_programs(0)
    slot = b % 2
    n_hist_blk = past // SEL_BLK
    n_pick = sel_ref.shape[2]
    blk_rows = SEL_BLK * N_PLANES
    blk_per_page = PAGE_SIZE // SEL_BLK

    def block_copy(bb, g, j, sl):
        blk = jnp.minimum(sel_ref[bb, g, j], n_hist_blk - 1)
        row0 = (pt_ref[bb, blk // blk_per_page] * PAGE_SIZE + (blk % blk_per_page) * SEL_BLK) * N_PLANES
        return pltpu.make_async_copy(cache_ref.at[pl.ds(pl.multiple_of(row0, blk_rows), blk_rows)],
                                     blocks.at[sl, g * n_pick + j], sem.at[sl])

    def fetch(bb, sl):
        for g in range(B_KV):
            for j in range(n_pick):
                block_copy(bb, g, j, sl).start()

    @pl.when(b == 0)
    def _():
        fetch(0, 0)

    @pl.when(b + 1 < nb)
    def _():
        fetch(b + 1, 1 - slot)

    for g in range(B_KV):
        for j in range(n_pick):
            block_copy(b, g, j, slot).wait()

    row8 = lax.broadcasted_iota(jnp.int32, (B_HEADS, 1), 0)
    q8 = jnp.concatenate([q_ref[:, h * B_DH:(h + 1) * B_DH] for h in range(B_HEADS)], axis=0) * B_SCALE
    rnew = rnew_ref[...]
    o_sel = []
    for g in range(B_KV):
        bufs = [blocks.at[slot, g * n_pick + j] for j in range(n_pick)]
        ks = jnp.concatenate([_plane_rows(r, 2 * B_KV + g, 0, SEL_BLK) for r in bufs], axis=0)
        vs = jnp.concatenate([_plane_rows(r, 3 * B_KV + g, 0, SEL_BLK) for r in bufs], axis=0)
        picks = [sel_ref[b, g, j] for j in range(n_pick)]
        bias = jnp.concatenate(
            [jnp.broadcast_to(jnp.where(pk < n_hist_blk, 0.0, NEG).astype(f32), (1, SEL_BLK)) for pk in picks], axis=1)
        new_live = functools.reduce(jnp.logical_or, [pk >= n_hist_blk for pk in picks])
        k_new = rnew[:, (2 * B_KV + g) * B_DH:(2 * B_KV + g + 1) * B_DH]
        v_new = rnew[:, (3 * B_KV + g) * B_DH:(3 * B_KV + g + 1) * B_DH]
        sh = _dot_nt(q8, ks) + bias
        sn = jnp.sum(q8 * k_new, -1, keepdims=True) + jnp.where(new_live, 0.0, NEG).astype(f32)
        m = jnp.maximum(jnp.max(sh, -1, keepdims=True), sn)
        eh = jnp.where(bias == 0.0, jnp.exp(sh - m), 0.0)
        en = jnp.where(new_live, jnp.exp(sn - m), 0.0)
        o_sel.append((_dot(eh, vs) + en * v_new) / (jnp.sum(eh, -1, keepdims=True) + en))
    os_ = jnp.where(row8 < B_HPG, o_sel[0], o_sel[1])
    oc = ocmp_ref[...]
    ow = owin_ref[...]
    gates = jax.nn.sigmoid(small_ref[...])
    outs = []
    for hh in range(B_HEADS):
        c0 = GB_LANE0 + hh * N_BRANCH
        outs.append(gates[:, c0:c0 + 1] * oc[hh:hh + 1] + gates[:, c0 + 1:c0 + 2] * os_[hh:hh + 1]
                    + gates[:, c0 + 2:c0 + 3] * ow[hh:hh + 1])
    o_ref[...] = jnp.concatenate(outs, axis=1)


def _nsa_decode(page_table, qb, rows_new, win_new, small, win_buf, cache_rows):
    nb, n_pages = page_table.shape
    past = n_pages * PAGE_SIZE
    assert n_pages % DECODE_PAGES == 0 and past % SEL_BLK == 0
    grp = n_pages // DECODE_PAGES
    n_pick = min(SEL_TOPK, past // SEL_BLK + 1)
    seq = lambda shape: pl.BlockSpec((None,) + shape, lambda s, pt: (s // grp, 0, 0))
    cmp_spec = pltpu.PrefetchScalarGridSpec(
        num_scalar_prefetch=1, grid=(nb * grp,),
        in_specs=[seq((1, B_WIDTH)), seq((1, 2 * B_KV * B_DH)), seq(win_buf.shape[1:]),
                  pl.BlockSpec(memory_space=pl.ANY)],
        out_specs=[seq((B_HEADS, B_DH)), seq((B_HEADS, B_DH)), seq((8, LANES))],
        scratch_shapes=[pltpu.VMEM((2, DECODE_PAGES, PAGE_ROWS, B_DH), f32),
                        pltpu.VMEM((past // CMP_BLK * N_PLANES, B_DH), f32), pltpu.SemaphoreType.DMA((2,))])
    o_cmp, o_win, idx = pl.pallas_call(
        functools.partial(_nsa_decode_cmp_kernel, n_pages=n_pages, past=past), grid_spec=cmp_spec,
        out_shape=[jax.ShapeDtypeStruct((nb, B_HEADS, B_DH), f32), jax.ShapeDtypeStruct((nb, B_HEADS, B_DH), f32),
                   jax.ShapeDtypeStruct((nb, 8, LANES), jnp.int32)],
        compiler_params=_cparams(("arbitrary",)), name="nsa_decode_cmp",
    )(page_table.reshape(-1), qb, win_new, win_buf, cache_rows)
    picks = idx[:, :B_KV, :n_pick]
    one = lambda shape: pl.BlockSpec((None,) + shape, lambda i, pt, sel: (i, 0, 0))
    sel_spec = pltpu.PrefetchScalarGridSpec(
        num_scalar_prefetch=2, grid=(nb,),
        in_specs=[one((1, B_WIDTH)), one((1, 4 * B_KV * B_DH)), one((1, LANES)),
                  one((B_HEADS, B_DH)), one((B_HEADS, B_DH)), pl.BlockSpec(memory_space=pl.ANY)],
        out_specs=one((1, B_WIDTH)),
        scratch_shapes=[pltpu.VMEM((2, B_KV * n_pick, SEL_BLK * N_PLANES, B_DH), f32),
                        pltpu.SemaphoreType.DMA((2,))])
    return pl.pallas_call(
        functools.partial(_nsa_decode_sel_kernel, past=past), grid_spec=sel_spec,
        out_shape=jax.ShapeDtypeStruct((nb, 1, B_WIDTH), f32),
        compiler_params=_cparams(("arbitrary",)), name="nsa_decode_sel",
    )(page_table, picks, qb, rows_new, small, o_cmp, o_win, cache_rows)


SAMPLE_PAD = DELTA_CHUNK


def _even_weights(w_in):
    o = 0
    cols = {}
    for name, n in (("qkv", A_CONV_CH), ("beta", A_HEADS), ("a", A_HEADS), ("gate", A_WIDTH), ("qb", B_WIDTH),
                    ("kv", N_BRANCH * 2 * B_KV * B_DH), ("gb", B_HEADS * N_BRANCH)):
        cols[name] = w_in[:, o:o + n]
        o += n
    n_rows = 4 * B_KV * B_DH
    small = jnp.concatenate([cols["beta"], cols["a"], cols["gb"]], axis=1)
    small = jnp.pad(small, ((0, 0), (0, LANES - small.shape[1])))
    ws = [cols["qkv"], cols["gate"], cols["qb"], cols["kv"][:, :n_rows], cols["kv"][:, n_rows:], small]
    return [w.astype(bf16) for w in ws]


def _odd_weights(w_in):
    o = 0
    ws = []
    for n in (C_QK, C_QK, C_WIDTH, C_LOWRANK, C_WIDTH):
        ws.append(w_in[:, o:o + n])
        o += n
    ws[3] = jnp.pad(ws[3], ((0, 0), (0, LANES - C_LOWRANK)))
    return [w.astype(bf16) for w in ws]


def _lane_param(v, lane0):
    return jnp.zeros((1, LANES), f32).at[0, lane0:lane0 + v.shape[0]].set(v.astype(f32))


def _pad_rows(a, nb):
    return jnp.pad(a.reshape(nb, 1, -1), ((0, 0), (0, SAMPLE_PAD - 1), (0, 0)))


def kernel(x_prompt, x_sample, cache_nsa_kv, state_nsa_win, state_delta_conv, state_delta_S, state_gla_S,
           page_table, w_in_even, conv_w_delta, delta_A_log, delta_dt_bias, delta_norm_g, w_out_even,
           w_in_odd, w_gla_gate2, b_gla_gate2, gla_norm_g, w_out_odd, w_ffn_gate, w_ffn_up, w_ffn_down,
           ln_g, ln_b):
    bp, t, d = x_prompt.shape
    bs = x_sample.shape[0]
    assert x_sample.shape[1] == 1 and w_in_even.shape[0] == 1 and w_in_odd.shape[0] == 1
    n_pages = page_table.shape[1]
    past = n_pages * PAGE_SIZE
    xp = x_prompt.reshape(bp * t, d)
    xs = x_sample.reshape(bs, d)
    ln = lambda layer, j: (ln_g[layer, j].reshape(1, d), ln_b[layer, j].reshape(1, d))
    ffn_w = lambda layer: (w_ffn_gate[layer].astype(bf16), w_ffn_up[layer].astype(bf16),
                           w_ffn_down[layer].astype(bf16))

    ws = _even_weights(w_in_even[0])
    conv_w = conv_w_delta[0]
    alog = _lane_param(delta_A_log[0], A_HEADS)
    dtb = _lane_param(delta_dt_bias[0], A_HEADS)
    dng = delta_norm_g[0].reshape(1, A_DV)
    wo = w_out_even[0].astype(bf16)
    wo_parts = [wo[:A_WIDTH], wo[A_WIDTH:]]

    qkv_p, gate_p, qb_p, rows_p, win_p, small_p, rows_planes_p = _proj(xp, ws, planes_of=3)
    r3 = lambda a: a.reshape(bp, t, -1)
    o_a_p, ds_p = _delta(r3(qkv_p), r3(small_p), r3(gate_p), jnp.zeros((bp, 8, A_CONV_CH), f32), conv_w,
                         alog, dtb, dng, jnp.zeros((bp, A_HEADS, A_DK, A_DV), f32), t_valid=t)
    o_b_p = _nsa_prompt(r3(qb_p), r3(rows_p), r3(win_p), r3(small_p))
    xp = _outproj_ln([o_a_p.reshape(bp * t, -1), o_b_p.reshape(bp * t, -1)], wo_parts, xp, *ln(0, 0))
    xp = _ffn_ln(xp, *ffn_w(0), *ln(0, 1))

    qkv_s, gate_s, qb_s, rows_s, win_s, small_s = _proj(xs, ws)
    conv_s = state_delta_conv[0]
    o_a_s, ds_s = _delta(_pad_rows(qkv_s, bs), _pad_rows(small_s, bs), _pad_rows(gate_s, bs),
                         jnp.pad(conv_s, ((0, 0), (8 - (CONV_W - 1), 0), (0, 0))), conv_w, alog, dtb, dng,
                         state_delta_S[0], t_valid=1)
    win_buf = state_nsa_win[0]
    wb = win_buf.shape[1]
    r1 = lambda a: a.reshape(bs, 1, -1)
    o_b_s = _nsa_decode(page_table, r1(qb_s), r1(rows_s), r1(win_s), r1(small_s), win_buf.reshape(bs, wb, -1),
                        cache_nsa_kv.reshape(-1, B_DH))
    xs = _outproj_ln([o_a_s[:, 0], o_b_s.reshape(bs, -1)], wo_parts, xs, *ln(0, 0))
    xs = _ffn_ln(xs, *ffn_w(0), *ln(0, 1))

    kvd = (B_KV, B_DH)
    nsa_rows_p = rows_planes_p.reshape((1, bp, t, 4) + kvd)
    nsa_win_p = r3(win_p)[:, -min(WINDOW, t):].reshape((1, bp, min(WINDOW, t), 2) + kvd)
    delta_conv_p = jnp.concatenate([jnp.zeros((bp, CONV_W - 1, A_CONV_CH), f32), r3(qkv_p)], axis=1)[:, -(CONV_W - 1):][None]
    nsa_rows_s = rows_s.reshape((1, bs, 1, 4) + kvd)
    win_cat = jnp.concatenate([win_buf, win_s.reshape((bs, 1, 2) + kvd)], axis=1)
    nsa_win_s = win_cat[:, -min(WINDOW, wb + 1):][None]
    delta_conv_s = jnp.concatenate([conv_s, qkv_s[:, None, :]], axis=1)[:, -(CONV_W - 1):][None]

    wq, wk, wv, wg1, wr = _odd_weights(w_in_odd[0])
    wg2 = jnp.pad(w_gla_gate2[0], ((0, LANES - C_LOWRANK), (0, 0))).astype(bf16)
    bg2 = b_gla_gate2[0].reshape(1, C_QK)
    gng = gla_norm_g[0].reshape(1, C_DV)
    wo1 = [w_out_odd[0].astype(bf16)]

    q_p, k_p, v_p, g1_p, rr_p = _proj(xp, [wq, wk, wv, wg1, wr])
    o_c_p, gs_p = _gla(r3(q_p), r3(k_p), r3(v_p), r3(rr_p), r3(g1_p), wg2, bg2, gng,
                       jnp.zeros((bp, C_HEADS, C_DK, C_DV), f32), t_valid=t)
    xp = _outproj_ln([o_c_p.reshape(bp * t, -1)], wo1, xp, *ln(1, 0))
    xp = _ffn_ln(xp, *ffn_w(1), *ln(1, 1))

    q_s, k_s, v_s, g1_s, rr_s = _proj(xs, [wq, wk, wv, wg1, wr])
    o_c_s, gs_s = _gla(_pad_rows(q_s, bs), _pad_rows(k_s, bs), _pad_rows(v_s, bs), _pad_rows(rr_s, bs),
                       _pad_rows(g1_s, bs), wg2, bg2, gng, state_gla_S[0], t_valid=1)
    xs = _outproj_ln([o_c_s[:, 0]], wo1, xs, *ln(1, 0))
    xs = _ffn_ln(xs, *ffn_w(1), *ln(1, 1))

    return (xp.reshape(bp, t, d), xs.reshape(bs, 1, d),
            nsa_rows_p, nsa_win_p, delta_conv_p, ds_p[None], gs_p[None],
            nsa_rows_s, nsa_win_s, delta_conv_s, ds_s[None], gs_s[None])
```

```python
import functools
import math

import jax
import jax.numpy as jnp
from jax import lax
from jax.experimental import pallas as pl
from jax.experimental.pallas import tpu as pltpu

f32 = jnp.float32
bf16 = jnp.bfloat16
HI = lax.Precision.HIGHEST

D_MODEL = 1024
DEPTH = 2
PAGE_SIZE = 128

A_HEADS = 4
A_DK = 128
A_DV = 128
A_QK = A_HEADS * A_DK
A_WIDTH = A_HEADS * A_DV
A_CONV_CH = 2 * A_QK + A_WIDTH
CONV_W = 4
DELTA_CHUNK = 64

B_HEADS = 8
B_KV = 2
B_HPG = B_HEADS // B_KV
B_DH = 64
B_WIDTH = B_HEADS * B_DH
N_BRANCH = 3
CMP_BLK = 32
SEL_BLK = 64
SEL_TOPK = 16
WINDOW = 512
B_SCALE = B_DH ** -0.5

C_HEADS = 4
C_DK = 128
C_DV = 256
C_QK = C_HEADS * C_DK
C_WIDTH = C_HEADS * C_DV
C_LOWRANK = 16
GLA_TAU = 16.0
GLA_CHUNK = 64
GLA_SUB = 16

D_FF = -(-(8 * D_MODEL) // (3 * 256)) * 256
ALPHA = (2 * DEPTH) ** 0.25
NEG = -1e30
FORCE = 1e9

LANES = 128
VMEM_LIMIT = 56 * 1024 * 1024


def _cparams(sem):
    return pltpu.CompilerParams(dimension_semantics=sem, vmem_limit_bytes=VMEM_LIMIT)


def _dot(a, b, precision=None):
    return jnp.dot(a, b, preferred_element_type=f32, precision=precision)


def _dot_nt(a, b, precision=None):
    return lax.dot_general(a, b, (((1,), (1,)), ((), ())), preferred_element_type=f32, precision=precision)


def _dot_tn(a, b, precision=None):
    return lax.dot_general(a, b, (((0,), (0,)), ((), ())), preferred_element_type=f32, precision=precision)


def _silu(x):
    return x * jax.nn.sigmoid(x)


def _layer_norm(x, g, b, eps=1e-5):
    mu = jnp.mean(x, -1, keepdims=True)
    xc = x - mu
    var = jnp.mean(xc * xc, -1, keepdims=True)
    return xc * lax.rsqrt(var + eps) * g + b


def _row_tile(m, pref):
    t = min(pref, m)
    assert m % t == 0
    return t


def _proj_kernel(x_ref, *refs, n, planes_of):
    xb = x_ref[...].astype(bf16)
    tm = x_ref.shape[0]
    for idx, (w_ref, o_ref) in enumerate(zip(refs[:n], refs[n:2 * n])):
        val = _dot(xb, w_ref[...])
        o_ref[...] = val
        if idx == planes_of:
            p_ref = refs[2 * n]
            n_planes = val.shape[1] // B_DH
            for j in range(n_planes):
                p_ref[pl.ds(j, tm, stride=n_planes), :] = val[:, j * B_DH:(j + 1) * B_DH]


def _proj(x, weights, tm=512, planes_of=None):
    m, k = x.shape
    tm = _row_tile(m, tm)
    in_specs = [pl.BlockSpec((tm, k), lambda i: (i, 0))]
    in_specs += [pl.BlockSpec(w.shape, lambda i: (0, 0)) for w in weights]
    out_specs = [pl.BlockSpec((tm, w.shape[1]), lambda i: (i, 0)) for w in weights]
    out_shape = [jax.ShapeDtypeStruct((m, w.shape[1]), f32) for w in weights]
    if planes_of is not None:
        n_planes = weights[planes_of].shape[1] // B_DH
        out_specs.append(pl.BlockSpec((tm * n_planes, B_DH), lambda i: (i, 0)))
        out_shape.append(jax.ShapeDtypeStruct((m * n_planes, B_DH), f32))
    return pl.pallas_call(
        functools.partial(_proj_kernel, n=len(weights), planes_of=planes_of),
        grid=(m // tm,), in_specs=in_specs, out_specs=out_specs, out_shape=out_shape,
        compiler_params=_cparams(("parallel",)), name="in_proj")(x, *weights)


def _outproj_ln_kernel(*refs, n_parts):
    parts = refs[:n_parts]
    ws = refs[n_parts:2 * n_parts]
    x_ref, g_ref, b_ref, o_ref = refs[2 * n_parts:]
    y = None
    for p_ref, w_ref in zip(parts, ws):
        d = _dot(p_ref[...].astype(bf16), w_ref[...])
        y = d if y is None else y + d
    o_ref[...] = _layer_norm(ALPHA * x_ref[...] + y, g_ref[...], b_ref[...])


def _outproj_ln(parts, weights, x, g, b, tm=512):
    m, d = x.shape
    tm = _row_tile(m, tm)
    n = len(parts)
    in_specs = [pl.BlockSpec((tm, p.shape[1]), lambda i: (i, 0)) for p in parts]
    in_specs += [pl.BlockSpec(w.shape, lambda i: (0, 0)) for w in weights]
    in_specs += [pl.BlockSpec((tm, d), lambda i: (i, 0)),
                 pl.BlockSpec((1, d), lambda i: (0, 0)), pl.BlockSpec((1, d), lambda i: (0, 0))]
    return pl.pallas_call(
        functools.partial(_outproj_ln_kernel, n_parts=n), grid=(m // tm,), in_specs=in_specs,
        out_specs=pl.BlockSpec((tm, d), lambda i: (i, 0)), out_shape=jax.ShapeDtypeStruct((m, d), f32),
        compiler_params=_cparams(("parallel",)), name="out_proj_ln")(*parts, *weights, x, g, b)


FFN_COLS = 256


def _ffn_ln_kernel(x_ref, wg_ref, wu_ref, wd_ref, g_ref, b_ref, o_ref):
    x = x_ref[...]
    xb = x.astype(bf16)
    acc = jnp.zeros(x.shape, f32)
    for c in range(0, D_FF, FFN_COLS):
        hg = _dot(xb, wg_ref[:, c:c + FFN_COLS])
        hu = _dot(xb, wu_ref[:, c:c + FFN_COLS])
        h = (_silu(hg) * hu).astype(bf16)
        acc = acc + _dot(h, wd_ref[c:c + FFN_COLS, :])
    o_ref[...] = _layer_norm(ALPHA * x + acc, g_ref[...], b_ref[...])


def _ffn_ln(x, wg, wu, wd, g, b, tm=256):
    m, d = x.shape
    tm = _row_tile(m, tm)
    full = lambda a: pl.BlockSpec(a.shape, lambda i: (0, 0))
    return pl.pallas_call(
        _ffn_ln_kernel, grid=(m // tm,),
        in_specs=[pl.BlockSpec((tm, d), lambda i: (i, 0)), full(wg), full(wu), full(wd), full(g), full(b)],
        out_specs=pl.BlockSpec((tm, d), lambda i: (i, 0)), out_shape=jax.ShapeDtypeStruct((m, d), f32),
        compiler_params=_cparams(("parallel",)), name="ffn_ln")(x, wg, wu, wd, g, b)


DELTA_INV_BLK = 16


def _tri_masks(c):
    r = lax.broadcasted_iota(jnp.int32, (c, c), 0)
    s = lax.broadcasted_iota(jnp.int32, (c, c), 1)
    return r, s


def _split2(x):
    hi = x.astype(bf16)
    return hi, (x - hi.astype(f32)).astype(bf16)


def _split3(x):
    hi = x.astype(bf16)
    rest = x - hi.astype(f32)
    mid = rest.astype(bf16)
    return hi, mid, (rest - mid.astype(f32)).astype(bf16)


def _dot_x3(a, b):
    ah, al = _split2(a)
    bh, bl = _split2(b)
    return _dot(ah, bh) + (_dot(ah, bl) + _dot(al, bh))


def _dot_ones(a_ones, b):
    a16 = a_ones.astype(bf16)
    b1, b2, b3 = _split3(b)
    return _dot(a16, b1) + (_dot(a16, b2) + _dot(a16, b3))


def _unit_lower_inverses(mats, r, s):
    c = mats[0].shape[0]
    eye = (r == s).astype(f32)
    same_blk = (r // DELTA_INV_BLK) == (s // DELTA_INV_BLK)
    diag = [jnp.where(same_blk, a, 0.0) for a in mats]
    low = [a - d for a, d in zip(mats, diag)]
    pw = [-d for d in diag]
    p = [eye + x for x in pw]
    k = 2
    while k < DELTA_INV_BLK:
        pw = [_dot_x3(x, x) for x in pw]
        p = [pi + _dot_x3(pi, x) for pi, x in zip(p, pw)]
        k *= 2
    pw = [_dot_x3(pi, lo) for pi, lo in zip(p, low)]
    q = [eye - x for x in pw]
    k = 2
    while k < c // DELTA_INV_BLK:
        pw = [_dot_x3(x, x) for x in pw]
        q = [qi + _dot_x3(qi, x) for qi, x in zip(q, pw)]
        k *= 2
    return [_dot_x3(qi, pi) for qi, pi in zip(q, p)]


DELTA_PREP_CHUNKS = 2


def _delta_kernel(qkv_ref, small_ref, gate_ref, cbuf_ref, cw_ref, alog_ref, dtb_ref, dng_ref, s0_ref,
                  o_ref, s_ref, xbuf, ybuf, tail, u_ref, w_ref, qg_ref, kd_ref, qk_ref, gl_ref,
                  *, tt, t_valid, t_total):
    t = pl.program_id(1)
    c = DELTA_CHUNK
    n_chunks = tt // c
    cpi = math.gcd(DELTA_PREP_CHUNKS, n_chunks)

    @pl.when(t == 0)
    def _():
        s_ref[...] = s0_ref[...]
        tail[...] = cbuf_ref[...]

    xbuf[0:8, :] = tail[...]
    xbuf[8:8 + tt, :] = qkv_ref[...]
    tail[...] = xbuf[tt:tt + 8, :]

    def conv_body(ci, carry):
        r0 = pl.multiple_of(ci * c, c)
        for col in range(0, A_CONV_CH, A_QK):
            cols = slice(col, col + A_QK)
            x = xbuf[pl.ds(r0, c + 8), cols]
            first = 8 - (CONV_W - 1)
            y = x[first:first + c] * cw_ref[0:1, cols]
            for j in range(1, CONV_W):
                y = y + x[first + j:first + j + c] * cw_ref[j:j + 1, cols]
            ybuf[pl.ds(r0, c), cols] = _silu(y)
        return carry

    lax.fori_loop(0, n_chunks, conv_body, 0)

    r, s = _tri_masks(c)
    causal = r >= s
    strict = r > s
    lower_ones = causal.astype(f32)
    upper4 =jnp.concatenate([(r <= s).astype(f32)] * A_HEADS, axis=1)
    all_ones = jnp.ones((c, c), f32)
    alog = alog_ref[...]
    dtb = dtb_ref[...]
    dng = dng_ref[...]

    def prep_body(ci, carry):
        items = []
        for cc in range(cpi):
            r0 = pl.multiple_of((ci * cpi + cc) * c, c)
            sm = small_ref[pl.ds(r0, c), :]
            beta_all = jax.nn.sigmoid(sm)
            g_all = -jnp.exp(alog) * jax.nn.softplus(sm + dtb)
            if t_valid < t_total:
                live = t * tt + r0 + lax.broadcasted_iota(jnp.int32, (c, 1), 0) < t_valid
                beta_all = jnp.where(live, beta_all, 0.0)
                g_all = jnp.where(live, g_all, 0.0)
            gam_c_all = _dot_ones(lower_ones, g_all)
            g_rows = jnp.concatenate(
                [jnp.broadcast_to(g_all[:, A_HEADS + h:A_HEADS + h + 1], (c, c)) for h in range(A_HEADS)], axis=1)
            gam_r_all = _dot_ones(all_ones, g_rows * upper4)
            for h in range(A_HEADS):
                qh = ybuf[pl.ds(r0, c), h * A_DK:(h + 1) * A_DK]
                kh = ybuf[pl.ds(r0, c), A_QK + h * A_DK:A_QK + (h + 1) * A_DK]
                vh = ybuf[pl.ds(r0, c), 2 * A_QK + h * A_DV:2 * A_QK + (h + 1) * A_DV]
                qh = qh * lax.rsqrt(jnp.sum(qh * qh, -1, keepdims=True) + 1e-6) * (A_DK ** -0.5)
                kh = kh * lax.rsqrt(jnp.sum(kh * kh, -1, keepdims=True) + 1e-6)
                if t_valid < t_total:
                    qh = jnp.where(live, qh, 0.0)
                    kh = jnp.where(live, kh, 0.0)
                    vh = jnp.where(live, vh, 0.0)
                beta = beta_all[:, h:h + 1]
                gam_c = jnp.broadcast_to(gam_c_all[:, A_HEADS + h:A_HEADS + h + 1], (c, A_DK))
                diff = gam_c[:, :c] - gam_r_all[:, h * c:(h + 1) * c]
                ldec = jnp.where(causal, jnp.exp(jnp.where(causal, diff, 0.0)), 0.0)
                items.append(dict(r0=r0, h=h, q=qh, k=kh, v=vh, beta=beta, gam=gam_c, ldec=ldec, kb=kh * beta))
        amats = [jnp.where(strict, _dot_nt(it["kb"], it["k"]) * it["ldec"], 0.0) for it in items]
        tms = _unit_lower_inverses(amats, r, s)
        for it, tm in zip(items, tms):
            r0, h = it["r0"], it["h"]
            cols = slice(h * A_DK, (h + 1) * A_DK)
            eg = jnp.exp(it["gam"])
            g_last = it["gam"][c - 1:c, :]
            u_ref[pl.ds(r0, c), cols] = _dot(tm, it["v"] * it["beta"])
            w_ref[pl.ds(r0, c), cols] = _dot(tm, it["kb"] * eg)
            qg_ref[pl.ds(r0, c), cols] = it["q"] * eg
            kd_ref[pl.ds(r0, c), cols] = it["k"] * jnp.exp(g_last - it["gam"])
            qk_ref[pl.ds(r0, c), h * c:(h + 1) * c] = _dot_nt(it["q"], it["k"]) * it["ldec"]
            gl_ref[pl.ds(r0, 1), cols] = jnp.exp(g_last)
        return carry

    lax.fori_loop(0, n_chunks // cpi, prep_body, 0)

    def scan_body(ci, carry):
        r0 = pl.multiple_of(ci * c, c)
        for h in range(A_HEADS):
            cols = slice(h * A_DK, (h + 1) * A_DK)
            st = s_ref[h]
            v_new = u_ref[pl.ds(r0, c), cols] - _dot(w_ref[pl.ds(r0, c), cols], st)
            o = _dot(qg_ref[pl.ds(r0, c), cols], st) + _dot(qk_ref[pl.ds(r0, c), h * c:(h + 1) * c], v_new)
            s_ref[h] = st * gl_ref[pl.ds(r0, 1), cols] + _dot_tn(kd_ref[pl.ds(r0, c), cols], v_new)
            o = o * lax.rsqrt(jnp.mean(o * o, -1, keepdims=True) + 1e-6) * dng
            gt = gate_ref[pl.ds(r0, c), h * A_DV:(h + 1) * A_DV]
            o_ref[pl.ds(r0, c), h * A_DV:(h + 1) * A_DV] = o * _silu(gt)
        return carry

    lax.fori_loop(0, n_chunks, scan_body, 0)


def _delta(qkv, small, gate, conv_buf8, conv_w, alog, dtb, dng, s0, *, t_valid, tt=512):
    b, t_total, _ = qkv.shape
    tt = _row_tile(t_total, tt)
    nt = t_total // tt
    row = lambda w: pl.BlockSpec((None, tt, w), lambda i, j: (i, j, 0))
    full2 = lambda a: pl.BlockSpec(a.shape, lambda i, j: (0, 0))
    st_spec = pl.BlockSpec((None, A_HEADS, A_DK, A_DV), lambda i, j: (i, 0, 0, 0))
    kern = functools.partial(_delta_kernel, tt=tt, t_valid=t_valid, t_total=t_total)
    return pl.pallas_call(
        kern, grid=(b, nt),
        in_specs=[row(A_CONV_CH), row(LANES), row(A_WIDTH),
                  pl.BlockSpec((None, 8, A_CONV_CH), lambda i, j: (i, 0, 0)),
                  full2(conv_w), full2(alog), full2(dtb), full2(dng), st_spec],
        out_specs=[row(A_WIDTH), st_spec],
        out_shape=[jax.ShapeDtypeStruct((b, t_total, A_WIDTH), f32),
                   jax.ShapeDtypeStruct((b, A_HEADS, A_DK, A_DV), f32)],
        scratch_shapes=[pltpu.VMEM((tt + 8, A_CONV_CH), f32), pltpu.VMEM((tt, A_CONV_CH), f32),
                        pltpu.VMEM((8, A_CONV_CH), f32)]
        + [pltpu.VMEM((tt, A_QK), f32)] * 4
        + [pltpu.VMEM((tt, A_HEADS * DELTA_CHUNK), f32), pltpu.VMEM((tt, A_QK), f32)],
        compiler_params=_cparams(("parallel", "arbitrary")), name="delta_mixer",
    )(qkv, small, gate, conv_buf8, conv_w, alog, dtb, dng, s0)


def _gla_kernel(q_ref, k_ref, v_ref, r_ref, g1_ref, wg2_ref, bg2_ref, gng_ref, s0_ref,
                o_ref, s_ref, st_ref, la_ref, qe_ref, gl_ref, ds_ref, *, tt, t_valid, t_total):
    t = pl.program_id(1)
    nt = pl.num_programs(1)
    c = GLA_CHUNK
    heads = range(C_HEADS)
    kcols = [slice(h * C_DK, (h + 1) * C_DK) for h in heads]
    vcols = [slice(h * C_DV, (h + 1) * C_DV) for h in heads]

    @pl.when(t == 0)
    def _():
        for h in range(C_HEADS):
            st_ref[h] = s0_ref[h].T

    la = jax.nn.log_sigmoid(_dot(g1_ref[...].astype(bf16), wg2_ref[...]) + bg2_ref[...]) / GLA_TAU
    if t_valid < t_total:
        rows = t * tt + lax.broadcasted_iota(jnp.int32, (tt, 1), 0)
        la = jnp.where(rows < t_valid, la, 0.0)
    la_ref[...] = la

    r, s = _tri_masks(c)
    lower_ones = (r >= s).astype(f32)
    rs = lax.broadcasted_iota(jnp.int32, (GLA_SUB, c), 0)
    ss = lax.broadcasted_iota(jnp.int32, (GLA_SUB, c), 1)
    krow = lax.broadcasted_iota(jnp.int32, (c, 1), 0)
    gng = gng_ref[...]

    def prep_body(ci, carry):
        r0 = pl.multiple_of(ci * c, c)
        rows = pl.ds(r0, c)
        bc_all = _dot_ones(lower_ones, la_ref[rows, :])
        qs = [q_ref[rows, kcols[h]] * (C_DK ** -0.5) for h in heads]
        ks = [k_ref[rows, kcols[h]] for h in heads]
        vs = [v_ref[rows, vcols[h]] for h in heads]
        bcs = [bc_all[:, kcols[h]] for h in heads]
        for h in heads:
            qe_ref[rows, kcols[h]] = qs[h] * jnp.exp(bcs[h])
        for a in range(c // GLA_SUB):
            lo, hi = a * GLA_SUB, (a + 1) * GLA_SUB
            seen = krow < hi
            brefs = [bcs[h][lo - 1:lo, :] if a > 0 else jnp.zeros((1, C_DK), f32) for h in heads]
            qa = [qs[h][lo:hi, :] * jnp.exp(bcs[h][lo:hi, :] - brefs[h]) for h in heads]
            ka = [jnp.where(seen, ks[h] * jnp.exp(jnp.where(seen, brefs[h] - bcs[h], 0.0)), 0.0) for h in heads]
            att = [jnp.where(rs + lo >= ss, _dot_nt(qa[h], ka[h]), 0.0) for h in heads]
            for h in heads:
                o_ref[pl.ds(r0 + lo, GLA_SUB), vcols[h]] = _dot(att[h], vs[h])
        for h in heads:
            b_last = bcs[h][c - 1:c, :]
            gl_ref[pl.ds(r0, 1), kcols[h]] = jnp.exp(b_last)
            ds_ref[ci, h] = _dot_tn(vs[h], ks[h] * jnp.exp(b_last - bcs[h]))
        return carry

    lax.fori_loop(0, tt // c, prep_body, 0)

    def scan_body(ci, carry):
        r0 = pl.multiple_of(ci * c, c)
        rows = pl.ds(r0, c)
        for h in heads:
            st = st_ref[h]
            o = _dot_nt(qe_ref[rows, kcols[h]], st) + o_ref[rows, vcols[h]]
            st_ref[h] = st * gl_ref[pl.ds(r0, 1), kcols[h]] + ds_ref[ci, h]
            o = o * lax.rsqrt(jnp.mean(o * o, -1, keepdims=True) + 1e-6) * gng
            o_ref[rows, vcols[h]] = o * _silu(r_ref[rows, vcols[h]])
        return carry

    lax.fori_loop(0, tt // c, scan_body, 0)

    @pl.when(t == nt - 1)
    def _():
        for h in range(C_HEADS):
            s_ref[h] = st_ref[h].T


def _gla(q, k, v, r, g1, wg2, bg2, gng, s0, *, t_valid, tt=512):
    b, t_total, _ = q.shape
    tt = _row_tile(t_total, tt)
    nt = t_total // tt
    row = lambda w: pl.BlockSpec((None, tt, w), lambda i, j: (i, j, 0))
    full2 = lambda a: pl.BlockSpec(a.shape, lambda i, j: (0, 0))
    st_spec = pl.BlockSpec((None, C_HEADS, C_DK, C_DV), lambda i, j: (i, 0, 0, 0))
    kern = functools.partial(_gla_kernel, tt=tt, t_valid=t_valid, t_total=t_total)
    return pl.pallas_call(
        kern, grid=(b, nt),
        in_specs=[row(C_QK), row(C_QK), row(C_WIDTH), row(C_WIDTH), row(LANES),
                  full2(wg2), full2(bg2), full2(gng), st_spec],
        out_specs=[row(C_WIDTH), st_spec],
        out_shape=[jax.ShapeDtypeStruct((b, t_total, C_WIDTH), f32),
                   jax.ShapeDtypeStruct((b, C_HEADS, C_DK, C_DV), f32)],
        scratch_shapes=[pltpu.VMEM((C_HEADS, C_DV, C_DK), f32), pltpu.VMEM((tt, C_QK), f32),
                        pltpu.VMEM((tt, C_QK), f32), pltpu.VMEM((tt, C_QK), f32),
                        pltpu.VMEM((tt // GLA_CHUNK, C_HEADS, C_DV, C_DK), f32)],
        compiler_params=_cparams(("parallel", "arbitrary")), name="gla_mixer",
    )(q, k, v, r, g1, wg2, bg2, gng, s0)


GB_LANE0 = 2 * A_HEADS


def _masked_softmax(s, mask):
    sm = jnp.where(mask, s, NEG)
    e = jnp.exp(sm - jnp.max(sm, -1, keepdims=True))
    p = e / jnp.sum(e, -1, keepdims=True)
    return jnp.where(mask, p, 0.0)


def _topk_rank(score, n_cand):
    lane = lax.broadcasted_iota(jnp.int32, score.shape, score.ndim - 1)
    rank = jnp.zeros(score.shape, jnp.int32)
    for i in range(n_cand):
        si = score[..., i:i + 1]
        ahead = (si > score) | ((si == score) & (i < lane))
        rank = rank + ahead.astype(jnp.int32)
    return rank


def _block_scores(imp, qpos, n_blk):
    blk = lax.broadcasted_iota(jnp.int32, imp.shape, imp.ndim - 1)
    cur = qpos // SEL_BLK
    valid = blk * SEL_BLK <= qpos
    forced = (blk == 0) | (blk == cur) | (blk == cur - 1)
    return jnp.where(forced, FORCE, jnp.where(valid, imp, NEG))


def _split_cmp_index(col, half):
    return jnp.where(col < half, 2 * col, 2 * (col - half) + 1)


NSA_TQ = 128
NSA_TK = 512


def _topk_mask_rows(score, k):
    n = score.shape[0]
    row = lax.broadcasted_iota(jnp.int32, score.shape, 0)
    rank = jnp.zeros(score.shape, jnp.int32)
    for i in range(n):
        si = score[i:i + 1, :]
        ahead = (si > score) | ((si == score) & (i < row))
        rank = rank + ahead.astype(jnp.int32)
    return (rank < k).astype(f32)


def _nsa_prompt_kernel(q_ref, rows_ref, win_ref, small_ref, o_ref, cb_ref, sel_ref, *, t_total):
    i = pl.program_id(1)
    tq = NSA_TQ
    tk = min(NSA_TK, t_total)
    tw = min(WINDOW + tq, t_total)
    n_sel = t_total // SEL_BLK
    half = n_sel
    hq = B_HPG * tq
    blk_per_tile = tk // SEL_BLK

    @pl.when(i == 0)
    def _():
        x = rows_ref[:, 0:2 * B_KV * B_DH].reshape(n_sel, SEL_BLK, 2 * B_KV * B_DH)
        cb_ref[0:half, :] = jnp.sum(x[:, :CMP_BLK, :], axis=1) * (1.0 / CMP_BLK)
        cb_ref[half:2 * half, :] = jnp.sum(x[:, CMP_BLK:, :], axis=1) * (1.0 / CMP_BLK)

    qpos = i * tq + lax.broadcasted_iota(jnp.int32, (1, tq), 1)
    qpos4 = jnp.concatenate([qpos] * B_HPG, axis=1)
    gates_t = jax.nn.sigmoid(small_ref[...].T)
    krow = lax.broadcasted_iota(jnp.int32, (tk, 1), 0)
    wrow = lax.broadcasted_iota(jnp.int32, (tw, 1), 0)
    n_causal = (i * tq + tq - 1) // tk + 1
    groups = range(B_KV)
    q4 = [jnp.concatenate([q_ref[:, (g * B_HPG + h) * B_DH:(g * B_HPG + h + 1) * B_DH] for h in range(B_HPG)],
                          axis=0) * B_SCALE for g in groups]

    o_cmp = []
    for g in groups:
        kc = cb_ref[:, g * B_DH:(g + 1) * B_DH]
        vc = cb_ref[:, (B_KV + g) * B_DH:(B_KV + g + 1) * B_DH]
        st = _dot_nt(kc, q4[g])
        cidx = _split_cmp_index(lax.broadcasted_iota(jnp.int32, (2 * half, 1), 0), half)
        cmask = (cidx + 1) * CMP_BLK <= qpos4 + 1
        sm = jnp.where(cmask, st, NEG)
        e = jnp.exp(sm - jnp.max(sm, axis=0, keepdims=True))
        p = jnp.where(cmask, e / jnp.sum(e, axis=0, keepdims=True), 0.0)
        o_cmp.append(_dot_tn(vc, p))
        imp = p[:, 0:tq]
        for h in range(1, B_HPG):
            imp = imp + p[:, h * tq:(h + 1) * tq]
        imp = imp[:half] + imp[half:]
        blk = lax.broadcasted_iota(jnp.int32, (n_sel, 1), 0)
        cur = qpos // SEL_BLK
        forced = (blk == 0) | (blk == cur) | (blk == cur - 1)
        score = jnp.where(forced, FORCE, jnp.where(blk * SEL_BLK <= qpos, imp, NEG))
        sel_ref[g] = _topk_mask_rows(score, min(SEL_TOPK, n_sel))

    def attend(qg, k, v, live, carry):
        m, l, acc = carry
        bias = jnp.where(live, 0.0, NEG)
        sc = _dot_nt(k, qg) + jnp.concatenate([bias] * B_HPG, axis=1)
        m_new = jnp.maximum(m, jnp.max(sc, axis=0, keepdims=True))
        pe = jnp.exp(sc - m_new)
        alpha = jnp.exp(m - m_new)
        l = alpha * l + jnp.sum(pe, axis=0, keepdims=True)
        acc = alpha * acc + _dot_tn(v, pe)
        return m_new, l, acc

    init = (jnp.full((1, hq), NEG, f32), jnp.zeros((1, hq), f32), jnp.zeros((B_DH, hq), f32))

    def sel_step(j, carries):
        k0 = pl.multiple_of(j * tk, tk)
        causal = k0 + krow <= qpos
        out = []
        for g in groups:
            flags = sel_ref[g, pl.ds(pl.multiple_of(j * blk_per_tile, blk_per_tile), blk_per_tile), :]
            flags = jnp.concatenate(
                [jnp.broadcast_to(flags[u:u + 1, :], (SEL_BLK, tq)) for u in range(blk_per_tile)], axis=0)
            kl, vl = (2 * B_KV + g) * B_DH, (3 * B_KV + g) * B_DH
            out.append(attend(q4[g], rows_ref[pl.ds(k0, tk), kl:kl + B_DH], rows_ref[pl.ds(k0, tk), vl:vl + B_DH],
                              (flags > 0.5) & causal, carries[g]))
        return tuple(out)

    sel_out = lax.fori_loop(0, n_causal, sel_step, (init,) * B_KV)
    w0 = pl.multiple_of(jnp.maximum((i + 1) * tq - tw, 0), tq)
    d = qpos - (w0 + wrow)
    in_window = (d >= 0) & (d < WINDOW)
    out_rows = []
    for g in groups:
        kl, vl = g * B_DH, (B_KV + g) * B_DH
        _, l_w, acc_w = attend(q4[g], win_ref[pl.ds(w0, tw), kl:kl + B_DH], win_ref[pl.ds(w0, tw), vl:vl + B_DH],
                               in_window, init)
        o_win = acc_w / l_w
        o_sel = sel_out[g][2] / sel_out[g][1]
        for h in range(B_HPG):
            c0 = GB_LANE0 + (g * B_HPG + h) * N_BRANCH
            ls = slice(h * tq, (h + 1) * tq)
            out_rows.append(gates_t[c0:c0 + 1, :] * o_cmp[g][:, ls] + gates_t[c0 + 1:c0 + 2, :] * o_sel[:, ls]
                            + gates_t[c0 + 2:c0 + 3, :] * o_win[:, ls])
    o_ref[...] = jnp.concatenate(out_rows, axis=0).T


def _nsa_prompt(qb, rows, win, small):
    b, t_total, _ = qb.shape
    tk = min(NSA_TK, t_total)
    assert t_total % tk == 0 and tk % NSA_TQ == 0 and tk % SEL_BLK == 0 and WINDOW % NSA_TQ == 0
    nq = t_total // NSA_TQ
    n_sel = t_total // SEL_BLK
    tile = lambda w: pl.BlockSpec((None, NSA_TQ, w), lambda bi, i: (bi, i, 0))
    seq = lambda w: pl.BlockSpec((None, t_total, w), lambda bi, i: (bi, 0, 0))
    return pl.pallas_call(
        functools.partial(_nsa_prompt_kernel, t_total=t_total), grid=(b, nq),
        in_specs=[tile(B_WIDTH), seq(4 * B_KV * B_DH), seq(2 * B_KV * B_DH), tile(LANES)],
        out_specs=tile(B_WIDTH), out_shape=jax.ShapeDtypeStruct((b, t_total, B_WIDTH), f32),
        scratch_shapes=[pltpu.VMEM((2 * n_sel, 2 * B_KV * B_DH), f32), pltpu.VMEM((B_KV, n_sel, NSA_TQ), f32)],
        compiler_params=_cparams(("parallel", "arbitrary")), name="nsa_prompt",
    )(qb, rows, win, small)


def _nsa_decode_cmp_kernel(pt_ref, q_ref, wnew_ref, wbuf_ref, pool_ref, cache_ref, ocmp_ref, owin_ref, idx_ref,
                           pages, acc_ref, sem, *, n_pages, past):
    b = pl.program_id(0)
    nb = pl.num_programs(0)
    slot = b % 2
    n_hist_blk = past // SEL_BLK
    n_sel = -(-(past + 1) // SEL_BLK)
    half = n_hist_blk
    lanes_sel = 2 * half
    wb = wbuf_ref.shape[0]
    n_cmp_rows = 2 * B_KV * B_DH

    def page_copy(bb, p, sl):
        return pltpu.make_async_copy(cache_ref.at[pt_ref[bb, p], pl.ds(0, 2)], pages.at[sl, p], sem.at[sl])

    def fetch(bb, sl):
        for p in range(n_pages):
            page_copy(bb, p, sl).start()

    @pl.when(b == 0)
    def _():
        fetch(0, 0)

    @pl.when(b + 1 < nb)
    def _():
        fetch(b + 1, 1 - slot)

    for p in range(n_pages):
        page_copy(b, p, slot).wait()

    acc_ref[...] = jnp.zeros(acc_ref.shape, f32)

    def pool_body(p, carry):
        x = pages[slot, p].reshape(n_cmp_rows, PAGE_SIZE)
        hi, lo = _split2(x)
        w = pool_ref[pl.ds(pl.multiple_of(p * PAGE_SIZE, PAGE_SIZE), PAGE_SIZE), :]
        acc_ref[...] += _dot(hi, w) + _dot(lo, w)
        return carry

    lax.fori_loop(0, n_pages, pool_body, 0)

    qpos = past
    row8 = lax.broadcasted_iota(jnp.int32, (B_HEADS, 1), 0)
    q8 = jnp.concatenate([q_ref[:, h * B_DH:(h + 1) * B_DH] for h in range(B_HEADS)], axis=0) * B_SCALE
    wnew = wnew_ref[...]
    lane = lax.broadcasted_iota(jnp.int32, (1, lanes_sel), 1)
    o_cmp, o_win, idx_rows = [], [], []
    for g in range(B_KV):
        in_g = (row8 // B_HPG) == g
        kc_t = acc_ref[g * B_DH:(g + 1) * B_DH, :]
        vc_t = acc_ref[(B_KV + g) * B_DH:(B_KV + g + 1) * B_DH, :]
        sc = _dot(q8, kc_t)
        cidx = _split_cmp_index(lane, half)
        p = _masked_softmax(sc, (cidx + 1) * CMP_BLK <= qpos + 1)
        o_cmp.append(_dot_nt(p, vc_t))
        imp = jnp.sum(jnp.where(in_g, p, 0.0), axis=0, keepdims=True)
        imp = imp[:, :half] + imp[:, half:]
        imp = jnp.concatenate([imp, jnp.zeros((1, lanes_sel - half), f32)], axis=1)
        rank = _topk_rank(_block_scores(imp, qpos, n_sel), n_sel)
        lane_f = lane.astype(f32)
        out_lane = lax.broadcasted_iota(jnp.int32, (1, LANES), 1)
        picked = jnp.zeros((1, LANES), f32)
        for r in range(min(SEL_TOPK, n_sel)):
            blk_r = jnp.sum(jnp.where((rank == r) & (lane < n_sel), lane_f, 0.0), axis=-1, keepdims=True)
            picked = picked + jnp.where(out_lane == r, blk_r, 0.0)
        idx_rows.append(picked)
        kw = wbuf_ref[:, g * B_DH:(g + 1) * B_DH]
        vw = wbuf_ref[:, (B_KV + g) * B_DH:(B_KV + g + 1) * B_DH]
        kw_new = wnew[:, g * B_DH:(g + 1) * B_DH]
        vw_new = wnew[:, (B_KV + g) * B_DH:(B_KV + g + 1) * B_DH]
        kpos = qpos - wb + lax.broadcasted_iota(jnp.int32, (1, wb), 1)
        d = qpos - kpos
        wmask = (d >= 0) & (d < WINDOW) & (kpos >= 0)
        sw = jnp.where(wmask, _dot_nt(q8, kw), NEG)
        swn = jnp.sum(q8 * kw_new, -1, keepdims=True)
        m = jnp.maximum(jnp.max(sw, -1, keepdims=True), swn)
        ew = jnp.where(wmask, jnp.exp(sw - m), 0.0)
        ewn = jnp.exp(swn - m)
        o_win.append((_dot(ew, vw) + ewn * vw_new) / (jnp.sum(ew, -1, keepdims=True) + ewn))
    lower = row8 < B_HPG
    ocmp_ref[...] = jnp.where(lower, o_cmp[0], o_cmp[1])
    owin_ref[...] = jnp.where(lower, o_win[0], o_win[1])
    idx_ref[...] = jnp.concatenate(idx_rows + [jnp.zeros((8 - B_KV, LANES), f32)], axis=0).astype(jnp.int32)


def _nsa_decode_sel_kernel(pt_ref, sel_ref, q_ref, rnew_ref, small_ref, ocmp_ref, owin_ref, cache_ref,
                           o_ref, blocks, sem, *, past):
    b = pl.program_id(0)
    nb = pl.num_programs(0)
    slot = b % 2
    n_hist_blk = past // SEL_BLK
    n_pick = sel_ref.shape[2]
    blk_per_page = PAGE_SIZE // SEL_BLK
    tok_lane = lax.broadcasted_iota(jnp.int32, (1, PAGE_SIZE), 1)

    def hist_block(bb, g, j):
        return jnp.minimum(sel_ref[bb, g, j], n_hist_blk - 1)

    def block_copy(bb, g, j, sl):
        page = pt_ref[bb, hist_block(bb, g, j) // blk_per_page]
        return pltpu.make_async_copy(cache_ref.at[page, pl.ds(2, 2), g], blocks.at[sl, g * n_pick + j], sem.at[sl])

    def fetch(bb, sl):
        for g in range(B_KV):
            for j in range(n_pick):
                block_copy(bb, g, j, sl).start()

    @pl.when(b == 0)
    def _():
        fetch(0, 0)

    @pl.when(b + 1 < nb)
    def _():
        fetch(b + 1, 1 - slot)

    for g in range(B_KV):
        for j in range(n_pick):
            block_copy(b, g, j, slot).wait()

    row8 = lax.broadcasted_iota(jnp.int32, (B_HEADS, 1), 0)
    q8 = jnp.concatenate([q_ref[:, h * B_DH:(h + 1) * B_DH] for h in range(B_HEADS)], axis=0) * B_SCALE
    rnew = rnew_ref[...]
    o_sel = []
    for g in range(B_KV):
        ks_t = jnp.concatenate([blocks[slot, g * n_pick + j, 0] for j in range(n_pick)], axis=1)
        vs_t = jnp.concatenate([blocks[slot, g * n_pick + j, 1] for j in range(n_pick)], axis=1)
        picks = [sel_ref[b, g, j] for j in range(n_pick)]
        live = jnp.concatenate(
            [(tok_lane // SEL_BLK == hist_block(b, g, j) % blk_per_page) & (picks[j] < n_hist_blk)
             for j in range(n_pick)], axis=1)
        new_live = functools.reduce(jnp.logical_or, [pk >= n_hist_blk for pk in picks])
        k_new = rnew[:, (2 * B_KV + g) * B_DH:(2 * B_KV + g + 1) * B_DH]
        v_new = rnew[:, (3 * B_KV + g) * B_DH:(3 * B_KV + g + 1) * B_DH]
        sh = jnp.where(live, _dot(q8, ks_t), NEG)
        sn = jnp.sum(q8 * k_new, -1, keepdims=True) + jnp.where(new_live, 0.0, NEG).astype(f32)
        m = jnp.maximum(jnp.max(sh, -1, keepdims=True), sn)
        eh = jnp.where(live, jnp.exp(sh - m), 0.0)
        en = jnp.where(new_live, jnp.exp(sn - m), 0.0)
        o_sel.append((_dot_nt(eh, vs_t) + en * v_new) / (jnp.sum(eh, -1, keepdims=True) + en))
    os_ = jnp.where(row8 < B_HPG, o_sel[0], o_sel[1])
    oc = ocmp_ref[...]
    ow = owin_ref[...]
    gates = jax.nn.sigmoid(small_ref[...])
    outs = []
    for hh in range(B_HEADS):
        c0 = GB_LANE0 + hh * N_BRANCH
        outs.append(gates[:, c0:c0 + 1] * oc[hh:hh + 1] + gates[:, c0 + 1:c0 + 2] * os_[hh:hh + 1]
                    + gates[:, c0 + 2:c0 + 3] * ow[hh:hh + 1])
    o_ref[...] = jnp.concatenate(outs, axis=1)


def _block_pool_matrix(past):
    blk = jnp.arange(past) // CMP_BLK
    col = (blk % 2) * (past // SEL_BLK) + blk // 2
    return jnp.where(col[:, None] == jnp.arange(2 * (past // SEL_BLK))[None, :], 1.0 / CMP_BLK, 0.0).astype(bf16)


def _nsa_decode(page_table, qb, rows_new, win_new, small, win_buf, cache_t):
    nb, n_pages = page_table.shape
    past = n_pages * PAGE_SIZE
    assert past % SEL_BLK == 0 and cache_t.shape[1:] == (4, B_KV, B_DH, PAGE_SIZE)
    n_blk = past // SEL_BLK
    n_pick = min(SEL_TOPK, n_blk + 1)
    pool = _block_pool_matrix(past)
    one1 = lambda shape: pl.BlockSpec((None,) + shape, lambda i, pt: (i, 0, 0))
    cmp_spec = pltpu.PrefetchScalarGridSpec(
        num_scalar_prefetch=1, grid=(nb,),
        in_specs=[one1((1, B_WIDTH)), one1((1, 2 * B_KV * B_DH)), one1(win_buf.shape[1:]),
                  pl.BlockSpec(pool.shape, lambda i, pt: (0, 0)), pl.BlockSpec(memory_space=pl.ANY)],
        out_specs=[one1((B_HEADS, B_DH)), one1((B_HEADS, B_DH)), one1((8, LANES))],
        scratch_shapes=[pltpu.VMEM((2, n_pages, 2, B_KV, B_DH, PAGE_SIZE), f32),
                        pltpu.VMEM((2 * B_KV * B_DH, 2 * n_blk), f32), pltpu.SemaphoreType.DMA((2,))])
    o_cmp, o_win, idx = pl.pallas_call(
        functools.partial(_nsa_decode_cmp_kernel, n_pages=n_pages, past=past), grid_spec=cmp_spec,
        out_shape=[jax.ShapeDtypeStruct((nb, B_HEADS, B_DH), f32), jax.ShapeDtypeStruct((nb, B_HEADS, B_DH), f32),
                   jax.ShapeDtypeStruct((nb, 8, LANES), jnp.int32)],
        compiler_params=_cparams(("arbitrary",)), name="nsa_decode_cmp",
    )(page_table, qb, win_new, win_buf, pool, cache_t)
    picks = idx[:, :B_KV, :n_pick]
    one2 = lambda shape: pl.BlockSpec((None,) + shape, lambda i, pt, sel: (i, 0, 0))
    sel_spec = pltpu.PrefetchScalarGridSpec(
        num_scalar_prefetch=2, grid=(nb,),
        in_specs=[one2((1, B_WIDTH)), one2((1, 4 * B_KV * B_DH)), one2((1, LANES)),
                  one2((B_HEADS, B_DH)), one2((B_HEADS, B_DH)), pl.BlockSpec(memory_space=pl.ANY)],
        out_specs=one2((1, B_WIDTH)),
        scratch_shapes=[pltpu.VMEM((2, B_KV * n_pick, 2, B_DH, PAGE_SIZE), f32), pltpu.SemaphoreType.DMA((2,))])
    return pl.pallas_call(
        functools.partial(_nsa_decode_sel_kernel, past=past), grid_spec=sel_spec,
        out_shape=jax.ShapeDtypeStruct((nb, 1, B_WIDTH), f32),
        compiler_params=_cparams(("arbitrary",)), name="nsa_decode_sel",
    )(page_table, picks, qb, rows_new, small, o_cmp, o_win, cache_t)


SAMPLE_PAD = DELTA_CHUNK


def _even_weights(w_in):
    o = 0
    cols = {}
    for name, n in (("qkv", A_CONV_CH), ("beta", A_HEADS), ("a", A_HEADS), ("gate", A_WIDTH), ("qb", B_WIDTH),
                    ("kv", N_BRANCH * 2 * B_KV * B_DH), ("gb", B_HEADS * N_BRANCH)):
        cols[name] = w_in[:, o:o + n]
        o += n
    n_rows = 4 * B_KV * B_DH
    small = jnp.concatenate([cols["beta"], cols["a"], cols["gb"]], axis=1)
    small = jnp.pad(small, ((0, 0), (0, LANES - small.shape[1])))
    ws = [cols["qkv"], cols["gate"], cols["qb"], cols["kv"][:, :n_rows], cols["kv"][:, n_rows:], small]
    return [w.astype(bf16) for w in ws]


def _odd_weights(w_in):
    o = 0
    ws = []
    for n in (C_QK, C_QK, C_WIDTH, C_LOWRANK, C_WIDTH):
        ws.append(w_in[:, o:o + n])
        o += n
    ws[3] = jnp.pad(ws[3], ((0, 0), (0, LANES - C_LOWRANK)))
    return [w.astype(bf16) for w in ws]


def _lane_param(v, lane0):
    return jnp.zeros((1, LANES), f32).at[0, lane0:lane0 + v.shape[0]].set(v.astype(f32))


def _pad_rows(a, nb):
    return jnp.pad(a.reshape(nb, 1, -1), ((0, 0), (0, SAMPLE_PAD - 1), (0, 0)))


def kernel(x_prompt, x_sample, cache_nsa_kv, state_nsa_win, state_delta_conv, state_delta_S, state_gla_S,
           page_table, w_in_even, conv_w_delta, delta_A_log, delta_dt_bias, delta_norm_g, w_out_even,
           w_in_odd, w_gla_gate2, b_gla_gate2, gla_norm_g, w_out_odd, w_ffn_gate, w_ffn_up, w_ffn_down,
           ln_g, ln_b):
    bp, t, d = x_prompt.shape
    bs = x_sample.shape[0]
    assert x_sample.shape[1] == 1 and w_in_even.shape[0] == 1 and w_in_odd.shape[0] == 1
    n_pages = page_table.shape[1]
    past = n_pages * PAGE_SIZE
    xp = x_prompt.reshape(bp * t, d)
    xs = x_sample.reshape(bs, d)
    ln = lambda layer, j: (ln_g[layer, j].reshape(1, d), ln_b[layer, j].reshape(1, d))
    ffn_w = lambda layer: (w_ffn_gate[layer].astype(bf16), w_ffn_up[layer].astype(bf16),
                           w_ffn_down[layer].astype(bf16))

    ws = _even_weights(w_in_even[0])
    conv_w = conv_w_delta[0]
    alog = _lane_param(delta_A_log[0], A_HEADS)
    dtb = _lane_param(delta_dt_bias[0], A_HEADS)
    dng = delta_norm_g[0].reshape(1, A_DV)
    wo = w_out_even[0].astype(bf16)
    wo_parts = [wo[:A_WIDTH], wo[A_WIDTH:]]

    qkv_p, gate_p, qb_p, rows_p, win_p, small_p, rows_planes_p = _proj(xp, ws, planes_of=3)
    r3 = lambda a: a.reshape(bp, t, -1)
    o_a_p, ds_p = _delta(r3(qkv_p), r3(small_p), r3(gate_p), jnp.zeros((bp, 8, A_CONV_CH), f32), conv_w,
                         alog, dtb, dng, jnp.zeros((bp, A_HEADS, A_DK, A_DV), f32), t_valid=t)
    o_b_p = _nsa_prompt(r3(qb_p), r3(rows_p), r3(win_p), r3(small_p))
    xp = _outproj_ln([o_a_p.reshape(bp * t, -1), o_b_p.reshape(bp * t, -1)], wo_parts, xp, *ln(0, 0))
    xp = _ffn_ln(xp, *ffn_w(0), *ln(0, 1))

    qkv_s, gate_s, qb_s, rows_s, win_s, small_s = _proj(xs, ws)
    conv_s = state_delta_conv[0]
    o_a_s, ds_s = _delta(_pad_rows(qkv_s, bs), _pad_rows(small_s, bs), _pad_rows(gate_s, bs),
                         jnp.pad(conv_s, ((0, 0), (8 - (CONV_W - 1), 0), (0, 0))), conv_w, alog, dtb, dng,
                         state_delta_S[0], t_valid=1)
    win_buf = state_nsa_win[0]
    wb = win_buf.shape[1]
    r1 = lambda a: a.reshape(bs, 1, -1)
    o_b_s = _nsa_decode(page_table, r1(qb_s), r1(rows_s), r1(win_s), r1(small_s), win_buf.reshape(bs, wb, -1),
                        cache_nsa_kv[:, 0].transpose(0, 2, 3, 4, 1))
    xs = _outproj_ln([o_a_s[:, 0], o_b_s.reshape(bs, -1)], wo_parts, xs, *ln(0, 0))
    xs = _ffn_ln(xs, *ffn_w(0), *ln(0, 1))

    kvd = (B_KV, B_DH)
    nsa_rows_p = rows_planes_p.reshape((1, bp, t, 4) + kvd)
    nsa_win_p = r3(win_p)[:, -min(WINDOW, t):].reshape((1, bp, min(WINDOW, t), 2) + kvd)
    delta_conv_p = jnp.concatenate([jnp.zeros((bp, CONV_W - 1, A_CONV_CH), f32), r3(qkv_p)], axis=1)[:, -(CONV_W - 1):][None]
    nsa_rows_s = rows_s.reshape((1, bs, 1, 4) + kvd)
    win_cat = jnp.concatenate([win_buf, win_s.reshape((bs, 1, 2) + kvd)], axis=1)
    nsa_win_s = win_cat[:, -min(WINDOW, wb + 1):][None]
    delta_conv_s = jnp.concatenate([conv_s, qkv_s[:, None, :]], axis=1)[:, -(CONV_W - 1):][None]

    wq, wk, wv, wg1, wr = _odd_weights(w_in_odd[0])
    wg2 = jnp.pad(w_gla_gate2[0], ((0, LANES - C_LOWRANK), (0, 0))).astype(bf16)
    bg2 = b_gla_gate2[0].reshape(1, C_QK)
    gng = gla_norm_g[0].reshape(1, C_DV)
    wo1 = [w_out_odd[0].astype(bf16)]

    q_p, k_p, v_p, g1_p, rr_p = _proj(xp, [wq, wk, wv, wg1, wr])
    o_c_p, gs_p = _gla(r3(q_p), r3(k_p), r3(v_p), r3(rr_p), r3(g1_p), wg2, bg2, gng,
                       jnp.zeros((bp, C_HEADS, C_DK, C_DV), f32), t_valid=t)
    xp = _outproj_ln([o_c_p.reshape(bp * t, -1)], wo1, xp, *ln(1, 0))
    xp = _ffn_ln(xp, *ffn_w(1), *ln(1, 1))

    q_s, k_s, v_s, g1_s, rr_s = _proj(xs, [wq, wk, wv, wg1, wr])
    o_c_s, gs_s = _gla(_pad_rows(q_s, bs), _pad_rows(k_s, bs), _pad_rows(v_s, bs), _pad_rows(rr_s, bs),
                       _pad_rows(g1_s, bs), wg2, bg2, gng, state_gla_S[0], t_valid=1)
    xs = _outproj_ln([o_c_s[:, 0]], wo1, xs, *ln(1, 0))
    xs = _ffn_ln(xs, *ffn_w(1), *ln(1, 1))

    return (xp.reshape(bp, t, d), xs.reshape(bs, 1, d),
            nsa_rows_p, nsa_win_p, delta_conv_p, ds_p[None], gs_p[None],
            nsa_rows_s, nsa_win_s, delta_conv_s, ds_s[None], gs_s[None])
```

```python
import functools
import math

import jax
import jax.numpy as jnp
from jax import lax
from jax.experimental import pallas as pl
from jax.experimental.pallas import tpu as pltpu

f32 = jnp.float32
bf16 = jnp.bfloat16
HI = lax.Precision.HIGHEST

D_MODEL = 1024
DEPTH = 2
PAGE_SIZE = 128

A_HEADS = 4
A_DK = 128
A_DV = 128
A_QK = A_HEADS * A_DK
A_WIDTH = A_HEADS * A_DV
A_CONV_CH = 2 * A_QK + A_WIDTH
CONV_W = 4
DELTA_CHUNK = 64

B_HEADS = 8
B_KV = 2
B_HPG = B_HEADS // B_KV
B_DH = 64
B_WIDTH = B_HEADS * B_DH
N_BRANCH = 3
CMP_BLK = 32
SEL_BLK = 64
SEL_TOPK = 16
WINDOW = 512
B_SCALE = B_DH ** -0.5

C_HEADS = 4
C_DK = 128
C_DV = 256
C_QK = C_HEADS * C_DK
C_WIDTH = C_HEADS * C_DV
C_LOWRANK = 16
GLA_TAU = 16.0
GLA_CHUNK = 64
GLA_SUB = 16
GLA_PREP_CHUNKS = 2

D_FF = -(-(8 * D_MODEL) // (3 * 256)) * 256
ALPHA = (2 * DEPTH) ** 0.25
NEG = -1e30
FORCE = 1e9

LANES = 128
VMEM_LIMIT = 56 * 1024 * 1024


def _cparams(sem):
    return pltpu.CompilerParams(dimension_semantics=sem, vmem_limit_bytes=VMEM_LIMIT)


def _dot(a, b, precision=None):
    return jnp.dot(a, b, preferred_element_type=f32, precision=precision)


def _dot_nt(a, b, precision=None):
    return lax.dot_general(a, b, (((1,), (1,)), ((), ())), preferred_element_type=f32, precision=precision)


def _dot_tn(a, b, precision=None):
    return lax.dot_general(a, b, (((0,), (0,)), ((), ())), preferred_element_type=f32, precision=precision)


def _silu(x):
    return x * jax.nn.sigmoid(x)


def _layer_norm(x, g, b, eps=1e-5):
    mu = jnp.mean(x, -1, keepdims=True)
    xc = x - mu
    var = jnp.mean(xc * xc, -1, keepdims=True)
    return xc * lax.rsqrt(var + eps) * g + b


def _row_tile(m, pref):
    t = min(pref, m)
    assert m % t == 0
    return t


def _proj_kernel(x_ref, *refs, n, planes_of):
    xb = x_ref[...].astype(bf16)
    tm = x_ref.shape[0]
    for idx, (w_ref, o_ref) in enumerate(zip(refs[:n], refs[n:2 * n])):
        val = _dot(xb, w_ref[...])
        o_ref[...] = val
        if idx == planes_of:
            p_ref = refs[2 * n]
            n_planes = val.shape[1] // B_DH
            for j in range(n_planes):
                p_ref[pl.ds(j, tm, stride=n_planes), :] = val[:, j * B_DH:(j + 1) * B_DH]


def _proj(x, weights, tm=512, planes_of=None):
    m, k = x.shape
    tm = _row_tile(m, tm)
    in_specs = [pl.BlockSpec((tm, k), lambda i: (i, 0))]
    in_specs += [pl.BlockSpec(w.shape, lambda i: (0, 0)) for w in weights]
    out_specs = [pl.BlockSpec((tm, w.shape[1]), lambda i: (i, 0)) for w in weights]
    out_shape = [jax.ShapeDtypeStruct((m, w.shape[1]), f32) for w in weights]
    if planes_of is not None:
        n_planes = weights[planes_of].shape[1] // B_DH
        out_specs.append(pl.BlockSpec((tm * n_planes, B_DH), lambda i: (i, 0)))
        out_shape.append(jax.ShapeDtypeStruct((m * n_planes, B_DH), f32))
    return pl.pallas_call(
        functools.partial(_proj_kernel, n=len(weights), planes_of=planes_of),
        grid=(m // tm,), in_specs=in_specs, out_specs=out_specs, out_shape=out_shape,
        compiler_params=_cparams(("parallel",)), name="in_proj")(x, *weights)


def _outproj_ln_kernel(*refs, n_parts):
    parts = refs[:n_parts]
    ws = refs[n_parts:2 * n_parts]
    x_ref, g_ref, b_ref, o_ref = refs[2 * n_parts:]
    y = None
    for p_ref, w_ref in zip(parts, ws):
        d = _dot(p_ref[...].astype(bf16), w_ref[...])
        y = d if y is None else y + d
    o_ref[...] = _layer_norm(ALPHA * x_ref[...] + y, g_ref[...], b_ref[...])


def _outproj_ln(parts, weights, x, g, b, tm=512):
    m, d = x.shape
    tm = _row_tile(m, tm)
    n = len(parts)
    in_specs = [pl.BlockSpec((tm, p.shape[1]), lambda i: (i, 0)) for p in parts]
    in_specs += [pl.BlockSpec(w.shape, lambda i: (0, 0)) for w in weights]
    in_specs += [pl.BlockSpec((tm, d), lambda i: (i, 0)),
                 pl.BlockSpec((1, d), lambda i: (0, 0)), pl.BlockSpec((1, d), lambda i: (0, 0))]
    return pl.pallas_call(
        functools.partial(_outproj_ln_kernel, n_parts=n), grid=(m // tm,), in_specs=in_specs,
        out_specs=pl.BlockSpec((tm, d), lambda i: (i, 0)), out_shape=jax.ShapeDtypeStruct((m, d), f32),
        compiler_params=_cparams(("parallel",)), name="out_proj_ln")(*parts, *weights, x, g, b)


FFN_COLS = 256


def _ffn_ln_kernel(x_ref, wg_ref, wu_ref, wd_ref, g_ref, b_ref, o_ref):
    x = x_ref[...]
    xb = x.astype(bf16)
    acc = jnp.zeros(x.shape, f32)
    for c in range(0, D_FF, FFN_COLS):
        hg = _dot(xb, wg_ref[:, c:c + FFN_COLS])
        hu = _dot(xb, wu_ref[:, c:c + FFN_COLS])
        h = (_silu(hg) * hu).astype(bf16)
        acc = acc + _dot(h, wd_ref[c:c + FFN_COLS, :])
    o_ref[...] = _layer_norm(ALPHA * x + acc, g_ref[...], b_ref[...])


def _ffn_ln(x, wg, wu, wd, g, b, tm=512):
    m, d = x.shape
    tm = _row_tile(m, tm)
    full = lambda a: pl.BlockSpec(a.shape, lambda i: (0, 0))
    return pl.pallas_call(
        _ffn_ln_kernel, grid=(m // tm,),
        in_specs=[pl.BlockSpec((tm, d), lambda i: (i, 0)), full(wg), full(wu), full(wd), full(g), full(b)],
        out_specs=pl.BlockSpec((tm, d), lambda i: (i, 0)), out_shape=jax.ShapeDtypeStruct((m, d), f32),
        compiler_params=_cparams(("parallel",)), name="ffn_ln")(x, wg, wu, wd, g, b)


DELTA_INV_BLK = 16


def _tri_masks(c):
    r = lax.broadcasted_iota(jnp.int32, (c, c), 0)
    s = lax.broadcasted_iota(jnp.int32, (c, c), 1)
    return r, s


def _split2(x):
    hi = x.astype(bf16)
    return hi, (x - hi.astype(f32)).astype(bf16)


def _split3(x):
    hi = x.astype(bf16)
    rest = x - hi.astype(f32)
    mid = rest.astype(bf16)
    return hi, mid, (rest - mid.astype(f32)).astype(bf16)


def _dot_x3(a, b):
    ah, al = _split2(a)
    bh, bl = _split2(b)
    return _dot(ah, bh) + (_dot(ah, bl) + _dot(al, bh))


def _dot_ones(a_ones, b):
    a16 = a_ones.astype(bf16)
    b1, b2, b3 = _split3(b)
    return _dot(a16, b1) + (_dot(a16, b2) + _dot(a16, b3))


def _unit_lower_inverses(mats, r, s):
    c = mats[0].shape[0]
    eye = (r == s).astype(f32)
    same_blk = (r // DELTA_INV_BLK) == (s // DELTA_INV_BLK)
    diag = [jnp.where(same_blk, a, 0.0) for a in mats]
    low = [a - d for a, d in zip(mats, diag)]
    pw = [-d for d in diag]
    p = [eye + x for x in pw]
    k = 2
    while k < DELTA_INV_BLK:
        pw = [_dot_x3(x, x) for x in pw]
        p = [pi + _dot_x3(pi, x) for pi, x in zip(p, pw)]
        k *= 2
    pw = [_dot_x3(pi, lo) for pi, lo in zip(p, low)]
    q = [eye - x for x in pw]
    k = 2
    while k < c // DELTA_INV_BLK:
        pw = [_dot_x3(x, x) for x in pw]
        q = [qi + _dot_x3(qi, x) for qi, x in zip(q, pw)]
        k *= 2
    return [_dot_x3(qi, pi) for qi, pi in zip(q, p)]


DELTA_PREP_CHUNKS = 4


def _delta_kernel(qkv_ref, small_ref, gate_ref, cbuf_ref, cw_ref, alog_ref, dtb_ref, dng_ref, s0_ref,
                  o_ref, s_ref, xbuf, ybuf, tail, u_ref, w_ref, qg_ref, kd_ref, qk_ref, gl_ref,
                  *, tt, t_valid, t_total):
    t = pl.program_id(1)
    c = DELTA_CHUNK
    n_chunks = tt // c
    cpi = math.gcd(DELTA_PREP_CHUNKS, n_chunks)

    @pl.when(t == 0)
    def _():
        s_ref[...] = s0_ref[...]
        tail[...] = cbuf_ref[...]

    xbuf[0:8, :] = tail[...]
    xbuf[8:8 + tt, :] = qkv_ref[...]
    tail[...] = xbuf[tt:tt + 8, :]

    def conv_body(ci, carry):
        r0 = pl.multiple_of(ci * c, c)
        for col in range(0, A_CONV_CH, A_QK):
            cols = slice(col, col + A_QK)
            x = xbuf[pl.ds(r0, c + 8), cols]
            first = 8 - (CONV_W - 1)
            y = x[first:first + c] * cw_ref[0:1, cols]
            for j in range(1, CONV_W):
                y = y + x[first + j:first + j + c] * cw_ref[j:j + 1, cols]
            ybuf[pl.ds(r0, c), cols] = _silu(y)
        return carry

    lax.fori_loop(0, n_chunks, conv_body, 0)

    r, s = _tri_masks(c)
    causal = r >= s
    strict = r > s
    lower_ones = causal.astype(f32)
    upper4 =jnp.concatenate([(r <= s).astype(f32)] * A_HEADS, axis=1)
    all_ones = jnp.ones((c, c), f32)
    alog = alog_ref[...]
    dtb = dtb_ref[...]
    dng = dng_ref[...]

    def prep_body(ci, carry):
        items = []
        for cc in range(cpi):
            r0 = pl.multiple_of((ci * cpi + cc) * c, c)
            sm = small_ref[pl.ds(r0, c), :]
            beta_all = jax.nn.sigmoid(sm)
            g_all = -jnp.exp(alog) * jax.nn.softplus(sm + dtb)
            if t_valid < t_total:
                live = t * tt + r0 + lax.broadcasted_iota(jnp.int32, (c, 1), 0) < t_valid
                beta_all = jnp.where(live, beta_all, 0.0)
                g_all = jnp.where(live, g_all, 0.0)
            gam_c_all = _dot_ones(lower_ones, g_all)
            g_rows = jnp.concatenate(
                [jnp.broadcast_to(g_all[:, A_HEADS + h:A_HEADS + h + 1], (c, c)) for h in range(A_HEADS)], axis=1)
            gam_r_all = _dot_ones(all_ones, g_rows * upper4)
            for h in range(A_HEADS):
                qh = ybuf[pl.ds(r0, c), h * A_DK:(h + 1) * A_DK]
                kh = ybuf[pl.ds(r0, c), A_QK + h * A_DK:A_QK + (h + 1) * A_DK]
                vh = ybuf[pl.ds(r0, c), 2 * A_QK + h * A_DV:2 * A_QK + (h + 1) * A_DV]
                qh = qh * lax.rsqrt(jnp.sum(qh * qh, -1, keepdims=True) + 1e-6) * (A_DK ** -0.5)
                kh = kh * lax.rsqrt(jnp.sum(kh * kh, -1, keepdims=True) + 1e-6)
                if t_valid < t_total:
                    qh = jnp.where(live, qh, 0.0)
                    kh = jnp.where(live, kh, 0.0)
                    vh = jnp.where(live, vh, 0.0)
                beta = beta_all[:, h:h + 1]
                gam_c = jnp.broadcast_to(gam_c_all[:, A_HEADS + h:A_HEADS + h + 1], (c, A_DK))
                diff = gam_c[:, :c] - gam_r_all[:, h * c:(h + 1) * c]
                ldec = jnp.where(causal, jnp.exp(jnp.where(causal, diff, 0.0)), 0.0)
                items.append(dict(r0=r0, h=h, q=qh, k=kh, v=vh, beta=beta, gam=gam_c, ldec=ldec, kb=kh * beta))
        amats = [jnp.where(strict, _dot_nt(it["kb"], it["k"]) * it["ldec"], 0.0) for it in items]
        tms = _unit_lower_inverses(amats, r, s)
        for it, tm in zip(items, tms):
            r0, h = it["r0"], it["h"]
            cols = slice(h * A_DK, (h + 1) * A_DK)
            eg = jnp.exp(it["gam"])
            g_last = it["gam"][c - 1:c, :]
            u_ref[pl.ds(r0, c), cols] = _dot(tm, it["v"] * it["beta"])
            w_ref[pl.ds(r0, c), cols] = _dot(tm, it["kb"] * eg)
            qg_ref[pl.ds(r0, c), cols] = it["q"] * eg
            kd_ref[pl.ds(r0, c), cols] = it["k"] * jnp.exp(g_last - it["gam"])
            qk_ref[pl.ds(r0, c), h * c:(h + 1) * c] = _dot_nt(it["q"], it["k"]) * it["ldec"]
            gl_ref[pl.ds(r0, 1), cols] = jnp.exp(g_last)
        return carry

    lax.fori_loop(0, n_chunks // cpi, prep_body, 0)

    def scan_body(ci, carry):
        r0 = pl.multiple_of(ci * c, c)
        for h in range(A_HEADS):
            cols = slice(h * A_DK, (h + 1) * A_DK)
            st = s_ref[h]
            v_new = u_ref[pl.ds(r0, c), cols] - _dot(w_ref[pl.ds(r0, c), cols], st)
            o = _dot(qg_ref[pl.ds(r0, c), cols], st) + _dot(qk_ref[pl.ds(r0, c), h * c:(h + 1) * c], v_new)
            s_ref[h] = st * gl_ref[pl.ds(r0, 1), cols] + _dot_tn(kd_ref[pl.ds(r0, c), cols], v_new)
            o = o * lax.rsqrt(jnp.mean(o * o, -1, keepdims=True) + 1e-6) * dng
            gt = gate_ref[pl.ds(r0, c), h * A_DV:(h + 1) * A_DV]
            o_ref[pl.ds(r0, c), h * A_DV:(h + 1) * A_DV] = o * _silu(gt)
        return carry

    lax.fori_loop(0, n_chunks, scan_body, 0)


def _delta(qkv, small, gate, conv_buf8, conv_w, alog, dtb, dng, s0, *, t_valid, tt=512):
    b, t_total, _ = qkv.shape
    tt = _row_tile(t_total, tt)
    nt = t_total // tt
    row = lambda w: pl.BlockSpec((None, tt, w), lambda i, j: (i, j, 0))
    full2 = lambda a: pl.BlockSpec(a.shape, lambda i, j: (0, 0))
    st_spec = pl.BlockSpec((None, A_HEADS, A_DK, A_DV), lambda i, j: (i, 0, 0, 0))
    kern = functools.partial(_delta_kernel, tt=tt, t_valid=t_valid, t_total=t_total)
    return pl.pallas_call(
        kern, grid=(b, nt),
        in_specs=[row(A_CONV_CH), row(LANES), row(A_WIDTH),
                  pl.BlockSpec((None, 8, A_CONV_CH), lambda i, j: (i, 0, 0)),
                  full2(conv_w), full2(alog), full2(dtb), full2(dng), st_spec],
        out_specs=[row(A_WIDTH), st_spec],
        out_shape=[jax.ShapeDtypeStruct((b, t_total, A_WIDTH), f32),
                   jax.ShapeDtypeStruct((b, A_HEADS, A_DK, A_DV), f32)],
        scratch_shapes=[pltpu.VMEM((tt + 8, A_CONV_CH), f32), pltpu.VMEM((tt, A_CONV_CH), f32),
                        pltpu.VMEM((8, A_CONV_CH), f32)]
        + [pltpu.VMEM((tt, A_QK), f32)] * 4
        + [pltpu.VMEM((tt, A_HEADS * DELTA_CHUNK), f32), pltpu.VMEM((tt, A_QK), f32)],
        compiler_params=_cparams(("parallel", "arbitrary")), name="delta_mixer",
    )(qkv, small, gate, conv_buf8, conv_w, alog, dtb, dng, s0)


def _gla_kernel(q_ref, k_ref, v_ref, r_ref, g1_ref, wg2_ref, bg2_ref, gng_ref, s0_ref,
                o_ref, s_ref, st_ref, la_ref, qe_ref, gl_ref, ds_ref, *, tt, t_valid, t_total):
    t = pl.program_id(1)
    nt = pl.num_programs(1)
    c = GLA_CHUNK
    heads = range(C_HEADS)
    kcols = [slice(h * C_DK, (h + 1) * C_DK) for h in heads]
    vcols = [slice(h * C_DV, (h + 1) * C_DV) for h in heads]

    @pl.when(t == 0)
    def _():
        for h in range(C_HEADS):
            st_ref[h] = s0_ref[h].T

    la = jax.nn.log_sigmoid(_dot(g1_ref[...].astype(bf16), wg2_ref[...]) + bg2_ref[...]) / GLA_TAU
    if t_valid < t_total:
        rows = t * tt + lax.broadcasted_iota(jnp.int32, (tt, 1), 0)
        la = jnp.where(rows < t_valid, la, 0.0)
    la_ref[...] = la

    r, s = _tri_masks(c)
    lower_ones = (r >= s).astype(f32)
    rs = lax.broadcasted_iota(jnp.int32, (GLA_SUB, c), 0)
    ss = lax.broadcasted_iota(jnp.int32, (GLA_SUB, c), 1)
    krow = lax.broadcasted_iota(jnp.int32, (c, 1), 0)
    gng = gng_ref[...]

    n_chunks = tt // c
    cpi = math.gcd(GLA_PREP_CHUNKS, n_chunks)

    def prep_body(ci, carry):
        items = []
        for cc in range(cpi):
            chunk = ci * cpi + cc
            r0 = pl.multiple_of(chunk * c, c)
            rows = pl.ds(r0, c)
            bc_all = _dot_ones(lower_ones, la_ref[rows, :])
            for h in heads:
                items.append(dict(chunk=chunk, r0=r0, rows=rows, h=h, q=q_ref[rows, kcols[h]] * (C_DK ** -0.5),
                                  k=k_ref[rows, kcols[h]], v=v_ref[rows, vcols[h]], bc=bc_all[:, kcols[h]]))
        for it in items:
            qe_ref[it["rows"], kcols[it["h"]]] = it["q"] * jnp.exp(it["bc"])
        for a in range(c // GLA_SUB):
            lo, hi = a * GLA_SUB, (a + 1) * GLA_SUB
            seen = krow < hi
            att = []
            for it in items:
                bref = it["bc"][lo - 1:lo, :] if a > 0 else jnp.zeros((1, C_DK), f32)
                qa = it["q"][lo:hi, :] * jnp.exp(it["bc"][lo:hi, :] - bref)
                ka = jnp.where(seen, it["k"] * jnp.exp(jnp.where(seen, bref - it["bc"], 0.0)), 0.0)
                att.append(jnp.where(rs + lo >= ss, _dot_nt(qa, ka), 0.0))
            for it, at in zip(items, att):
                o_ref[pl.ds(it["r0"] + lo, GLA_SUB), vcols[it["h"]]] = _dot(at, it["v"])
        for it in items:
            b_last = it["bc"][c - 1:c, :]
            gl_ref[pl.ds(it["r0"], 1), kcols[it["h"]]] = jnp.exp(b_last)
            ds_ref[it["chunk"], it["h"]] = _dot_tn(it["v"], it["k"] * jnp.exp(b_last - it["bc"]))
        return carry

    lax.fori_loop(0, n_chunks // cpi, prep_body, 0)

    def scan_body(ci, carry):
        r0 = pl.multiple_of(ci * c, c)
        rows = pl.ds(r0, c)
        for h in heads:
            st = st_ref[h]
            o = _dot_nt(qe_ref[rows, kcols[h]], st) + o_ref[rows, vcols[h]]
            st_ref[h] = st * gl_ref[pl.ds(r0, 1), kcols[h]] + ds_ref[ci, h]
            o = o * lax.rsqrt(jnp.mean(o * o, -1, keepdims=True) + 1e-6) * gng
            o_ref[rows, vcols[h]] = o * _silu(r_ref[rows, vcols[h]])
        return carry

    lax.fori_loop(0, tt // c, scan_body, 0)

    @pl.when(t == nt - 1)
    def _():
        for h in range(C_HEADS):
            s_ref[h] = st_ref[h].T


def _gla(q, k, v, r, g1, wg2, bg2, gng, s0, *, t_valid, tt=512):
    b, t_total, _ = q.shape
    tt = _row_tile(t_total, tt)
    nt = t_total // tt
    row = lambda w: pl.BlockSpec((None, tt, w), lambda i, j: (i, j, 0))
    full2 = lambda a: pl.BlockSpec(a.shape, lambda i, j: (0, 0))
    st_spec = pl.BlockSpec((None, C_HEADS, C_DK, C_DV), lambda i, j: (i, 0, 0, 0))
    kern = functools.partial(_gla_kernel, tt=tt, t_valid=t_valid, t_total=t_total)
    return pl.pallas_call(
        kern, grid=(b, nt),
        in_specs=[row(C_QK), row(C_QK), row(C_WIDTH), row(C_WIDTH), row(LANES),
                  full2(wg2), full2(bg2), full2(gng), st_spec],
        out_specs=[row(C_WIDTH), st_spec],
        out_shape=[jax.ShapeDtypeStruct((b, t_total, C_WIDTH), f32),
                   jax.ShapeDtypeStruct((b, C_HEADS, C_DK, C_DV), f32)],
        scratch_shapes=[pltpu.VMEM((C_HEADS, C_DV, C_DK), f32), pltpu.VMEM((tt, C_QK), f32),
                        pltpu.VMEM((tt, C_QK), f32), pltpu.VMEM((tt, C_QK), f32),
                        pltpu.VMEM((tt // GLA_CHUNK, C_HEADS, C_DV, C_DK), f32)],
        compiler_params=_cparams(("parallel", "arbitrary")), name="gla_mixer",
    )(q, k, v, r, g1, wg2, bg2, gng, s0)


GB_LANE0 = 2 * A_HEADS


def _masked_softmax(s, mask):
    sm = jnp.where(mask, s, NEG)
    e = jnp.exp(sm - jnp.max(sm, -1, keepdims=True))
    p = e / jnp.sum(e, -1, keepdims=True)
    return jnp.where(mask, p, 0.0)


def _topk_rank(score, n_cand):
    lane = lax.broadcasted_iota(jnp.int32, score.shape, score.ndim - 1)
    rank = jnp.zeros(score.shape, jnp.int32)
    for i in range(n_cand):
        si = score[..., i:i + 1]
        ahead = (si > score) | ((si == score) & (i < lane))
        rank = rank + ahead.astype(jnp.int32)
    return rank


def _block_scores(imp, qpos, n_blk):
    blk = lax.broadcasted_iota(jnp.int32, imp.shape, imp.ndim - 1)
    cur = qpos // SEL_BLK
    valid = blk * SEL_BLK <= qpos
    forced = (blk == 0) | (blk == cur) | (blk == cur - 1)
    return jnp.where(forced, FORCE, jnp.where(valid, imp, NEG))


def _split_cmp_index(col, half):
    return jnp.where(col < half, 2 * col, 2 * (col - half) + 1)


NSA_TQ = 128
NSA_TK = 512


def _topk_mask_rows(score, k):
    n = score.shape[0]
    row = lax.broadcasted_iota(jnp.int32, score.shape, 0)
    rank = jnp.zeros(score.shape, jnp.int32)
    for i in range(n):
        si = score[i:i + 1, :]
        ahead = (si > score) | ((si == score) & (i < row))
        rank = rank + ahead.astype(jnp.int32)
    return (rank < k).astype(f32)


def _nsa_prompt_kernel(q_ref, rows_ref, win_ref, small_ref, o_ref, cb_ref, sel_ref, *, t_total):
    i = pl.program_id(1)
    tq = NSA_TQ
    tk = min(NSA_TK, t_total)
    tw = min(WINDOW + tq, t_total)
    n_sel = t_total // SEL_BLK
    half = n_sel
    hq = B_HPG * tq
    blk_per_tile = tk // SEL_BLK

    @pl.when(i == 0)
    def _():
        x = rows_ref[:, 0:2 * B_KV * B_DH].reshape(n_sel, SEL_BLK, 2 * B_KV * B_DH)
        cb_ref[0:half, :] = jnp.sum(x[:, :CMP_BLK, :], axis=1) * (1.0 / CMP_BLK)
        cb_ref[half:2 * half, :] = jnp.sum(x[:, CMP_BLK:, :], axis=1) * (1.0 / CMP_BLK)

    qpos = i * tq + lax.broadcasted_iota(jnp.int32, (1, tq), 1)
    qpos4 = jnp.concatenate([qpos] * B_HPG, axis=1)
    gates_t = jax.nn.sigmoid(small_ref[...].T)
    krow = lax.broadcasted_iota(jnp.int32, (tk, 1), 0)
    wrow = lax.broadcasted_iota(jnp.int32, (tw, 1), 0)
    n_causal = (i * tq + tq - 1) // tk + 1
    groups = range(B_KV)
    q4 = [jnp.concatenate([q_ref[:, (g * B_HPG + h) * B_DH:(g * B_HPG + h + 1) * B_DH] for h in range(B_HPG)],
                          axis=0) * B_SCALE for g in groups]

    o_cmp = []
    for g in groups:
        kc = cb_ref[:, g * B_DH:(g + 1) * B_DH]
        vc = cb_ref[:, (B_KV + g) * B_DH:(B_KV + g + 1) * B_DH]
        st = _dot_nt(kc, q4[g])
        cidx = _split_cmp_index(lax.broadcasted_iota(jnp.int32, (2 * half, 1), 0), half)
        cmask = (cidx + 1) * CMP_BLK <= qpos4 + 1
        sm = jnp.where(cmask, st, NEG)
        e = jnp.exp(sm - jnp.max(sm, axis=0, keepdims=True))
        p = jnp.where(cmask, e / jnp.sum(e, axis=0, keepdims=True), 0.0)
        o_cmp.append(_dot_tn(vc, p))
        imp = p[:, 0:tq]
        for h in range(1, B_HPG):
            imp = imp + p[:, h * tq:(h + 1) * tq]
        imp = imp[:half] + imp[half:]
        blk = lax.broadcasted_iota(jnp.int32, (n_sel, 1), 0)
        cur = qpos // SEL_BLK
        forced = (blk == 0) | (blk == cur) | (blk == cur - 1)
        score = jnp.where(forced, FORCE, jnp.where(blk * SEL_BLK <= qpos, imp, NEG))
        sel_ref[g] = _topk_mask_rows(score, min(SEL_TOPK, n_sel))

    def attend(qg, k, v, live, carry):
        m, l, acc = carry
        bias = jnp.where(live, 0.0, NEG)
        sc = _dot_nt(k, qg) + jnp.concatenate([bias] * B_HPG, axis=1)
        m_new = jnp.maximum(m, jnp.max(sc, axis=0, keepdims=True))
        pe = jnp.exp(sc - m_new)
        alpha = jnp.exp(m - m_new)
        l = alpha * l + jnp.sum(pe, axis=0, keepdims=True)
        acc = alpha * acc + _dot_tn(v, pe)
        return m_new, l, acc

    init = (jnp.full((1, hq), NEG, f32), jnp.zeros((1, hq), f32), jnp.zeros((B_DH, hq), f32))

    def sel_step(j, carries):
        k0 = pl.multiple_of(j * tk, tk)
        causal = k0 + krow <= qpos
        out = []
        for g in groups:
            flags = sel_ref[g, pl.ds(pl.multiple_of(j * blk_per_tile, blk_per_tile), blk_per_tile), :]
            flags = jnp.concatenate(
                [jnp.broadcast_to(flags[u:u + 1, :], (SEL_BLK, tq)) for u in range(blk_per_tile)], axis=0)
            kl, vl = (2 * B_KV + g) * B_DH, (3 * B_KV + g) * B_DH
            out.append(attend(q4[g], rows_ref[pl.ds(k0, tk), kl:kl + B_DH], rows_ref[pl.ds(k0, tk), vl:vl + B_DH],
                              (flags > 0.5) & causal, carries[g]))
        return tuple(out)

    sel_out = lax.fori_loop(0, n_causal, sel_step, (init,) * B_KV)
    w0 = pl.multiple_of(jnp.maximum((i + 1) * tq - tw, 0), tq)
    d = qpos - (w0 + wrow)
    in_window = (d >= 0) & (d < WINDOW)
    out_rows = []
    for g in groups:
        kl, vl = g * B_DH, (B_KV + g) * B_DH
        _, l_w, acc_w = attend(q4[g], win_ref[pl.ds(w0, tw), kl:kl + B_DH], win_ref[pl.ds(w0, tw), vl:vl + B_DH],
                               in_window, init)
        o_win = acc_w / l_w
        o_sel = sel_out[g][2] / sel_out[g][1]
        for h in range(B_HPG):
            c0 = GB_LANE0 + (g * B_HPG + h) * N_BRANCH
            ls = slice(h * tq, (h + 1) * tq)
            out_rows.append(gates_t[c0:c0 + 1, :] * o_cmp[g][:, ls] + gates_t[c0 + 1:c0 + 2, :] * o_sel[:, ls]
                            + gates_t[c0 + 2:c0 + 3, :] * o_win[:, ls])
    o_ref[...] = jnp.concatenate(out_rows, axis=0).T


def _nsa_prompt(qb, rows, win, small):
    b, t_total, _ = qb.shape
    tk = min(NSA_TK, t_total)
    assert t_total % tk == 0 and tk % NSA_TQ == 0 and tk % SEL_BLK == 0 and WINDOW % NSA_TQ == 0
    nq = t_total // NSA_TQ
    n_sel = t_total // SEL_BLK
    tile = lambda w: pl.BlockSpec((None, NSA_TQ, w), lambda bi, i: (bi, i, 0))
    seq = lambda w: pl.BlockSpec((None, t_total, w), lambda bi, i: (bi, 0, 0))
    return pl.pallas_call(
        functools.partial(_nsa_prompt_kernel, t_total=t_total), grid=(b, nq),
        in_specs=[tile(B_WIDTH), seq(4 * B_KV * B_DH), seq(2 * B_KV * B_DH), tile(LANES)],
        out_specs=tile(B_WIDTH), out_shape=jax.ShapeDtypeStruct((b, t_total, B_WIDTH), f32),
        scratch_shapes=[pltpu.VMEM((2 * n_sel, 2 * B_KV * B_DH), f32), pltpu.VMEM((B_KV, n_sel, NSA_TQ), f32)],
        compiler_params=_cparams(("parallel", "arbitrary")), name="nsa_prompt",
    )(qb, rows, win, small)


DECODE_POOL_PAGES = 8


def _nsa_decode_cmp_kernel(pt_ref, q_ref, wnew_ref, wbuf_ref, pool_ref, cache_ref, ocmp_ref, owin_ref, idx_ref,
                           pages, acc_ref, sem, *, n_pages, past):
    b = pl.program_id(0)
    nb = pl.num_programs(0)
    slot = b % 2
    n_hist_blk = past // SEL_BLK
    n_sel = -(-(past + 1) // SEL_BLK)
    half = n_hist_blk
    lanes_sel = 2 * half
    wb = wbuf_ref.shape[0]
    n_cmp_rows = 2 * B_KV * B_DH

    def page_copy(bb, p, sl):
        return pltpu.make_async_copy(cache_ref.at[pt_ref[bb, p], pl.ds(0, 2)], pages.at[sl, p], sem.at[sl])

    def fetch(bb, sl):
        for p in range(n_pages):
            page_copy(bb, p, sl).start()

    @pl.when(b == 0)
    def _():
        fetch(0, 0)

    @pl.when(b + 1 < nb)
    def _():
        fetch(b + 1, 1 - slot)

    for p in range(n_pages):
        page_copy(b, p, slot).wait()

    acc_ref[...] = jnp.zeros(acc_ref.shape, f32)
    ppc = math.gcd(n_pages, DECODE_POOL_PAGES)

    def pool_body(c, carry):
        x = jnp.concatenate([pages[slot, c * ppc + i].reshape(n_cmp_rows, PAGE_SIZE) for i in range(ppc)], axis=1)
        hi, lo = _split2(x)
        w = pool_ref[pl.ds(pl.multiple_of(c * ppc * PAGE_SIZE, ppc * PAGE_SIZE), ppc * PAGE_SIZE), :]
        acc_ref[...] += _dot(hi, w) + _dot(lo, w)
        return carry

    lax.fori_loop(0, n_pages // ppc, pool_body, 0)

    qpos = past
    row8 = lax.broadcasted_iota(jnp.int32, (B_HEADS, 1), 0)
    q8 = jnp.concatenate([q_ref[:, h * B_DH:(h + 1) * B_DH] for h in range(B_HEADS)], axis=0) * B_SCALE
    wnew = wnew_ref[...]
    lane = lax.broadcasted_iota(jnp.int32, (1, lanes_sel), 1)
    o_cmp, o_win, idx_rows = [], [], []
    for g in range(B_KV):
        in_g = (row8 // B_HPG) == g
        kc_t = acc_ref[g * B_DH:(g + 1) * B_DH, :]
        vc_t = acc_ref[(B_KV + g) * B_DH:(B_KV + g + 1) * B_DH, :]
        sc = _dot(q8, kc_t)
        cidx = _split_cmp_index(lane, half)
        p = _masked_softmax(sc, (cidx + 1) * CMP_BLK <= qpos + 1)
        o_cmp.append(_dot_nt(p, vc_t))
        imp = jnp.sum(jnp.where(in_g, p, 0.0), axis=0, keepdims=True)
        imp = imp[:, :half] + imp[:, half:]
        imp = jnp.concatenate([imp, jnp.zeros((1, lanes_sel - half), f32)], axis=1)
        rank = _topk_rank(_block_scores(imp, qpos, n_sel), n_sel)
        lane_f = lane.astype(f32)
        out_lane = lax.broadcasted_iota(jnp.int32, (1, LANES), 1)
        picked = jnp.zeros((1, LANES), f32)
        for r in range(min(SEL_TOPK, n_sel)):
            blk_r = jnp.sum(jnp.where((rank == r) & (lane < n_sel), lane_f, 0.0), axis=-1, keepdims=True)
            picked = picked + jnp.where(out_lane == r, blk_r, 0.0)
        idx_rows.append(picked)
        kw = wbuf_ref[:, g * B_DH:(g + 1) * B_DH]
        vw = wbuf_ref[:, (B_KV + g) * B_DH:(B_KV + g + 1) * B_DH]
        kw_new = wnew[:, g * B_DH:(g + 1) * B_DH]
        vw_new = wnew[:, (B_KV + g) * B_DH:(B_KV + g + 1) * B_DH]
        kpos = qpos - wb + lax.broadcasted_iota(jnp.int32, (1, wb), 1)
        d = qpos - kpos
        wmask = (d >= 0) & (d < WINDOW) & (kpos >= 0)
        sw = jnp.where(wmask, _dot_nt(q8, kw), NEG)
        swn = jnp.sum(q8 * kw_new, -1, keepdims=True)
        m = jnp.maximum(jnp.max(sw, -1, keepdims=True), swn)
        ew = jnp.where(wmask, jnp.exp(sw - m), 0.0)
        ewn = jnp.exp(swn - m)
        o_win.append((_dot(ew, vw) + ewn * vw_new) / (jnp.sum(ew, -1, keepdims=True) + ewn))
    lower = row8 < B_HPG
    ocmp_ref[...] = jnp.where(lower, o_cmp[0], o_cmp[1])
    owin_ref[...] = jnp.where(lower, o_win[0], o_win[1])
    idx_ref[...] = jnp.concatenate(idx_rows + [jnp.zeros((8 - B_KV, LANES), f32)], axis=0).astype(jnp.int32)


def _nsa_decode_sel_kernel(pt_ref, sel_ref, q_ref, rnew_ref, small_ref, ocmp_ref, owin_ref, cache_ref,
                           o_ref, blocks, sem, *, past):
    b = pl.program_id(0)
    nb = pl.num_programs(0)
    slot = b % 2
    n_hist_blk = past // SEL_BLK
    n_pick = sel_ref.shape[2]
    blk_per_page = PAGE_SIZE // SEL_BLK
    tok_lane = lax.broadcasted_iota(jnp.int32, (1, PAGE_SIZE), 1)

    def hist_block(bb, g, j):
        return jnp.minimum(sel_ref[bb, g, j], n_hist_blk - 1)

    def block_copy(bb, g, j, sl):
        page = pt_ref[bb, hist_block(bb, g, j) // blk_per_page]
        return pltpu.make_async_copy(cache_ref.at[page, pl.ds(2, 2), g], blocks.at[sl, g * n_pick + j], sem.at[sl])

    def fetch(bb, sl):
        for g in range(B_KV):
            for j in range(n_pick):
                block_copy(bb, g, j, sl).start()

    @pl.when(b == 0)
    def _():
        fetch(0, 0)

    @pl.when(b + 1 < nb)
    def _():
        fetch(b + 1, 1 - slot)

    for g in range(B_KV):
        for j in range(n_pick):
            block_copy(b, g, j, slot).wait()

    row8 = lax.broadcasted_iota(jnp.int32, (B_HEADS, 1), 0)
    q8 = jnp.concatenate([q_ref[:, h * B_DH:(h + 1) * B_DH] for h in range(B_HEADS)], axis=0) * B_SCALE
    rnew = rnew_ref[...]
    o_sel = []
    for g in range(B_KV):
        ks_t = jnp.concatenate([blocks[slot, g * n_pick + j, 0] for j in range(n_pick)], axis=1)
        vs_t = jnp.concatenate([blocks[slot, g * n_pick + j, 1] for j in range(n_pick)], axis=1)
        picks = [sel_ref[b, g, j] for j in range(n_pick)]
        live = jnp.concatenate(
            [(tok_lane // SEL_BLK == hist_block(b, g, j) % blk_per_page) & (picks[j] < n_hist_blk)
             for j in range(n_pick)], axis=1)
        new_live = functools.reduce(jnp.logical_or, [pk >= n_hist_blk for pk in picks])
        k_new = rnew[:, (2 * B_KV + g) * B_DH:(2 * B_KV + g + 1) * B_DH]
        v_new = rnew[:, (3 * B_KV + g) * B_DH:(3 * B_KV + g + 1) * B_DH]
        sh = jnp.where(live, _dot(q8, ks_t), NEG)
        sn = jnp.sum(q8 * k_new, -1, keepdims=True) + jnp.where(new_live, 0.0, NEG).astype(f32)
        m = jnp.maximum(jnp.max(sh, -1, keepdims=True), sn)
        eh = jnp.where(live, jnp.exp(sh - m), 0.0)
        en = jnp.where(new_live, jnp.exp(sn - m), 0.0)
        o_sel.append((_dot_nt(eh, vs_t) + en * v_new) / (jnp.sum(eh, -1, keepdims=True) + en))
    os_ = jnp.where(row8 < B_HPG, o_sel[0], o_sel[1])
    oc = ocmp_ref[...]
    ow = owin_ref[...]
    gates = jax.nn.sigmoid(small_ref[...])
    outs = []
    for hh in range(B_HEADS):
        c0 = GB_LANE0 + hh * N_BRANCH
        outs.append(gates[:, c0:c0 + 1] * oc[hh:hh + 1] + gates[:, c0 + 1:c0 + 2] * os_[hh:hh + 1]
                    + gates[:, c0 + 2:c0 + 3] * ow[hh:hh + 1])
    o_ref[...] = jnp.concatenate(outs, axis=1)


def _block_pool_matrix(past):
    blk = jnp.arange(past) // CMP_BLK
    col = (blk % 2) * (past // SEL_BLK) + blk // 2
    return jnp.where(col[:, None] == jnp.arange(2 * (past // SEL_BLK))[None, :], 1.0 / CMP_BLK, 0.0).astype(bf16)


def _nsa_decode(page_table, qb, rows_new, win_new, small, win_buf, cache_t):
    nb, n_pages = page_table.shape
    past = n_pages * PAGE_SIZE
    assert past % SEL_BLK == 0 and cache_t.shape[1:] == (4, B_KV, B_DH, PAGE_SIZE)
    n_blk = past // SEL_BLK
    n_pick = min(SEL_TOPK, n_blk + 1)
    pool = _block_pool_matrix(past)
    one1 = lambda shape: pl.BlockSpec((None,) + shape, lambda i, pt: (i, 0, 0))
    cmp_spec = pltpu.PrefetchScalarGridSpec(
        num_scalar_prefetch=1, grid=(nb,),
        in_specs=[one1((1, B_WIDTH)), one1((1, 2 * B_KV * B_DH)), one1(win_buf.shape[1:]),
                  pl.BlockSpec(pool.shape, lambda i, pt: (0, 0)), pl.BlockSpec(memory_space=pl.ANY)],
        out_specs=[one1((B_HEADS, B_DH)), one1((B_HEADS, B_DH)), one1((8, LANES))],
        scratch_shapes=[pltpu.VMEM((2, n_pages, 2, B_KV, B_DH, PAGE_SIZE), f32),
                        pltpu.VMEM((2 * B_KV * B_DH, 2 * n_blk), f32), pltpu.SemaphoreType.DMA((2,))])
    o_cmp, o_win, idx = pl.pallas_call(
        functools.partial(_nsa_decode_cmp_kernel, n_pages=n_pages, past=past), grid_spec=cmp_spec,
        out_shape=[jax.ShapeDtypeStruct((nb, B_HEADS, B_DH), f32), jax.ShapeDtypeStruct((nb, B_HEADS, B_DH), f32),
                   jax.ShapeDtypeStruct((nb, 8, LANES), jnp.int32)],
        compiler_params=_cparams(("arbitrary",)), name="nsa_decode_cmp",
    )(page_table, qb, win_new, win_buf, pool, cache_t)
    picks = idx[:, :B_KV, :n_pick]
    one2 = lambda shape: pl.BlockSpec((None,) + shape, lambda i, pt, sel: (i, 0, 0))
    sel_spec = pltpu.PrefetchScalarGridSpec(
        num_scalar_prefetch=2, grid=(nb,),
        in_specs=[one2((1, B_WIDTH)), one2((1, 4 * B_KV * B_DH)), one2((1, LANES)),
                  one2((B_HEADS, B_DH)), one2((B_HEADS, B_DH)), pl.BlockSpec(memory_space=pl.ANY)],
        out_specs=one2((1, B_WIDTH)),
        scratch_shapes=[pltpu.VMEM((2, B_KV * n_pick, 2, B_DH, PAGE_SIZE), f32), pltpu.SemaphoreType.DMA((2,))])
    return pl.pallas_call(
        functools.partial(_nsa_decode_sel_kernel, past=past), grid_spec=sel_spec,
        out_shape=jax.ShapeDtypeStruct((nb, 1, B_WIDTH), f32),
        compiler_params=_cparams(("arbitrary",)), name="nsa_decode_sel",
    )(page_table, picks, qb, rows_new, small, o_cmp, o_win, cache_t)


SAMPLE_PAD = DELTA_CHUNK


def _even_weights(w_in):
    o = 0
    cols = {}
    for name, n in (("qkv", A_CONV_CH), ("beta", A_HEADS), ("a", A_HEADS), ("gate", A_WIDTH), ("qb", B_WIDTH),
                    ("kv", N_BRANCH * 2 * B_KV * B_DH), ("gb", B_HEADS * N_BRANCH)):
        cols[name] = w_in[:, o:o + n]
        o += n
    n_rows = 4 * B_KV * B_DH
    small = jnp.concatenate([cols["beta"], cols["a"], cols["gb"]], axis=1)
    small = jnp.pad(small, ((0, 0), (0, LANES - small.shape[1])))
    ws = [cols["qkv"], cols["gate"], cols["qb"], cols["kv"][:, :n_rows], cols["kv"][:, n_rows:], small]
    return [w.astype(bf16) for w in ws]


def _odd_weights(w_in):
    o = 0
    ws = []
    for n in (C_QK, C_QK, C_WIDTH, C_LOWRANK, C_WIDTH):
        ws.append(w_in[:, o:o + n])
        o += n
    ws[3] = jnp.pad(ws[3], ((0, 0), (0, LANES - C_LOWRANK)))
    return [w.astype(bf16) for w in ws]


def _lane_param(v, lane0):
    return jnp.zeros((1, LANES), f32).at[0, lane0:lane0 + v.shape[0]].set(v.astype(f32))


def _pad_rows(a, nb):
    return jnp.pad(a.reshape(nb, 1, -1), ((0, 0), (0, SAMPLE_PAD - 1), (0, 0)))


def kernel(x_prompt, x_sample, cache_nsa_kv, state_nsa_win, state_delta_conv, state_delta_S, state_gla_S,
           page_table, w_in_even, conv_w_delta, delta_A_log, delta_dt_bias, delta_norm_g, w_out_even,
           w_in_odd, w_gla_gate2, b_gla_gate2, gla_norm_g, w_out_odd, w_ffn_gate, w_ffn_up, w_ffn_down,
           ln_g, ln_b):
    bp, t, d = x_prompt.shape
    bs = x_sample.shape[0]
    assert x_sample.shape[1] == 1 and w_in_even.shape[0] == 1 and w_in_odd.shape[0] == 1
    n_pages = page_table.shape[1]
    past = n_pages * PAGE_SIZE
    xp = x_prompt.reshape(bp * t, d)
    xs = x_sample.reshape(bs, d)
    ln = lambda layer, j: (ln_g[layer, j].reshape(1, d), ln_b[layer, j].reshape(1, d))
    ffn_w = lambda layer: (w_ffn_gate[layer].astype(bf16), w_ffn_up[layer].astype(bf16),
                           w_ffn_down[layer].astype(bf16))

    ws = _even_weights(w_in_even[0])
    conv_w = conv_w_delta[0]
    alog = _lane_param(delta_A_log[0], A_HEADS)
    dtb = _lane_param(delta_dt_bias[0], A_HEADS)
    dng = delta_norm_g[0].reshape(1, A_DV)
    wo = w_out_even[0].astype(bf16)
    wo_parts = [wo[:A_WIDTH], wo[A_WIDTH:]]

    qkv_p, gate_p, qb_p, rows_p, win_p, small_p, rows_planes_p = _proj(xp, ws, planes_of=3)
    r3 = lambda a: a.reshape(bp, t, -1)
    o_a_p, ds_p = _delta(r3(qkv_p), r3(small_p), r3(gate_p), jnp.zeros((bp, 8, A_CONV_CH), f32), conv_w,
                         alog, dtb, dng, jnp.zeros((bp, A_HEADS, A_DK, A_DV), f32), t_valid=t)
    o_b_p = _nsa_prompt(r3(qb_p), r3(rows_p), r3(win_p), r3(small_p))
    xp = _outproj_ln([o_a_p.reshape(bp * t, -1), o_b_p.reshape(bp * t, -1)], wo_parts, xp, *ln(0, 0))
    xp = _ffn_ln(xp, *ffn_w(0), *ln(0, 1))

    qkv_s, gate_s, qb_s, rows_s, win_s, small_s = _proj(xs, ws)
    conv_s = state_delta_conv[0]
    o_a_s, ds_s = _delta(_pad_rows(qkv_s, bs), _pad_rows(small_s, bs), _pad_rows(gate_s, bs),
                         jnp.pad(conv_s, ((0, 0), (8 - (CONV_W - 1), 0), (0, 0))), conv_w, alog, dtb, dng,
                         state_delta_S[0], t_valid=1)
    win_buf = state_nsa_win[0]
    wb = win_buf.shape[1]
    r1 = lambda a: a.reshape(bs, 1, -1)
    o_b_s = _nsa_decode(page_table, r1(qb_s), r1(rows_s), r1(win_s), r1(small_s), win_buf.reshape(bs, wb, -1),
                        cache_nsa_kv[:, 0].transpose(0, 2, 3, 4, 1))
    xs = _outproj_ln([o_a_s[:, 0], o_b_s.reshape(bs, -1)], wo_parts, xs, *ln(0, 0))
    xs = _ffn_ln(xs, *ffn_w(0), *ln(0, 1))

    kvd = (B_KV, B_DH)
    nsa_rows_p = rows_planes_p.reshape((1, bp, t, 4) + kvd)
    nsa_win_p = r3(win_p)[:, -min(WINDOW, t):].reshape((1, bp, min(WINDOW, t), 2) + kvd)
    delta_conv_p = jnp.concatenate([jnp.zeros((bp, CONV_W - 1, A_CONV_CH), f32), r3(qkv_p)], axis=1)[:, -(CONV_W - 1):][None]
    nsa_rows_s = rows_s.reshape((1, bs, 1, 4) + kvd)
    win_cat = jnp.concatenate([win_buf, win_s.reshape((bs, 1, 2) + kvd)], axis=1)
    nsa_win_s = win_cat[:, -min(WINDOW, wb + 1):][None]
    delta_conv_s = jnp.concatenate([conv_s, qkv_s[:, None, :]], axis=1)[:, -(CONV_W - 1):][None]

    wq, wk, wv, wg1, wr = _odd_weights(w_in_odd[0])
    wg2 = jnp.pad(w_gla_gate2[0], ((0, LANES - C_LOWRANK), (0, 0))).astype(bf16)
    bg2 = b_gla_gate2[0].reshape(1, C_QK)
    gng = gla_norm_g[0].reshape(1, C_DV)
    wo1 = [w_out_odd[0].astype(bf16)]

    q_p, k_p, v_p, g1_p, rr_p = _proj(xp, [wq, wk, wv, wg1, wr])
    o_c_p, gs_p = _gla(r3(q_p), r3(k_p), r3(v_p), r3(rr_p), r3(g1_p), wg2, bg2, gng,
                       jnp.zeros((bp, C_HEADS, C_DK, C_DV), f32), t_valid=t)
    xp = _outproj_ln([o_c_p.reshape(bp * t, -1)], wo1, xp, *ln(1, 0))
    xp = _ffn_ln(xp, *ffn_w(1), *ln(1, 1))

    q_s, k_s, v_s, g1_s, rr_s = _proj(xs, [wq, wk, wv, wg1, wr])
    o_c_s, gs_s = _gla(_pad_rows(q_s, bs), _pad_rows(k_s, bs), _pad_rows(v_s, bs), _pad_rows(rr_s, bs),
                       _pad_rows(g1_s, bs), wg2, bg2, gng, state_gla_S[0], t_valid=1)
    xs = _outproj_ln([o_c_s[:, 0]], wo1, xs, *ln(1, 0))
    xs = _ffn_ln(xs, *ffn_w(1), *ln(1, 1))

    return (xp.reshape(bp, t, d), xs.reshape(bs, 1, d),
            nsa_rows_p, nsa_win_p, delta_conv_p, ds_p[None], gs_p[None],
            nsa_rows_s, nsa_win_s, delta_conv_s, ds_s[None], gs_s[None])
```

```python
import functools
import math

import jax
import jax.numpy as jnp
from jax import lax
from jax.experimental import pallas as pl
from jax.experimental.pallas import tpu as pltpu

f32 = jnp.float32
bf16 = jnp.bfloat16
HI = lax.Precision.HIGHEST

D_MODEL = 1024
DEPTH = 2
PAGE_SIZE = 128

A_HEADS = 4
A_DK = 128
A_DV = 128
A_QK = A_HEADS * A_DK
A_WIDTH = A_HEADS * A_DV
A_CONV_CH = 2 * A_QK + A_WIDTH
CONV_W = 4
DELTA_CHUNK = 64

B_HEADS = 8
B_KV = 2
B_HPG = B_HEADS // B_KV
B_DH = 64
B_WIDTH = B_HEADS * B_DH
N_BRANCH = 3
CMP_BLK = 32
SEL_BLK = 64
SEL_TOPK = 16
WINDOW = 512
B_SCALE = B_DH ** -0.5

C_HEADS = 4
C_DK = 128
C_DV = 256
C_QK = C_HEADS * C_DK
C_WIDTH = C_HEADS * C_DV
C_LOWRANK = 16
GLA_TAU = 16.0
GLA_CHUNK = 64
GLA_SUB = 16
GLA_PREP_CHUNKS = 2

D_FF = -(-(8 * D_MODEL) // (3 * 256)) * 256
ALPHA = (2 * DEPTH) ** 0.25
NEG = -1e30
FORCE = 1e9

LANES = 128
VMEM_LIMIT = 56 * 1024 * 1024


def _cparams(sem):
    return pltpu.CompilerParams(dimension_semantics=sem, vmem_limit_bytes=VMEM_LIMIT)


def _dot(a, b, precision=None):
    return jnp.dot(a, b, preferred_element_type=f32, precision=precision)


def _dot_nt(a, b, precision=None):
    return lax.dot_general(a, b, (((1,), (1,)), ((), ())), preferred_element_type=f32, precision=precision)


def _dot_tn(a, b, precision=None):
    return lax.dot_general(a, b, (((0,), (0,)), ((), ())), preferred_element_type=f32, precision=precision)


def _silu(x):
    return x * jax.nn.sigmoid(x)


def _layer_norm(x, g, b, eps=1e-5):
    mu = jnp.mean(x, -1, keepdims=True)
    xc = x - mu
    var = jnp.mean(xc * xc, -1, keepdims=True)
    return xc * lax.rsqrt(var + eps) * g + b


def _row_tile(m, pref):
    t = min(pref, m)
    assert m % t == 0
    return t


def _proj_kernel(x_ref, *refs, n, planes_of):
    xb = x_ref[...].astype(bf16)
    tm = x_ref.shape[0]
    for idx, (w_ref, o_ref) in enumerate(zip(refs[:n], refs[n:2 * n])):
        val = _dot(xb, w_ref[...])
        o_ref[...] = val
        if idx == planes_of:
            p_ref = refs[2 * n]
            n_planes = val.shape[1] // B_DH
            for j in range(n_planes):
                p_ref[pl.ds(j, tm, stride=n_planes), :] = val[:, j * B_DH:(j + 1) * B_DH]


def _proj(x, weights, tm=512, planes_of=None):
    m, k = x.shape
    tm = _row_tile(m, tm)
    in_specs = [pl.BlockSpec((tm, k), lambda i: (i, 0))]
    in_specs += [pl.BlockSpec(w.shape, lambda i: (0, 0)) for w in weights]
    out_specs = [pl.BlockSpec((tm, w.shape[1]), lambda i: (i, 0)) for w in weights]
    out_shape = [jax.ShapeDtypeStruct((m, w.shape[1]), f32) for w in weights]
    if planes_of is not None:
        n_planes = weights[planes_of].shape[1] // B_DH
        out_specs.append(pl.BlockSpec((tm * n_planes, B_DH), lambda i: (i, 0)))
        out_shape.append(jax.ShapeDtypeStruct((m * n_planes, B_DH), f32))
    return pl.pallas_call(
        functools.partial(_proj_kernel, n=len(weights), planes_of=planes_of),
        grid=(m // tm,), in_specs=in_specs, out_specs=out_specs, out_shape=out_shape,
        compiler_params=_cparams(("parallel",)), name="in_proj")(x, *weights)


def _outproj_ln_kernel(*refs, n_parts):
    parts = refs[:n_parts]
    ws = refs[n_parts:2 * n_parts]
    x_ref, g_ref, b_ref, o_ref = refs[2 * n_parts:]
    y = None
    for p_ref, w_ref in zip(parts, ws):
        d = _dot(p_ref[...].astype(bf16), w_ref[...])
        y = d if y is None else y + d
    o_ref[...] = _layer_norm(ALPHA * x_ref[...] + y, g_ref[...], b_ref[...])


def _outproj_ln(parts, weights, x, g, b, tm=512):
    m, d = x.shape
    tm = _row_tile(m, tm)
    n = len(parts)
    in_specs = [pl.BlockSpec((tm, p.shape[1]), lambda i: (i, 0)) for p in parts]
    in_specs += [pl.BlockSpec(w.shape, lambda i: (0, 0)) for w in weights]
    in_specs += [pl.BlockSpec((tm, d), lambda i: (i, 0)),
                 pl.BlockSpec((1, d), lambda i: (0, 0)), pl.BlockSpec((1, d), lambda i: (0, 0))]
    return pl.pallas_call(
        functools.partial(_outproj_ln_kernel, n_parts=n), grid=(m // tm,), in_specs=in_specs,
        out_specs=pl.BlockSpec((tm, d), lambda i: (i, 0)), out_shape=jax.ShapeDtypeStruct((m, d), f32),
        compiler_params=_cparams(("parallel",)), name="out_proj_ln")(*parts, *weights, x, g, b)


FFN_COLS = 256


def _ffn_ln_kernel(x_ref, wg_ref, wu_ref, wd_ref, g_ref, b_ref, o_ref):
    x = x_ref[...]
    xb = x.astype(bf16)
    acc = jnp.zeros(x.shape, f32)
    for c in range(0, D_FF, FFN_COLS):
        hg = _dot(xb, wg_ref[:, c:c + FFN_COLS])
        hu = _dot(xb, wu_ref[:, c:c + FFN_COLS])
        h = (_silu(hg) * hu).astype(bf16)
        acc = acc + _dot(h, wd_ref[c:c + FFN_COLS, :])
    o_ref[...] = _layer_norm(ALPHA * x + acc, g_ref[...], b_ref[...])


def _ffn_ln(x, wg, wu, wd, g, b, tm=512):
    m, d = x.shape
    tm = _row_tile(m, tm)
    full = lambda a: pl.BlockSpec(a.shape, lambda i: (0, 0))
    return pl.pallas_call(
        _ffn_ln_kernel, grid=(m // tm,),
        in_specs=[pl.BlockSpec((tm, d), lambda i: (i, 0)), full(wg), full(wu), full(wd), full(g), full(b)],
        out_specs=pl.BlockSpec((tm, d), lambda i: (i, 0)), out_shape=jax.ShapeDtypeStruct((m, d), f32),
        compiler_params=_cparams(("parallel",)), name="ffn_ln")(x, wg, wu, wd, g, b)


DELTA_INV_BLK = 16


def _tri_masks(c):
    r = lax.broadcasted_iota(jnp.int32, (c, c), 0)
    s = lax.broadcasted_iota(jnp.int32, (c, c), 1)
    return r, s


def _split2(x):
    hi = x.astype(bf16)
    return hi, (x - hi.astype(f32)).astype(bf16)


def _split3(x):
    hi = x.astype(bf16)
    rest = x - hi.astype(f32)
    mid = rest.astype(bf16)
    return hi, mid, (rest - mid.astype(f32)).astype(bf16)


def _dot_x3(a, b):
    ah, al = _split2(a)
    bh, bl = _split2(b)
    return _dot(ah, bh) + (_dot(ah, bl) + _dot(al, bh))


def _dot_ones(a_ones, b):
    a16 = a_ones.astype(bf16)
    b1, b2, b3 = _split3(b)
    return _dot(a16, b1) + (_dot(a16, b2) + _dot(a16, b3))


def _unit_lower_inverses(mats, r, s):
    c = mats[0].shape[0]
    eye = (r == s).astype(f32)
    same_blk = (r // DELTA_INV_BLK) == (s // DELTA_INV_BLK)
    diag = [jnp.where(same_blk, a, 0.0) for a in mats]
    low = [a - d for a, d in zip(mats, diag)]
    pw = [-d for d in diag]
    p = [eye + x for x in pw]
    k = 2
    while k < DELTA_INV_BLK:
        pw = [_dot_x3(x, x) for x in pw]
        p = [pi + _dot_x3(pi, x) for pi, x in zip(p, pw)]
        k *= 2
    pw = [_dot_x3(pi, lo) for pi, lo in zip(p, low)]
    q = [eye - x for x in pw]
    k = 2
    while k < c // DELTA_INV_BLK:
        pw = [_dot_x3(x, x) for x in pw]
        q = [qi + _dot_x3(qi, x) for qi, x in zip(q, pw)]
        k *= 2
    return [_dot_x3(qi, pi) for qi, pi in zip(q, p)]


DELTA_PREP_CHUNKS = 2


def _delta_kernel(qkv_ref, small_ref, gate_ref, cbuf_ref, cw_ref, alog_ref, dtb_ref, dng_ref, s0_ref,
                  o_ref, s_ref, xbuf, ybuf, tail, u_ref, w_ref, qg_ref, kd_ref, qk_ref, gl_ref,
                  *, tt, t_valid, t_total):
    t = pl.program_id(1)
    c = DELTA_CHUNK
    n_chunks = tt // c
    cpi = math.gcd(DELTA_PREP_CHUNKS, n_chunks)

    @pl.when(t == 0)
    def _():
        s_ref[...] = s0_ref[...]
        tail[...] = cbuf_ref[...]

    xbuf[0:8, :] = tail[...]
    xbuf[8:8 + tt, :] = qkv_ref[...]
    tail[...] = xbuf[tt:tt + 8, :]

    def row0(chunk):
        return chunk * c if isinstance(chunk, int) else pl.multiple_of(chunk * c, c)

    def conv_group(gi):
        for cc in range(cpi):
            r0 = row0(gi * cpi + cc)
            for col in range(0, A_CONV_CH, A_QK):
                cols = slice(col, col + A_QK)
                x = xbuf[pl.ds(r0, c + 8), cols]
                first = 8 - (CONV_W - 1)
                y = x[first:first + c] * cw_ref[0:1, cols]
                for j in range(1, CONV_W):
                    y = y + x[first + j:first + j + c] * cw_ref[j:j + 1, cols]
                ybuf[pl.ds(r0, c), cols] = _silu(y)

    r, s = _tri_masks(c)
    causal = r >= s
    strict = r > s
    lower_ones = causal.astype(f32)
    upper4 =jnp.concatenate([(r <= s).astype(f32)] * A_HEADS, axis=1)
    all_ones = jnp.ones((c, c), f32)
    alog = alog_ref[...]
    dtb = dtb_ref[...]
    dng = dng_ref[...]

    def prep_group(gi):
        items = []
        for cc in range(cpi):
            r0 = row0(gi * cpi + cc)
            sm = small_ref[pl.ds(r0, c), :]
            beta_all = jax.nn.sigmoid(sm)
            g_all = -jnp.exp(alog) * jax.nn.softplus(sm + dtb)
            if t_valid < t_total:
                live = t * tt + r0 + lax.broadcasted_iota(jnp.int32, (c, 1), 0) < t_valid
                beta_all = jnp.where(live, beta_all, 0.0)
                g_all = jnp.where(live, g_all, 0.0)
            gam_c_all = _dot_ones(lower_ones, g_all)
            g_rows = jnp.concatenate(
                [jnp.broadcast_to(g_all[:, A_HEADS + h:A_HEADS + h + 1], (c, c)) for h in range(A_HEADS)], axis=1)
            gam_r_all = _dot_ones(all_ones, g_rows * upper4)
            for h in range(A_HEADS):
                qh = ybuf[pl.ds(r0, c), h * A_DK:(h + 1) * A_DK]
                kh = ybuf[pl.ds(r0, c), A_QK + h * A_DK:A_QK + (h + 1) * A_DK]
                vh = ybuf[pl.ds(r0, c), 2 * A_QK + h * A_DV:2 * A_QK + (h + 1) * A_DV]
                qh = qh * lax.rsqrt(jnp.sum(qh * qh, -1, keepdims=True) + 1e-6) * (A_DK ** -0.5)
                kh = kh * lax.rsqrt(jnp.sum(kh * kh, -1, keepdims=True) + 1e-6)
                if t_valid < t_total:
                    qh = jnp.where(live, qh, 0.0)
                    kh = jnp.where(live, kh, 0.0)
                    vh = jnp.where(live, vh, 0.0)
                beta = beta_all[:, h:h + 1]
                gam_c = jnp.broadcast_to(gam_c_all[:, A_HEADS + h:A_HEADS + h + 1], (c, A_DK))
                diff = gam_c[:, :c] - gam_r_all[:, h * c:(h + 1) * c]
                ldec = jnp.where(causal, jnp.exp(jnp.where(causal, diff, 0.0)), 0.0)
                items.append(dict(r0=r0, h=h, q=qh, k=kh, v=vh, beta=beta, gam=gam_c, ldec=ldec, kb=kh * beta))
        amats = [jnp.where(strict, _dot_nt(it["kb"], it["k"]) * it["ldec"], 0.0) for it in items]
        tms = _unit_lower_inverses(amats, r, s)
        for it, tm in zip(items, tms):
            r0, h = it["r0"], it["h"]
            cols = slice(h * A_DK, (h + 1) * A_DK)
            eg = jnp.exp(it["gam"])
            g_last = it["gam"][c - 1:c, :]
            u_ref[pl.ds(r0, c), cols] = _dot(tm, it["v"] * it["beta"])
            w_ref[pl.ds(r0, c), cols] = _dot(tm, it["kb"] * eg)
            qg_ref[pl.ds(r0, c), cols] = it["q"] * eg
            kd_ref[pl.ds(r0, c), cols] = it["k"] * jnp.exp(g_last - it["gam"])
            qk_ref[pl.ds(r0, c), h * c:(h + 1) * c] = _dot_nt(it["q"], it["k"]) * it["ldec"]
            gl_ref[pl.ds(r0, 1), cols] = jnp.exp(g_last)

    def scan_group(gi):
        for cc in range(cpi):
            r0 = row0(gi * cpi + cc)
            for h in range(A_HEADS):
                cols = slice(h * A_DK, (h + 1) * A_DK)
                st = s_ref[h]
                v_new = u_ref[pl.ds(r0, c), cols] - _dot(w_ref[pl.ds(r0, c), cols], st)
                o = _dot(qg_ref[pl.ds(r0, c), cols], st) + _dot(qk_ref[pl.ds(r0, c), h * c:(h + 1) * c], v_new)
                s_ref[h] = st * gl_ref[pl.ds(r0, 1), cols] + _dot_tn(kd_ref[pl.ds(r0, c), cols], v_new)
                o = o * lax.rsqrt(jnp.mean(o * o, -1, keepdims=True) + 1e-6) * dng
                gt = gate_ref[pl.ds(r0, c), h * A_DV:(h + 1) * A_DV]
                o_ref[pl.ds(r0, c), h * A_DV:(h + 1) * A_DV] = o * _silu(gt)

    n_groups = n_chunks // cpi
    conv_group(0)
    if n_groups > 1:
        prep_group(0)
        conv_group(1)

        def steady(gi, carry):
            scan_group(gi - 1)
            prep_group(gi)
            conv_group(gi + 1)
            return carry

        lax.fori_loop(1, n_groups - 1, steady, 0)
        scan_group(n_groups - 2)
    prep_group(n_groups - 1)
    scan_group(n_groups - 1)


def _delta(qkv, small, gate, conv_buf8, conv_w, alog, dtb, dng, s0, *, t_valid, tt=1024):
    b, t_total, _ = qkv.shape
    tt = _row_tile(t_total, tt)
    nt = t_total // tt
    row = lambda w: pl.BlockSpec((None, tt, w), lambda i, j: (i, j, 0))
    full2 = lambda a: pl.BlockSpec(a.shape, lambda i, j: (0, 0))
    st_spec = pl.BlockSpec((None, A_HEADS, A_DK, A_DV), lambda i, j: (i, 0, 0, 0))
    kern = functools.partial(_delta_kernel, tt=tt, t_valid=t_valid, t_total=t_total)
    return pl.pallas_call(
        kern, grid=(b, nt),
        in_specs=[row(A_CONV_CH), row(LANES), row(A_WIDTH),
                  pl.BlockSpec((None, 8, A_CONV_CH), lambda i, j: (i, 0, 0)),
                  full2(conv_w), full2(alog), full2(dtb), full2(dng), st_spec],
        out_specs=[row(A_WIDTH), st_spec],
        out_shape=[jax.ShapeDtypeStruct((b, t_total, A_WIDTH), f32),
                   jax.ShapeDtypeStruct((b, A_HEADS, A_DK, A_DV), f32)],
        scratch_shapes=[pltpu.VMEM((tt + 8, A_CONV_CH), f32), pltpu.VMEM((tt, A_CONV_CH), f32),
                        pltpu.VMEM((8, A_CONV_CH), f32)]
        + [pltpu.VMEM((tt, A_QK), f32)] * 4
        + [pltpu.VMEM((tt, A_HEADS * DELTA_CHUNK), f32), pltpu.VMEM((tt, A_QK), f32)],
        compiler_params=_cparams(("parallel", "arbitrary")), name="delta_mixer",
    )(qkv, small, gate, conv_buf8, conv_w, alog, dtb, dng, s0)


def _gla_kernel(q_ref, k_ref, v_ref, r_ref, g1_ref, wg2_ref, bg2_ref, gng_ref, s0_ref,
                o_ref, s_ref, st_ref, la_ref, qe_ref, gl_ref, ds_ref, *, tt, t_valid, t_total):
    t = pl.program_id(1)
    nt = pl.num_programs(1)
    c = GLA_CHUNK
    heads = range(C_HEADS)
    kcols = [slice(h * C_DK, (h + 1) * C_DK) for h in heads]
    vcols = [slice(h * C_DV, (h + 1) * C_DV) for h in heads]

    @pl.when(t == 0)
    def _():
        for h in range(C_HEADS):
            st_ref[h] = s0_ref[h].T

    la = jax.nn.log_sigmoid(_dot(g1_ref[...].astype(bf16), wg2_ref[...]) + bg2_ref[...]) / GLA_TAU
    if t_valid < t_total:
        rows = t * tt + lax.broadcasted_iota(jnp.int32, (tt, 1), 0)
        la = jnp.where(rows < t_valid, la, 0.0)
    la_ref[...] = la

    r, s = _tri_masks(c)
    lower_ones = (r >= s).astype(f32)
    rs = lax.broadcasted_iota(jnp.int32, (GLA_SUB, c), 0)
    ss = lax.broadcasted_iota(jnp.int32, (GLA_SUB, c), 1)
    krow = lax.broadcasted_iota(jnp.int32, (c, 1), 0)
    gng = gng_ref[...]

    n_chunks = tt // c
    cpi = math.gcd(GLA_PREP_CHUNKS, n_chunks)

    def row0(chunk):
        return chunk * c if isinstance(chunk, int) else pl.multiple_of(chunk * c, c)

    def prep_group(gi):
        items = []
        for cc in range(cpi):
            chunk = gi * cpi + cc
            r0 = row0(chunk)
            rows = pl.ds(r0, c)
            bc_all = _dot_ones(lower_ones, la_ref[rows, :])
            for h in heads:
                items.append(dict(chunk=chunk, r0=r0, rows=rows, h=h, q=q_ref[rows, kcols[h]] * (C_DK ** -0.5),
                                  k=k_ref[rows, kcols[h]], v=v_ref[rows, vcols[h]], bc=bc_all[:, kcols[h]]))
        for it in items:
            qe_ref[it["rows"], kcols[it["h"]]] = it["q"] * jnp.exp(it["bc"])
        for a in range(c // GLA_SUB):
            lo, hi = a * GLA_SUB, (a + 1) * GLA_SUB
            seen = krow < hi
            att = []
            for it in items:
                bref = it["bc"][lo - 1:lo, :] if a > 0 else jnp.zeros((1, C_DK), f32)
                qa = it["q"][lo:hi, :] * jnp.exp(it["bc"][lo:hi, :] - bref)
                ka = jnp.where(seen, it["k"] * jnp.exp(jnp.where(seen, bref - it["bc"], 0.0)), 0.0)
                att.append(jnp.where(rs + lo >= ss, _dot_nt(qa, ka), 0.0))
            for it, at in zip(items, att):
                o_ref[pl.ds(it["r0"] + lo, GLA_SUB), vcols[it["h"]]] = _dot(at, it["v"])
        for it in items:
            b_last = it["bc"][c - 1:c, :]
            gl_ref[pl.ds(it["r0"], 1), kcols[it["h"]]] = jnp.exp(b_last)
            ds_ref[it["chunk"], it["h"]] = _dot_tn(it["v"], it["k"] * jnp.exp(b_last - it["bc"]))

    def scan_group(gi):
        for cc in range(cpi):
            chunk = gi * cpi + cc
            r0 = row0(chunk)
            rows = pl.ds(r0, c)
            for h in heads:
                st = st_ref[h]
                o = _dot_nt(qe_ref[rows, kcols[h]], st) + o_ref[rows, vcols[h]]
                st_ref[h] = st * gl_ref[pl.ds(r0, 1), kcols[h]] + ds_ref[chunk, h]
                o = o * lax.rsqrt(jnp.mean(o * o, -1, keepdims=True) + 1e-6) * gng
                o_ref[rows, vcols[h]] = o * _silu(r_ref[rows, vcols[h]])

    n_groups = n_chunks // cpi
    prep_group(0)

    def steady(gi, carry):
        scan_group(gi - 1)
        prep_group(gi)
        return carry

    lax.fori_loop(1, n_groups, steady, 0)
    scan_group(n_groups - 1)

    @pl.when(t == nt - 1)
    def _():
        for h in range(C_HEADS):
            s_ref[h] = st_ref[h].T


def _gla(q, k, v, r, g1, wg2, bg2, gng, s0, *, t_valid, tt=512):
    b, t_total, _ = q.shape
    tt = _row_tile(t_total, tt)
    nt = t_total // tt
    row = lambda w: pl.BlockSpec((None, tt, w), lambda i, j: (i, j, 0))
    full2 = lambda a: pl.BlockSpec(a.shape, lambda i, j: (0, 0))
    st_spec = pl.BlockSpec((None, C_HEADS, C_DK, C_DV), lambda i, j: (i, 0, 0, 0))
    kern = functools.partial(_gla_kernel, tt=tt, t_valid=t_valid, t_total=t_total)
    return pl.pallas_call(
        kern, grid=(b, nt),
        in_specs=[row(C_QK), row(C_QK), row(C_WIDTH), row(C_WIDTH), row(LANES),
                  full2(wg2), full2(bg2), full2(gng), st_spec],
        out_specs=[row(C_WIDTH), st_spec],
        out_shape=[jax.ShapeDtypeStruct((b, t_total, C_WIDTH), f32),
                   jax.ShapeDtypeStruct((b, C_HEADS, C_DK, C_DV), f32)],
        scratch_shapes=[pltpu.VMEM((C_HEADS, C_DV, C_DK), f32), pltpu.VMEM((tt, C_QK), f32),
                        pltpu.VMEM((tt, C_QK), f32), pltpu.VMEM((tt, C_QK), f32),
                        pltpu.VMEM((tt // GLA_CHUNK, C_HEADS, C_DV, C_DK), f32)],
        compiler_params=_cparams(("parallel", "arbitrary")), name="gla_mixer",
    )(q, k, v, r, g1, wg2, bg2, gng, s0)


GB_LANE0 = 2 * A_HEADS


def _masked_softmax(s, mask):
    sm = jnp.where(mask, s, NEG)
    e = jnp.exp(sm - jnp.max(sm, -1, keepdims=True))
    p = e / jnp.sum(e, -1, keepdims=True)
    return jnp.where(mask, p, 0.0)


def _topk_rank(score, n_cand):
    lane = lax.broadcasted_iota(jnp.int32, score.shape, score.ndim - 1)
    rank = jnp.zeros(score.shape, jnp.int32)
    for i in range(n_cand):
        si = score[..., i:i + 1]
        ahead = (si > score) | ((si == score) & (i < lane))
        rank = rank + ahead.astype(jnp.int32)
    return rank


def _block_scores(imp, qpos, n_blk):
    blk = lax.broadcasted_iota(jnp.int32, imp.shape, imp.ndim - 1)
    cur = qpos // SEL_BLK
    valid = blk * SEL_BLK <= qpos
    forced = (blk == 0) | (blk == cur) | (blk == cur - 1)
    return jnp.where(forced, FORCE, jnp.where(valid, imp, NEG))


def _split_cmp_index(col, half):
    return jnp.where(col < half, 2 * col, 2 * (col - half) + 1)


NSA_TQ = 128
NSA_TK = 512


def _topk_mask_rows(score, k):
    n = score.shape[0]
    row = lax.broadcasted_iota(jnp.int32, score.shape, 0)
    rank = jnp.zeros(score.shape, jnp.int32)
    for i in range(n):
        si = score[i:i + 1, :]
        ahead = (si > score) | ((si == score) & (i < row))
        rank = rank + ahead.astype(jnp.int32)
    return (rank < k).astype(f32)


def _nsa_prompt_kernel(q_ref, rows_ref, win_ref, small_ref, o_ref, cb_ref, sel_ref, *, t_total):
    i = pl.program_id(1)
    tq = NSA_TQ
    tk = min(NSA_TK, t_total)
    tw = min(WINDOW + tq, t_total)
    n_sel = t_total // SEL_BLK
    half = n_sel
    hq = B_HPG * tq
    blk_per_tile = tk // SEL_BLK

    @pl.when(i == 0)
    def _():
        x = rows_ref[:, 0:2 * B_KV * B_DH].reshape(n_sel, SEL_BLK, 2 * B_KV * B_DH)
        cb_ref[0:half, :] = jnp.sum(x[:, :CMP_BLK, :], axis=1) * (1.0 / CMP_BLK)
        cb_ref[half:2 * half, :] = jnp.sum(x[:, CMP_BLK:, :], axis=1) * (1.0 / CMP_BLK)

    qpos = i * tq + lax.broadcasted_iota(jnp.int32, (1, tq), 1)
    qpos4 = jnp.concatenate([qpos] * B_HPG, axis=1)
    gates_t = jax.nn.sigmoid(small_ref[...].T)
    krow = lax.broadcasted_iota(jnp.int32, (tk, 1), 0)
    wrow = lax.broadcasted_iota(jnp.int32, (tw, 1), 0)
    n_causal = (i * tq + tq - 1) // tk + 1
    groups = range(B_KV)
    q4 = [jnp.concatenate([q_ref[:, (g * B_HPG + h) * B_DH:(g * B_HPG + h + 1) * B_DH] for h in range(B_HPG)],
                          axis=0) * B_SCALE for g in groups]

    o_cmp = []
    for g in groups:
        kc = cb_ref[:, g * B_DH:(g + 1) * B_DH]
        vc = cb_ref[:, (B_KV + g) * B_DH:(B_KV + g + 1) * B_DH]
        st = _dot_nt(kc, q4[g])
        cidx = _split_cmp_index(lax.broadcasted_iota(jnp.int32, (2 * half, 1), 0), half)
        cmask = (cidx + 1) * CMP_BLK <= qpos4 + 1
        sm = jnp.where(cmask, st, NEG)
        e = jnp.exp(sm - jnp.max(sm, axis=0, keepdims=True))
        p = jnp.where(cmask, e / jnp.sum(e, axis=0, keepdims=True), 0.0)
        o_cmp.append(_dot_tn(vc, p))
        imp = p[:, 0:tq]
        for h in range(1, B_HPG):
            imp = imp + p[:, h * tq:(h + 1) * tq]
        imp = imp[:half] + imp[half:]
        blk = lax.broadcasted_iota(jnp.int32, (n_sel, 1), 0)
        cur = qpos // SEL_BLK
        forced = (blk == 0) | (blk == cur) | (blk == cur - 1)
        score = jnp.where(forced, FORCE, jnp.where(blk * SEL_BLK <= qpos, imp, NEG))
        sel_ref[g] = _topk_mask_rows(score, min(SEL_TOPK, n_sel))

    def attend(qg, k, v, live, carry):
        m, l, acc = carry
        bias = jnp.where(live, 0.0, NEG)
        sc = _dot_nt(k, qg) + jnp.concatenate([bias] * B_HPG, axis=1)
        m_new = jnp.maximum(m, jnp.max(sc, axis=0, keepdims=True))
        pe = jnp.exp(sc - m_new)
        alpha = jnp.exp(m - m_new)
        l = alpha * l + jnp.sum(pe, axis=0, keepdims=True)
        acc = alpha * acc + _dot_tn(v, pe)
        return m_new, l, acc

    init = (jnp.full((1, hq), NEG, f32), jnp.zeros((1, hq), f32), jnp.zeros((B_DH, hq), f32))

    def sel_step(j, carries):
        k0 = pl.multiple_of(j * tk, tk)
        causal = k0 + krow <= qpos
        out = []
        for g in groups:
            flags = sel_ref[g, pl.ds(pl.multiple_of(j * blk_per_tile, blk_per_tile), blk_per_tile), :]
            flags = jnp.concatenate(
                [jnp.broadcast_to(flags[u:u + 1, :], (SEL_BLK, tq)) for u in range(blk_per_tile)], axis=0)
            kl, vl = (2 * B_KV + g) * B_DH, (3 * B_KV + g) * B_DH
            out.append(attend(q4[g], rows_ref[pl.ds(k0, tk), kl:kl + B_DH], rows_ref[pl.ds(k0, tk), vl:vl + B_DH],
                              (flags > 0.5) & causal, carries[g]))
        return tuple(out)

    sel_out = lax.fori_loop(0, n_causal, sel_step, (init,) * B_KV)
    w0 = pl.multiple_of(jnp.maximum((i + 1) * tq - tw, 0), tq)
    d = qpos - (w0 + wrow)
    in_window = (d >= 0) & (d < WINDOW)
    out_rows = []
    for g in groups:
        kl, vl = g * B_DH, (B_KV + g) * B_DH
        _, l_w, acc_w = attend(q4[g], win_ref[pl.ds(w0, tw), kl:kl + B_DH], win_ref[pl.ds(w0, tw), vl:vl + B_DH],
                               in_window, init)
        o_win = acc_w / l_w
        o_sel = sel_out[g][2] / sel_out[g][1]
        for h in range(B_HPG):
            c0 = GB_LANE0 + (g * B_HPG + h) * N_BRANCH
            ls = slice(h * tq, (h + 1) * tq)
            out_rows.append(gates_t[c0:c0 + 1, :] * o_cmp[g][:, ls] + gates_t[c0 + 1:c0 + 2, :] * o_sel[:, ls]
                            + gates_t[c0 + 2:c0 + 3, :] * o_win[:, ls])
    o_ref[...] = jnp.concatenate(out_rows, axis=0).T


def _nsa_prompt(qb, rows, win, small):
    b, t_total, _ = qb.shape
    tk = min(NSA_TK, t_total)
    assert t_total % tk == 0 and tk % NSA_TQ == 0 and tk % SEL_BLK == 0 and WINDOW % NSA_TQ == 0
    nq = t_total // NSA_TQ
    n_sel = t_total // SEL_BLK
    tile = lambda w: pl.BlockSpec((None, NSA_TQ, w), lambda bi, i: (bi, i, 0))
    seq = lambda w: pl.BlockSpec((None, t_total, w), lambda bi, i: (bi, 0, 0))
    return pl.pallas_call(
        functools.partial(_nsa_prompt_kernel, t_total=t_total), grid=(b, nq),
        in_specs=[tile(B_WIDTH), seq(4 * B_KV * B_DH), seq(2 * B_KV * B_DH), tile(LANES)],
        out_specs=tile(B_WIDTH), out_shape=jax.ShapeDtypeStruct((b, t_total, B_WIDTH), f32),
        scratch_shapes=[pltpu.VMEM((2 * n_sel, 2 * B_KV * B_DH), f32), pltpu.VMEM((B_KV, n_sel, NSA_TQ), f32)],
        compiler_params=_cparams(("parallel", "arbitrary")), name="nsa_prompt",
    )(qb, rows, win, small)


DECODE_POOL_PAGES = 8


def _nsa_decode_cmp_kernel(pt_ref, q_ref, wnew_ref, wbuf_ref, pool_ref, cache_ref, ocmp_ref, owin_ref, idx_ref,
                           pages, acc_ref, sem, *, n_pages, past):
    b = pl.program_id(0)
    nb = pl.num_programs(0)
    slot = b % 2
    n_hist_blk = past // SEL_BLK
    n_sel = -(-(past + 1) // SEL_BLK)
    half = n_hist_blk
    lanes_sel = 2 * half
    wb = wbuf_ref.shape[0]
    n_cmp_rows = 2 * B_KV * B_DH

    def page_copy(bb, p, sl):
        return pltpu.make_async_copy(cache_ref.at[pt_ref[bb, p], pl.ds(0, 2)], pages.at[sl, p], sem.at[sl])

    def fetch(bb, sl):
        for p in range(n_pages):
            page_copy(bb, p, sl).start()

    @pl.when(b == 0)
    def _():
        fetch(0, 0)

    @pl.when(b + 1 < nb)
    def _():
        fetch(b + 1, 1 - slot)

    for p in range(n_pages):
        page_copy(b, p, slot).wait()

    acc_ref[...] = jnp.zeros(acc_ref.shape, f32)
    ppc = math.gcd(n_pages, DECODE_POOL_PAGES)

    def pool_body(c, carry):
        x = jnp.concatenate([pages[slot, c * ppc + i].reshape(n_cmp_rows, PAGE_SIZE) for i in range(ppc)], axis=1)
        hi, lo = _split2(x)
        w = pool_ref[pl.ds(pl.multiple_of(c * ppc * PAGE_SIZE, ppc * PAGE_SIZE), ppc * PAGE_SIZE), :]
        acc_ref[...] += _dot(hi, w) + _dot(lo, w)
        return carry

    lax.fori_loop(0, n_pages // ppc, pool_body, 0)

    qpos = past
    row8 = lax.broadcasted_iota(jnp.int32, (B_HEADS, 1), 0)
    q8 = jnp.concatenate([q_ref[:, h * B_DH:(h + 1) * B_DH] for h in range(B_HEADS)], axis=0) * B_SCALE
    wnew = wnew_ref[...]
    lane = lax.broadcasted_iota(jnp.int32, (1, lanes_sel), 1)
    o_cmp, o_win, idx_rows = [], [], []
    for g in range(B_KV):
        in_g = (row8 // B_HPG) == g
        kc_t = acc_ref[g * B_DH:(g + 1) * B_DH, :]
        vc_t = acc_ref[(B_KV + g) * B_DH:(B_KV + g + 1) * B_DH, :]
        sc = _dot(q8, kc_t)
        cidx = _split_cmp_index(lane, half)
        p = _masked_softmax(sc, (cidx + 1) * CMP_BLK <= qpos + 1)
        o_cmp.append(_dot_nt(p, vc_t))
        imp = jnp.sum(jnp.where(in_g, p, 0.0), axis=0, keepdims=True)
        imp = imp[:, :half] + imp[:, half:]
        imp = jnp.concatenate([imp, jnp.zeros((1, lanes_sel - half), f32)], axis=1)
        rank = _topk_rank(_block_scores(imp, qpos, n_sel), n_sel)
        lane_f = lane.astype(f32)
        out_lane = lax.broadcasted_iota(jnp.int32, (1, LANES), 1)
        picked = jnp.zeros((1, LANES), f32)
        for r in range(min(SEL_TOPK, n_sel)):
            blk_r = jnp.sum(jnp.where((rank == r) & (lane < n_sel), lane_f, 0.0), axis=-1, keepdims=True)
            picked = picked + jnp.where(out_lane == r, blk_r, 0.0)
        idx_rows.append(picked)
        kw = wbuf_ref[:, g * B_DH:(g + 1) * B_DH]
        vw = wbuf_ref[:, (B_KV + g) * B_DH:(B_KV + g + 1) * B_DH]
        kw_new = wnew[:, g * B_DH:(g + 1) * B_DH]
        vw_new = wnew[:, (B_KV + g) * B_DH:(B_KV + g + 1) * B_DH]
        kpos = qpos - wb + lax.broadcasted_iota(jnp.int32, (1, wb), 1)
        d = qpos - kpos
        wmask = (d >= 0) & (d < WINDOW) & (kpos >= 0)
        sw = jnp.where(wmask, _dot_nt(q8, kw), NEG)
        swn = jnp.sum(q8 * kw_new, -1, keepdims=True)
        m = jnp.maximum(jnp.max(sw, -1, keepdims=True), swn)
        ew = jnp.where(wmask, jnp.exp(sw - m), 0.0)
        ewn = jnp.exp(swn - m)
        o_win.append((_dot(ew, vw) + ewn * vw_new) / (jnp.sum(ew, -1, keepdims=True) + ewn))
    lower = row8 < B_HPG
    ocmp_ref[...] = jnp.where(lower, o_cmp[0], o_cmp[1])
    owin_ref[...] = jnp.where(lower, o_win[0], o_win[1])
    idx_ref[...] = jnp.concatenate(idx_rows + [jnp.zeros((8 - B_KV, LANES), f32)], axis=0).astype(jnp.int32)


def _nsa_decode_sel_kernel(pt_ref, sel_ref, q_ref, rnew_ref, small_ref, ocmp_ref, owin_ref, cache_ref,
                           o_ref, blocks, sem, *, past):
    b = pl.program_id(0)
    nb = pl.num_programs(0)
    slot = b % 2
    n_hist_blk = past // SEL_BLK
    n_pick = sel_ref.shape[2]
    blk_per_page = PAGE_SIZE // SEL_BLK
    tok_lane = lax.broadcasted_iota(jnp.int32, (1, PAGE_SIZE), 1)

    def hist_block(bb, g, j):
        return jnp.minimum(sel_ref[bb, g, j], n_hist_blk - 1)

    def block_copy(bb, g, j, sl):
        page = pt_ref[bb, hist_block(bb, g, j) // blk_per_page]
        return pltpu.make_async_copy(cache_ref.at[page, pl.ds(2, 2), g], blocks.at[sl, g * n_pick + j], sem.at[sl])

    def fetch(bb, sl):
        for g in range(B_KV):
            for j in range(n_pick):
                block_copy(bb, g, j, sl).start()

    @pl.when(b == 0)
    def _():
        fetch(0, 0)

    @pl.when(b + 1 < nb)
    def _():
        fetch(b + 1, 1 - slot)

    for g in range(B_KV):
        for j in range(n_pick):
            block_copy(b, g, j, slot).wait()

    row8 = lax.broadcasted_iota(jnp.int32, (B_HEADS, 1), 0)
    q8 = jnp.concatenate([q_ref[:, h * B_DH:(h + 1) * B_DH] for h in range(B_HEADS)], axis=0) * B_SCALE
    rnew = rnew_ref[...]
    o_sel = []
    for g in range(B_KV):
        ks_t = jnp.concatenate([blocks[slot, g * n_pick + j, 0] for j in range(n_pick)], axis=1)
        vs_t = jnp.concatenate([blocks[slot, g * n_pick + j, 1] for j in range(n_pick)], axis=1)
        picks = [sel_ref[b, g, j] for j in range(n_pick)]
        live = jnp.concatenate(
            [(tok_lane // SEL_BLK == hist_block(b, g, j) % blk_per_page) & (picks[j] < n_hist_blk)
             for j in range(n_pick)], axis=1)
        new_live = functools.reduce(jnp.logical_or, [pk >= n_hist_blk for pk in picks])
        k_new = rnew[:, (2 * B_KV + g) * B_DH:(2 * B_KV + g + 1) * B_DH]
        v_new = rnew[:, (3 * B_KV + g) * B_DH:(3 * B_KV + g + 1) * B_DH]
        sh = jnp.where(live, _dot(q8, ks_t), NEG)
        sn = jnp.sum(q8 * k_new, -1, keepdims=True) + jnp.where(new_live, 0.0, NEG).astype(f32)
        m = jnp.maximum(jnp.max(sh, -1, keepdims=True), sn)
        eh = jnp.where(live, jnp.exp(sh - m), 0.0)
        en = jnp.where(new_live, jnp.exp(sn - m), 0.0)
        o_sel.append((_dot_nt(eh, vs_t) + en * v_new) / (jnp.sum(eh, -1, keepdims=True) + en))
    os_ = jnp.where(row8 < B_HPG, o_sel[0], o_sel[1])
    oc = ocmp_ref[...]
    ow = owin_ref[...]
    gates = jax.nn.sigmoid(small_ref[...])
    outs = []
    for hh in range(B_HEADS):
        c0 = GB_LANE0 + hh * N_BRANCH
        outs.append(gates[:, c0:c0 + 1] * oc[hh:hh + 1] + gates[:, c0 + 1:c0 + 2] * os_[hh:hh + 1]
                    + gates[:, c0 + 2:c0 + 3] * ow[hh:hh + 1])
    o_ref[...] = jnp.concatenate(outs, axis=1)


def _block_pool_matrix(past):
    blk = jnp.arange(past) // CMP_BLK
    col = (blk % 2) * (past // SEL_BLK) + blk // 2
    return jnp.where(col[:, None] == jnp.arange(2 * (past // SEL_BLK))[None, :], 1.0 / CMP_BLK, 0.0).astype(bf16)


def _nsa_decode(page_table, qb, rows_new, win_new, small, win_buf, cache_t):
    nb, n_pages = page_table.shape
    past = n_pages * PAGE_SIZE
    assert past % SEL_BLK == 0 and cache_t.shape[1:] == (4, B_KV, B_DH, PAGE_SIZE)
    n_blk = past // SEL_BLK
    n_pick = min(SEL_TOPK, n_blk + 1)
    pool = _block_pool_matrix(past)
    one1 = lambda shape: pl.BlockSpec((None,) + shape, lambda i, pt: (i, 0, 0))
    cmp_spec = pltpu.PrefetchScalarGridSpec(
        num_scalar_prefetch=1, grid=(nb,),
        in_specs=[one1((1, B_WIDTH)), one1((1, 2 * B_KV * B_DH)), one1(win_buf.shape[1:]),
                  pl.BlockSpec(pool.shape, lambda i, pt: (0, 0)), pl.BlockSpec(memory_space=pl.ANY)],
        out_specs=[one1((B_HEADS, B_DH)), one1((B_HEADS, B_DH)), one1((8, LANES))],
        scratch_shapes=[pltpu.VMEM((2, n_pages, 2, B_KV, B_DH, PAGE_SIZE), f32),
                        pltpu.VMEM((2 * B_KV * B_DH, 2 * n_blk), f32), pltpu.SemaphoreType.DMA((2,))])
    o_cmp, o_win, idx = pl.pallas_call(
        functools.partial(_nsa_decode_cmp_kernel, n_pages=n_pages, past=past), grid_spec=cmp_spec,
        out_shape=[jax.ShapeDtypeStruct((nb, B_HEADS, B_DH), f32), jax.ShapeDtypeStruct((nb, B_HEADS, B_DH), f32),
                   jax.ShapeDtypeStruct((nb, 8, LANES), jnp.int32)],
        compiler_params=_cparams(("arbitrary",)), name="nsa_decode_cmp",
    )(page_table, qb, win_new, win_buf, pool, cache_t)
    picks = idx[:, :B_KV, :n_pick]
    one2 = lambda shape: pl.BlockSpec((None,) + shape, lambda i, pt, sel: (i, 0, 0))
    sel_spec = pltpu.PrefetchScalarGridSpec(
        num_scalar_prefetch=2, grid=(nb,),
        in_specs=[one2((1, B_WIDTH)), one2((1, 4 * B_KV * B_DH)), one2((1, LANES)),
                  one2((B_HEADS, B_DH)), one2((B_HEADS, B_DH)), pl.BlockSpec(memory_space=pl.ANY)],
        out_specs=one2((1, B_WIDTH)),
        scratch_shapes=[pltpu.VMEM((2, B_KV * n_pick, 2, B_DH, PAGE_SIZE), f32), pltpu.SemaphoreType.DMA((2,))])
    return pl.pallas_call(
        functools.partial(_nsa_decode_sel_kernel, past=past), grid_spec=sel_spec,
        out_shape=jax.ShapeDtypeStruct((nb, 1, B_WIDTH), f32),
        compiler_params=_cparams(("arbitrary",)), name="nsa_decode_sel",
    )(page_table, picks, qb, rows_new, small, o_cmp, o_win, cache_t)


SAMPLE_PAD = DELTA_CHUNK


def _even_weights(w_in):
    o = 0
    cols = {}
    for name, n in (("qkv", A_CONV_CH), ("beta", A_HEADS), ("a", A_HEADS), ("gate", A_WIDTH), ("qb", B_WIDTH),
                    ("kv", N_BRANCH * 2 * B_KV * B_DH), ("gb", B_HEADS * N_BRANCH)):
        cols[name] = w_in[:, o:o + n]
        o += n
    n_rows = 4 * B_KV * B_DH
    small = jnp.concatenate([cols["beta"], cols["a"], cols["gb"]], axis=1)
    small = jnp.pad(small, ((0, 0), (0, LANES - small.shape[1])))
    ws = [cols["qkv"], cols["gate"], cols["qb"], cols["kv"][:, :n_rows], cols["kv"][:, n_rows:], small]
    return [w.astype(bf16) for w in ws]


def _odd_weights(w_in):
    o = 0
    ws = []
    for n in (C_QK, C_QK, C_WIDTH, C_LOWRANK, C_WIDTH):
        ws.append(w_in[:, o:o + n])
        o += n
    ws[3] = jnp.pad(ws[3], ((0, 0), (0, LANES - C_LOWRANK)))
    return [w.astype(bf16) for w in ws]


def _lane_param(v, lane0):
    return jnp.zeros((1, LANES), f32).at[0, lane0:lane0 + v.shape[0]].set(v.astype(f32))


def _pad_rows(a, nb):
    return jnp.pad(a.reshape(nb, 1, -1), ((0, 0), (0, SAMPLE_PAD - 1), (0, 0)))


def kernel(x_prompt, x_sample, cache_nsa_kv, state_nsa_win, state_delta_conv, state_delta_S, state_gla_S,
           page_table, w_in_even, conv_w_delta, delta_A_log, delta_dt_bias, delta_norm_g, w_out_even,
           w_in_odd, w_gla_gate2, b_gla_gate2, gla_norm_g, w_out_odd, w_ffn_gate, w_ffn_up, w_ffn_down,
           ln_g, ln_b):
    bp, t, d = x_prompt.shape
    bs = x_sample.shape[0]
    assert x_sample.shape[1] == 1 and w_in_even.shape[0] == 1 and w_in_odd.shape[0] == 1
    n_pages = page_table.shape[1]
    past = n_pages * PAGE_SIZE
    xp = x_prompt.reshape(bp * t, d)
    xs = x_sample.reshape(bs, d)
    ln = lambda layer, j: (ln_g[layer, j].reshape(1, d), ln_b[layer, j].reshape(1, d))
    ffn_w = lambda layer: (w_ffn_gate[layer].astype(bf16), w_ffn_up[layer].astype(bf16),
                           w_ffn_down[layer].astype(bf16))

    ws = _even_weights(w_in_even[0])
    conv_w = conv_w_delta[0]
    alog = _lane_param(delta_A_log[0], A_HEADS)
    dtb = _lane_param(delta_dt_bias[0], A_HEADS)
    dng = delta_norm_g[0].reshape(1, A_DV)
    wo = w_out_even[0].astype(bf16)
    wo_parts = [wo[:A_WIDTH], wo[A_WIDTH:]]

    qkv_p, gate_p, qb_p, rows_p, win_p, small_p, rows_planes_p = _proj(xp, ws, planes_of=3)
    r3 = lambda a: a.reshape(bp, t, -1)
    o_a_p, ds_p = _delta(r3(qkv_p), r3(small_p), r3(gate_p), jnp.zeros((bp, 8, A_CONV_CH), f32), conv_w,
                         alog, dtb, dng, jnp.zeros((bp, A_HEADS, A_DK, A_DV), f32), t_valid=t)
    o_b_p = _nsa_prompt(r3(qb_p), r3(rows_p), r3(win_p), r3(small_p))
    xp = _outproj_ln([o_a_p.reshape(bp * t, -1), o_b_p.reshape(bp * t, -1)], wo_parts, xp, *ln(0, 0))
    xp = _ffn_ln(xp, *ffn_w(0), *ln(0, 1))

    qkv_s, gate_s, qb_s, rows_s, win_s, small_s = _proj(xs, ws)
    conv_s = state_delta_conv[0]
    o_a_s, ds_s = _delta(_pad_rows(qkv_s, bs), _pad_rows(small_s, bs), _pad_rows(gate_s, bs),
                         jnp.pad(conv_s, ((0, 0), (8 - (CONV_W - 1), 0), (0, 0))), conv_w, alog, dtb, dng,
                         state_delta_S[0], t_valid=1)
    win_buf = state_nsa_win[0]
    wb = win_buf.shape[1]
    r1 = lambda a: a.reshape(bs, 1, -1)
    o_b_s = _nsa_decode(page_table, r1(qb_s), r1(rows_s), r1(win_s), r1(small_s), win_buf.reshape(bs, wb, -1),
                        cache_nsa_kv[:, 0].transpose(0, 2, 3, 4, 1))
    xs = _outproj_ln([o_a_s[:, 0], o_b_s.reshape(bs, -1)], wo_parts, xs, *ln(0, 0))
    xs = _ffn_ln(xs, *ffn_w(0), *ln(0, 1))

    kvd = (B_KV, B_DH)
    nsa_rows_p = rows_planes_p.reshape((1, bp, t, 4) + kvd)
    nsa_win_p = r3(win_p)[:, -min(WINDOW, t):].reshape((1, bp, min(WINDOW, t), 2) + kvd)
    delta_conv_p = jnp.concatenate([jnp.zeros((bp, CONV_W - 1, A_CONV_CH), f32), r3(qkv_p)], axis=1)[:, -(CONV_W - 1):][None]
    nsa_rows_s = rows_s.reshape((1, bs, 1, 4) + kvd)
    win_cat = jnp.concatenate([win_buf, win_s.reshape((bs, 1, 2) + kvd)], axis=1)
    nsa_win_s = win_cat[:, -min(WINDOW, wb + 1):][None]
    delta_conv_s = jnp.concatenate([conv_s, qkv_s[:, None, :]], axis=1)[:, -(CONV_W - 1):][None]

    wq, wk, wv, wg1, wr = _odd_weights(w_in_odd[0])
    wg2 = jnp.pad(w_gla_gate2[0], ((0, LANES - C_LOWRANK), (0, 0))).astype(bf16)
    bg2 = b_gla_gate2[0].reshape(1, C_QK)
    gng = gla_norm_g[0].reshape(1, C_DV)
    wo1 = [w_out_odd[0].astype(bf16)]

    q_p, k_p, v_p, g1_p, rr_p = _proj(xp, [wq, wk, wv, wg1, wr])
    o_c_p, gs_p = _gla(r3(q_p), r3(k_p), r3(v_p), r3(rr_p), r3(g1_p), wg2, bg2, gng,
                       jnp.zeros((bp, C_HEADS, C_DK, C_DV), f32), t_valid=t)
    xp = _outproj_ln([o_c_p.reshape(bp * t, -1)], wo1, xp, *ln(1, 0))
    xp = _ffn_ln(xp, *ffn_w(1), *ln(1, 1))

    q_s, k_s, v_s, g1_s, rr_s = _proj(xs, [wq, wk, wv, wg1, wr])
    o_c_s, gs_s = _gla(_pad_rows(q_s, bs), _pad_rows(k_s, bs), _pad_rows(v_s, bs), _pad_rows(rr_s, bs),
                       _pad_rows(g1_s, bs), wg2, bg2, gng, state_gla_S[0], t_valid=1)
    xs = _outproj_ln([o_c_s[:, 0]], wo1, xs, *ln(1, 0))
    xs = _ffn_ln(xs, *ffn_w(1), *ln(1, 1))

    return (xp.reshape(bp, t, d), xs.reshape(bs, 1, d),
            nsa_rows_p, nsa_win_p, delta_conv_p, ds_p[None], gs_p[None],
            nsa_rows_s, nsa_win_s, delta_conv_s, ds_s[None], gs_s[None])
```

```python
import functools
import math

import jax
import jax.numpy as jnp
from jax import lax
from jax.experimental import pallas as pl
from jax.experimental.pallas import tpu as pltpu

f32 = jnp.float32
bf16 = jnp.bfloat16
HI = lax.Precision.HIGHEST

D_MODEL = 1024
DEPTH = 2
PAGE_SIZE = 128

A_HEADS = 4
A_DK = 128
A_DV = 128
A_QK = A_HEADS * A_DK
A_WIDTH = A_HEADS * A_DV
A_CONV_CH = 2 * A_QK + A_WIDTH
CONV_W = 4
DELTA_CHUNK = 64

B_HEADS = 8
B_KV = 2
B_HPG = B_HEADS // B_KV
B_DH = 64
B_WIDTH = B_HEADS * B_DH
N_BRANCH = 3
CMP_BLK = 32
SEL_BLK = 64
SEL_TOPK = 16
WINDOW = 512
B_SCALE = B_DH ** -0.5

C_HEADS = 4
C_DK = 128
C_DV = 256
C_QK = C_HEADS * C_DK
C_WIDTH = C_HEADS * C_DV
C_LOWRANK = 16
GLA_TAU = 16.0
GLA_CHUNK = 64
GLA_SUB = 16
GLA_PREP_CHUNKS = 2

D_FF = -(-(8 * D_MODEL) // (3 * 256)) * 256
ALPHA = (2 * DEPTH) ** 0.25
NEG = -1e30
FORCE = 1e9
LOG2E = 1.4426950408889634

LANES = 128
VMEM_LIMIT = 56 * 1024 * 1024


def _cparams(sem):
    return pltpu.CompilerParams(dimension_semantics=sem, vmem_limit_bytes=VMEM_LIMIT)


def _dot(a, b, precision=None):
    return jnp.dot(a, b, preferred_element_type=f32, precision=precision)


def _dot_nt(a, b, precision=None):
    return lax.dot_general(a, b, (((1,), (1,)), ((), ())), preferred_element_type=f32, precision=precision)


def _dot_tn(a, b, precision=None):
    return lax.dot_general(a, b, (((0,), (0,)), ((), ())), preferred_element_type=f32, precision=precision)


def _silu(x):
    return x * jax.nn.sigmoid(x)


def _layer_norm(x, g, b, eps=1e-5):
    mu = jnp.mean(x, -1, keepdims=True)
    xc = x - mu
    var = jnp.mean(xc * xc, -1, keepdims=True)
    return xc * lax.rsqrt(var + eps) * g + b


def _row_tile(m, pref):
    t = min(pref, m)
    assert m % t == 0
    return t


def _proj_kernel(x_ref, *refs, n, planes_of):
    xb = x_ref[...].astype(bf16)
    tm = x_ref.shape[0]
    for idx, (w_ref, o_ref) in enumerate(zip(refs[:n], refs[n:2 * n])):
        val = _dot(xb, w_ref[...])
        o_ref[...] = val
        if idx == planes_of:
            p_ref = refs[2 * n]
            n_planes = val.shape[1] // B_DH
            for j in range(n_planes):
                p_ref[pl.ds(j, tm, stride=n_planes), :] = val[:, j * B_DH:(j + 1) * B_DH]


def _proj(x, weights, tm=512, planes_of=None):
    m, k = x.shape
    tm = _row_tile(m, tm)
    in_specs = [pl.BlockSpec((tm, k), lambda i: (i, 0))]
    in_specs += [pl.BlockSpec(w.shape, lambda i: (0, 0)) for w in weights]
    out_specs = [pl.BlockSpec((tm, w.shape[1]), lambda i: (i, 0)) for w in weights]
    out_shape = [jax.ShapeDtypeStruct((m, w.shape[1]), f32) for w in weights]
    if planes_of is not None:
        n_planes = weights[planes_of].shape[1] // B_DH
        out_specs.append(pl.BlockSpec((tm * n_planes, B_DH), lambda i: (i, 0)))
        out_shape.append(jax.ShapeDtypeStruct((m * n_planes, B_DH), f32))
    return pl.pallas_call(
        functools.partial(_proj_kernel, n=len(weights), planes_of=planes_of),
        grid=(m // tm,), in_specs=in_specs, out_specs=out_specs, out_shape=out_shape,
        compiler_params=_cparams(("parallel",)), name="in_proj")(x, *weights)


def _outproj_ln_kernel(*refs, n_parts):
    parts = refs[:n_parts]
    ws = refs[n_parts:2 * n_parts]
    x_ref, g_ref, b_ref, o_ref = refs[2 * n_parts:]
    y = None
    for p_ref, w_ref in zip(parts, ws):
        d = _dot(p_ref[...].astype(bf16), w_ref[...])
        y = d if y is None else y + d
    o_ref[...] = _layer_norm(ALPHA * x_ref[...] + y, g_ref[...], b_ref[...])


def _outproj_ln(parts, weights, x, g, b, tm=512):
    m, d = x.shape
    tm = _row_tile(m, tm)
    n = len(parts)
    in_specs = [pl.BlockSpec((tm, p.shape[1]), lambda i: (i, 0)) for p in parts]
    in_specs += [pl.BlockSpec(w.shape, lambda i: (0, 0)) for w in weights]
    in_specs += [pl.BlockSpec((tm, d), lambda i: (i, 0)),
                 pl.BlockSpec((1, d), lambda i: (0, 0)), pl.BlockSpec((1, d), lambda i: (0, 0))]
    return pl.pallas_call(
        functools.partial(_outproj_ln_kernel, n_parts=n), grid=(m // tm,), in_specs=in_specs,
        out_specs=pl.BlockSpec((tm, d), lambda i: (i, 0)), out_shape=jax.ShapeDtypeStruct((m, d), f32),
        compiler_params=_cparams(("parallel",)), name="out_proj_ln")(*parts, *weights, x, g, b)


FFN_COLS = 256


def _ffn_ln_kernel(x_ref, wg_ref, wu_ref, wd_ref, g_ref, b_ref, o_ref):
    x = x_ref[...]
    xb = x.astype(bf16)
    acc = jnp.zeros(x.shape, f32)
    for c in range(0, D_FF, FFN_COLS):
        hg = _dot(xb, wg_ref[:, c:c + FFN_COLS])
        hu = _dot(xb, wu_ref[:, c:c + FFN_COLS])
        h = (_silu(hg) * hu).astype(bf16)
        acc = acc + _dot(h, wd_ref[c:c + FFN_COLS, :])
    o_ref[...] = _layer_norm(ALPHA * x + acc, g_ref[...], b_ref[...])


def _ffn_ln(x, wg, wu, wd, g, b, tm=512):
    m, d = x.shape
    tm = _row_tile(m, tm)
    full = lambda a: pl.BlockSpec(a.shape, lambda i: (0, 0))
    return pl.pallas_call(
        _ffn_ln_kernel, grid=(m // tm,),
        in_specs=[pl.BlockSpec((tm, d), lambda i: (i, 0)), full(wg), full(wu), full(wd), full(g), full(b)],
        out_specs=pl.BlockSpec((tm, d), lambda i: (i, 0)), out_shape=jax.ShapeDtypeStruct((m, d), f32),
        compiler_params=_cparams(("parallel",)), name="ffn_ln")(x, wg, wu, wd, g, b)


DELTA_INV_BLK = 16


def _tri_masks(c):
    r = lax.broadcasted_iota(jnp.int32, (c, c), 0)
    s = lax.broadcasted_iota(jnp.int32, (c, c), 1)
    return r, s


def _split2(x):
    hi = x.astype(bf16)
    return hi, (x - hi.astype(f32)).astype(bf16)


def _split3(x):
    hi = x.astype(bf16)
    rest = x - hi.astype(f32)
    mid = rest.astype(bf16)
    return hi, mid, (rest - mid.astype(f32)).astype(bf16)


def _dot_x3(a, b):
    ah, al = _split2(a)
    bh, bl = _split2(b)
    return _dot(ah, bh) + (_dot(ah, bl) + _dot(al, bh))


def _dot_ones(a_ones, b):
    a16 = a_ones.astype(bf16)
    b1, b2, b3 = _split3(b)
    return _dot(a16, b1) + (_dot(a16, b2) + _dot(a16, b3))


def _unit_lower_inverses(mats, r, s):
    c = mats[0].shape[0]
    eye = (r == s).astype(f32)
    same_blk = (r // DELTA_INV_BLK) == (s // DELTA_INV_BLK)
    diag = [jnp.where(same_blk, a, 0.0) for a in mats]
    low = [a - d for a, d in zip(mats, diag)]
    pw = [-d for d in diag]
    p = [eye + x for x in pw]
    k = 2
    while k < DELTA_INV_BLK:
        pw = [_dot_x3(x, x) for x in pw]
        p = [pi + _dot_x3(pi, x) for pi, x in zip(p, pw)]
        k *= 2
    pw = [_dot_x3(pi, lo) for pi, lo in zip(p, low)]
    q = [eye - x for x in pw]
    k = 2
    while k < c // DELTA_INV_BLK:
        pw = [_dot_x3(x, x) for x in pw]
        q = [qi + _dot_x3(qi, x) for qi, x in zip(q, pw)]
        k *= 2
    return [_dot_x3(qi, pi) for qi, pi in zip(q, p)]


DELTA_PREP_CHUNKS = 2


def _delta_kernel(qkv_ref, small_ref, gate_ref, cbuf_ref, cw_ref, alog_ref, dtb_ref, dng_ref, s0_ref,
                  o_ref, s_ref, xbuf, ybuf, tail, u_ref, w_ref, qg_ref, kd_ref, qk_ref, gl_ref,
                  *, tt, t_valid, t_total):
    t = pl.program_id(1)
    c = DELTA_CHUNK
    n_chunks = tt // c
    cpi = math.gcd(DELTA_PREP_CHUNKS, n_chunks)

    @pl.when(t == 0)
    def _():
        s_ref[...] = s0_ref[...]
        tail[...] = cbuf_ref[...]

    xbuf[0:8, :] = tail[...]
    xbuf[8:8 + tt, :] = qkv_ref[...]
    tail[...] = xbuf[tt:tt + 8, :]

    def row0(chunk):
        return chunk * c if isinstance(chunk, int) else pl.multiple_of(chunk * c, c)

    def conv_group(gi):
        for cc in range(cpi):
            r0 = row0(gi * cpi + cc)
            for col in range(0, A_CONV_CH, A_QK):
                cols = slice(col, col + A_QK)
                x = xbuf[pl.ds(r0, c + 8), cols]
                first = 8 - (CONV_W - 1)
                y = x[first:first + c] * cw_ref[0:1, cols]
                for j in range(1, CONV_W):
                    y = y + x[first + j:first + j + c] * cw_ref[j:j + 1, cols]
                ybuf[pl.ds(r0, c), cols] = _silu(y)

    r, s = _tri_masks(c)
    causal = r >= s
    strict = r > s
    lower_ones = causal.astype(f32)
    upper4 =jnp.concatenate([(r <= s).astype(f32)] * A_HEADS, axis=1)
    all_ones = jnp.ones((c, c), f32)
    alog = alog_ref[...]
    dtb = dtb_ref[...]
    dng = dng_ref[...]

    def prep_group(gi):
        items = []
        for cc in range(cpi):
            r0 = row0(gi * cpi + cc)
            sm = small_ref[pl.ds(r0, c), :]
            beta_all = jax.nn.sigmoid(sm)
            g_all = -jnp.exp(alog) * jax.nn.softplus(sm + dtb)
            if t_valid < t_total:
                live = t * tt + r0 + lax.broadcasted_iota(jnp.int32, (c, 1), 0) < t_valid
                beta_all = jnp.where(live, beta_all, 0.0)
                g_all = jnp.where(live, g_all, 0.0)
            gam_c_all = _dot_ones(lower_ones, g_all)
            g_rows = jnp.concatenate(
                [jnp.broadcast_to(g_all[:, A_HEADS + h:A_HEADS + h + 1], (c, c)) for h in range(A_HEADS)], axis=1)
            gam_r_all = _dot_ones(all_ones, g_rows * upper4)
            for h in range(A_HEADS):
                qh = ybuf[pl.ds(r0, c), h * A_DK:(h + 1) * A_DK]
                kh = ybuf[pl.ds(r0, c), A_QK + h * A_DK:A_QK + (h + 1) * A_DK]
                vh = ybuf[pl.ds(r0, c), 2 * A_QK + h * A_DV:2 * A_QK + (h + 1) * A_DV]
                qh = qh * lax.rsqrt(jnp.sum(qh * qh, -1, keepdims=True) + 1e-6) * (A_DK ** -0.5)
                kh = kh * lax.rsqrt(jnp.sum(kh * kh, -1, keepdims=True) + 1e-6)
                if t_valid < t_total:
                    qh = jnp.where(live, qh, 0.0)
                    kh = jnp.where(live, kh, 0.0)
                    vh = jnp.where(live, vh, 0.0)
                beta = beta_all[:, h:h + 1]
                gam_c = jnp.broadcast_to(gam_c_all[:, A_HEADS + h:A_HEADS + h + 1], (c, A_DK))
                diff = gam_c[:, :c] - gam_r_all[:, h * c:(h + 1) * c]
                ldec = jnp.where(causal, jnp.exp(jnp.where(causal, diff, 0.0)), 0.0)
                items.append(dict(r0=r0, h=h, q=qh, k=kh, v=vh, beta=beta, gam=gam_c, ldec=ldec, kb=kh * beta))
        amats = [jnp.where(strict, _dot_nt(it["kb"], it["k"]) * it["ldec"], 0.0) for it in items]
        tms = _unit_lower_inverses(amats, r, s)
        for it, tm in zip(items, tms):
            r0, h = it["r0"], it["h"]
            cols = slice(h * A_DK, (h + 1) * A_DK)
            eg = jnp.exp(it["gam"])
            g_last = it["gam"][c - 1:c, :]
            u_ref[pl.ds(r0, c), cols] = _dot(tm, it["v"] * it["beta"])
            w_ref[pl.ds(r0, c), cols] = _dot(tm, it["kb"] * eg)
            qg_ref[pl.ds(r0, c), cols] = it["q"] * eg
            kd_ref[pl.ds(r0, c), cols] = it["k"] * jnp.exp(g_last - it["gam"])
            qk_ref[pl.ds(r0, c), h * c:(h + 1) * c] = _dot_nt(it["q"], it["k"]) * it["ldec"]
            gl_ref[pl.ds(r0, 1), cols] = jnp.exp(g_last)

    def scan_group(gi):
        for cc in range(cpi):
            r0 = row0(gi * cpi + cc)
            for h in range(A_HEADS):
                cols = slice(h * A_DK, (h + 1) * A_DK)
                st = s_ref[h]
                v_new = u_ref[pl.ds(r0, c), cols] - _dot(w_ref[pl.ds(r0, c), cols], st)
                o = _dot(qg_ref[pl.ds(r0, c), cols], st) + _dot(qk_ref[pl.ds(r0, c), h * c:(h + 1) * c], v_new)
                s_ref[h] = st * gl_ref[pl.ds(r0, 1), cols] + _dot_tn(kd_ref[pl.ds(r0, c), cols], v_new)
                o = o * lax.rsqrt(jnp.mean(o * o, -1, keepdims=True) + 1e-6) * dng
                gt = gate_ref[pl.ds(r0, c), h * A_DV:(h + 1) * A_DV]
                o_ref[pl.ds(r0, c), h * A_DV:(h + 1) * A_DV] = o * _silu(gt)

    n_groups = n_chunks // cpi
    conv_group(0)
    if n_groups > 1:
        prep_group(0)
        conv_group(1)

        def steady(gi, carry):
            scan_group(gi - 1)
            prep_group(gi)
            conv_group(gi + 1)
            return carry

        lax.fori_loop(1, n_groups - 1, steady, 0)
        scan_group(n_groups - 2)
    prep_group(n_groups - 1)
    scan_group(n_groups - 1)


def _delta(qkv, small, gate, conv_buf8, conv_w, alog, dtb, dng, s0, *, t_valid, tt=1024):
    b, t_total, _ = qkv.shape
    tt = _row_tile(t_total, tt)
    nt = t_total // tt
    row = lambda w: pl.BlockSpec((None, tt, w), lambda i, j: (i, j, 0))
    full2 = lambda a: pl.BlockSpec(a.shape, lambda i, j: (0, 0))
    st_spec = pl.BlockSpec((None, A_HEADS, A_DK, A_DV), lambda i, j: (i, 0, 0, 0))
    kern = functools.partial(_delta_kernel, tt=tt, t_valid=t_valid, t_total=t_total)
    return pl.pallas_call(
        kern, grid=(b, nt),
        in_specs=[row(A_CONV_CH), row(LANES), row(A_WIDTH),
                  pl.BlockSpec((None, 8, A_CONV_CH), lambda i, j: (i, 0, 0)),
                  full2(conv_w), full2(alog), full2(dtb), full2(dng), st_spec],
        out_specs=[row(A_WIDTH), st_spec],
        out_shape=[jax.ShapeDtypeStruct((b, t_total, A_WIDTH), f32),
                   jax.ShapeDtypeStruct((b, A_HEADS, A_DK, A_DV), f32)],
        scratch_shapes=[pltpu.VMEM((tt + 8, A_CONV_CH), f32), pltpu.VMEM((tt, A_CONV_CH), f32),
                        pltpu.VMEM((8, A_CONV_CH), f32)]
        + [pltpu.VMEM((tt, A_QK), f32)] * 4
        + [pltpu.VMEM((tt, A_HEADS * DELTA_CHUNK), f32), pltpu.VMEM((tt, A_QK), f32)],
        compiler_params=_cparams(("parallel", "arbitrary")), name="delta_mixer",
    )(qkv, small, gate, conv_buf8, conv_w, alog, dtb, dng, s0)


def _gla_kernel(q_ref, k_ref, v_ref, r_ref, g1_ref, wg2_ref, bg2_ref, gng_ref, s0_ref,
                o_ref, s_ref, st_ref, la_ref, qe_ref, gl_ref, ds_ref, *, tt, t_valid, t_total):
    t = pl.program_id(1)
    nt = pl.num_programs(1)
    c = GLA_CHUNK
    heads = range(C_HEADS)
    kcols = [slice(h * C_DK, (h + 1) * C_DK) for h in heads]
    vcols = [slice(h * C_DV, (h + 1) * C_DV) for h in heads]

    @pl.when(t == 0)
    def _():
        for h in range(C_HEADS):
            st_ref[h] = s0_ref[h].T

    la = jax.nn.log_sigmoid(_dot(g1_ref[...].astype(bf16), wg2_ref[...]) + bg2_ref[...]) / GLA_TAU
    if t_valid < t_total:
        rows = t * tt + lax.broadcasted_iota(jnp.int32, (tt, 1), 0)
        la = jnp.where(rows < t_valid, la, 0.0)
    la_ref[...] = la

    r, s = _tri_masks(c)
    lower_ones = (r >= s).astype(f32)
    rs = lax.broadcasted_iota(jnp.int32, (GLA_SUB, c), 0)
    ss = lax.broadcasted_iota(jnp.int32, (GLA_SUB, c), 1)
    krow = lax.broadcasted_iota(jnp.int32, (c, 1), 0)
    gng = gng_ref[...]

    n_chunks = tt // c
    cpi = math.gcd(GLA_PREP_CHUNKS, n_chunks)

    def row0(chunk):
        return chunk * c if isinstance(chunk, int) else pl.multiple_of(chunk * c, c)

    def prep_group(gi):
        items = []
        for cc in range(cpi):
            chunk = gi * cpi + cc
            r0 = row0(chunk)
            rows = pl.ds(r0, c)
            bc_all = _dot_ones(lower_ones, la_ref[rows, :])
            for h in heads:
                items.append(dict(chunk=chunk, r0=r0, rows=rows, h=h, q=q_ref[rows, kcols[h]] * (C_DK ** -0.5),
                                  k=k_ref[rows, kcols[h]], v=v_ref[rows, vcols[h]], bc=bc_all[:, kcols[h]]))
        for it in items:
            qe_ref[it["rows"], kcols[it["h"]]] = it["q"] * jnp.exp(it["bc"])
        for a in range(c // GLA_SUB):
            lo, hi = a * GLA_SUB, (a + 1) * GLA_SUB
            seen = krow < hi
            att = []
            for it in items:
                bref = it["bc"][lo - 1:lo, :] if a > 0 else jnp.zeros((1, C_DK), f32)
                qa = it["q"][lo:hi, :] * jnp.exp(it["bc"][lo:hi, :] - bref)
                ka = jnp.where(seen, it["k"] * jnp.exp(jnp.where(seen, bref - it["bc"], 0.0)), 0.0)
                att.append(jnp.where(rs + lo >= ss, _dot_nt(qa, ka), 0.0))
            for it, at in zip(items, att):
                o_ref[pl.ds(it["r0"] + lo, GLA_SUB), vcols[it["h"]]] = _dot(at, it["v"])
        for it in items:
            b_last = it["bc"][c - 1:c, :]
            gl_ref[pl.ds(it["r0"], 1), kcols[it["h"]]] = jnp.exp(b_last)
            ds_ref[it["chunk"], it["h"]] = _dot_tn(it["v"], it["k"] * jnp.exp(b_last - it["bc"]))

    def scan_group(gi):
        for cc in range(cpi):
            chunk = gi * cpi + cc
            r0 = row0(chunk)
            rows = pl.ds(r0, c)
            for h in heads:
                st = st_ref[h]
                o = _dot_nt(qe_ref[rows, kcols[h]], st) + o_ref[rows, vcols[h]]
                st_ref[h] = st * gl_ref[pl.ds(r0, 1), kcols[h]] + ds_ref[chunk, h]
                o = o * lax.rsqrt(jnp.mean(o * o, -1, keepdims=True) + 1e-6) * gng
                o_ref[rows, vcols[h]] = o * _silu(r_ref[rows, vcols[h]])

    n_groups = n_chunks // cpi
    prep_group(0)

    def steady(gi, carry):
        scan_group(gi - 1)
        prep_group(gi)
        return carry

    lax.fori_loop(1, n_groups, steady, 0)
    scan_group(n_groups - 1)

    @pl.when(t == nt - 1)
    def _():
        for h in range(C_HEADS):
            s_ref[h] = st_ref[h].T


def _gla(q, k, v, r, g1, wg2, bg2, gng, s0, *, t_valid, tt=512):
    b, t_total, _ = q.shape
    tt = _row_tile(t_total, tt)
    nt = t_total // tt
    row = lambda w: pl.BlockSpec((None, tt, w), lambda i, j: (i, j, 0))
    full2 = lambda a: pl.BlockSpec(a.shape, lambda i, j: (0, 0))
    st_spec = pl.BlockSpec((None, C_HEADS, C_DK, C_DV), lambda i, j: (i, 0, 0, 0))
    kern = functools.partial(_gla_kernel, tt=tt, t_valid=t_valid, t_total=t_total)
    return pl.pallas_call(
        kern, grid=(b, nt),
        in_specs=[row(C_QK), row(C_QK), row(C_WIDTH), row(C_WIDTH), row(LANES),
                  full2(wg2), full2(bg2), full2(gng), st_spec],
        out_specs=[row(C_WIDTH), st_spec],
        out_shape=[jax.ShapeDtypeStruct((b, t_total, C_WIDTH), f32),
                   jax.ShapeDtypeStruct((b, C_HEADS, C_DK, C_DV), f32)],
        scratch_shapes=[pltpu.VMEM((C_HEADS, C_DV, C_DK), f32), pltpu.VMEM((tt, C_QK), f32),
                        pltpu.VMEM((tt, C_QK), f32), pltpu.VMEM((tt, C_QK), f32),
                        pltpu.VMEM((tt // GLA_CHUNK, C_HEADS, C_DV, C_DK), f32)],
        compiler_params=_cparams(("parallel", "arbitrary")), name="gla_mixer",
    )(q, k, v, r, g1, wg2, bg2, gng, s0)


GB_LANE0 = 2 * A_HEADS


def _masked_softmax(s, mask):
    sm = jnp.where(mask, s, NEG)
    e = jnp.exp(sm - jnp.max(sm, -1, keepdims=True))
    p = e / jnp.sum(e, -1, keepdims=True)
    return jnp.where(mask, p, 0.0)


def _topk_rank(score, n_cand):
    lane = lax.broadcasted_iota(jnp.int32, score.shape, score.ndim - 1)
    rank = jnp.zeros(score.shape, jnp.int32)
    for i in range(n_cand):
        si = score[..., i:i + 1]
        ahead = (si > score) | ((si == score) & (i < lane))
        rank = rank + ahead.astype(jnp.int32)
    return rank


def _block_scores(imp, qpos, n_blk):
    blk = lax.broadcasted_iota(jnp.int32, imp.shape, imp.ndim - 1)
    cur = qpos // SEL_BLK
    valid = blk * SEL_BLK <= qpos
    forced = (blk == 0) | (blk == cur) | (blk == cur - 1)
    return jnp.where(forced, FORCE, jnp.where(valid, imp, NEG))


def _split_cmp_index(col, half):
    return jnp.where(col < half, 2 * col, 2 * (col - half) + 1)


NSA_TQ = 128
NSA_QSUB = 2
NSA_TK = 512
NSA_SUB = 128


def _topk_mask_rows(score, k):
    n = score.shape[0]
    row = lax.broadcasted_iota(jnp.int32, score.shape, 0)
    rank = jnp.zeros(score.shape, jnp.int32)
    for i in range(n):
        si = score[i:i + 1, :]
        ahead = (si > score) | ((si == score) & (i < row))
        rank = rank + ahead.astype(jnp.int32)
    return (rank < k).astype(f32)


def _nsa_prompt_kernel(q_ref, rows_ref, win_ref, small_ref, o_ref, cb_ref, sel_ref, sc_ref, *, t_total):
    i = pl.program_id(1)
    tq = NSA_TQ
    tk = min(NSA_TK, t_total)
    tw = min(WINDOW + tq, t_total)
    n_sel = t_total // SEL_BLK
    half = n_sel
    hq = B_HPG * tq
    blk_per_tile = tk // SEL_BLK

    @pl.when(i == 0)
    def _():
        x = rows_ref[:, 0:2 * B_KV * B_DH].reshape(n_sel, SEL_BLK, 2 * B_KV * B_DH)
        cb_ref[0:half, :] = jnp.sum(x[:, :CMP_BLK, :], axis=1) * (1.0 / CMP_BLK)
        cb_ref[half:2 * half, :] = jnp.sum(x[:, CMP_BLK:, :], axis=1) * (1.0 / CMP_BLK)

    n_qsub = o_ref.shape[0] // tq
    chains = [(s, g) for s in range(n_qsub) for g in range(B_KV)]
    tile0 = [(i * n_qsub + s) * tq for s in range(n_qsub)]
    qpos_s = [t0 + lax.broadcasted_iota(jnp.int32, (1, tq), 1) for t0 in tile0]
    gates_t = jax.nn.sigmoid(small_ref[...].T)
    n_causal = (tile0[-1] + tq - 1) // tk + 1
    q4 = [jnp.concatenate([q_ref[s * tq:(s + 1) * tq, (g * B_HPG + h) * B_DH:(g * B_HPG + h + 1) * B_DH]
                           for h in range(B_HPG)], axis=0) * (B_SCALE * LOG2E)
          for s, g in chains]

    o_cmp = []
    for c, (s, g) in enumerate(chains):
        qpos = qpos_s[s]
        qpos4 = jnp.concatenate([qpos] * B_HPG, axis=1)
        kc = cb_ref[:, g * B_DH:(g + 1) * B_DH]
        vc = cb_ref[:, (B_KV + g) * B_DH:(B_KV + g + 1) * B_DH]
        st = _dot_nt(kc, q4[c])
        cidx = _split_cmp_index(lax.broadcasted_iota(jnp.int32, (2 * half, 1), 0), half)
        cmask = (cidx + 1) * CMP_BLK <= qpos4 + 1
        sm = jnp.where(cmask, st, NEG)
        e = jnp.exp2(sm - jnp.max(sm, axis=0, keepdims=True))
        p = jnp.where(cmask, e / jnp.sum(e, axis=0, keepdims=True), 0.0)
        o_cmp.append(_dot_tn(vc, p))
        imp = p[:, 0:tq]
        for h in range(1, B_HPG):
            imp = imp + p[:, h * tq:(h + 1) * tq]
        imp = imp[:half] + imp[half:]
        blk = lax.broadcasted_iota(jnp.int32, (n_sel, 1), 0)
        cur = qpos // SEL_BLK
        forced = (blk == 0) | (blk == cur) | (blk == cur - 1)
        score = jnp.where(forced, FORCE, jnp.where(blk * SEL_BLK <= qpos, imp, NEG))
        picked = (_topk_mask_rows(score, min(SEL_TOPK, n_sel)) > 0.5) & (blk * SEL_BLK <= qpos)
        sel_ref[c] = jnp.where(picked, 0.0, NEG)

    sub = min(NSA_SUB, tk)

    def attend(c, kv_ref, k0, n_keys, kl, vl, bias_fn, carry):
        m, acc = carry
        m_new = m
        for u in range(n_keys // sub):
            k = kv_ref[pl.ds(k0 + u * sub, sub), kl:kl + B_DH]
            bias = bias_fn(u)
            sc = _dot_nt(k, q4[c]) + jnp.concatenate([bias] * B_HPG, axis=1)
            sc_ref[c, u * sub:(u + 1) * sub, :] = sc
            m_new = jnp.maximum(m_new, jnp.max(sc, axis=0, keepdims=True))
        acc = jnp.exp2(m - m_new) * acc
        for u in range(n_keys // sub):
            pe = jnp.exp2(sc_ref[c, u * sub:(u + 1) * sub, :] - m_new)
            v1 = jnp.concatenate([kv_ref[pl.ds(k0 + u * sub, sub), vl:vl + B_DH], ones_blk], axis=1)
            acc = acc + _dot_tn(v1, pe)
        return m_new, acc

    ones_blk = jnp.ones((sub, B_DH), f32)
    init = (jnp.full((1, hq), NEG, f32), jnp.zeros((2 * B_DH, hq), f32))
    srow = lax.broadcasted_iota(jnp.int32, (sub, 1), 0)
    blk_per_sub = sub // SEL_BLK
    assert sub == tq
    own_block_bias = jnp.where(srow <= lax.broadcasted_iota(jnp.int32, (1, tq), 1), 0.0, NEG)

    def sel_step(j, carries):
        k0 = pl.multiple_of(j * tk, tk)
        out = []
        for c, (s, g) in enumerate(chains):
            blk_bias = sel_ref[c, pl.ds(pl.multiple_of(j * blk_per_tile, blk_per_tile), blk_per_tile), :]

            def bias_fn(u, blk_bias=blk_bias, s=s):
                rows = jnp.concatenate(
                    [jnp.broadcast_to(blk_bias[u * blk_per_sub + v:u * blk_per_sub + v + 1, :], (SEL_BLK, tq))
                     for v in range(blk_per_sub)], axis=0)
                return rows + jnp.where(k0 + u * sub == tile0[s], own_block_bias, 0.0)

            out.append(attend(c, rows_ref, k0, tk, (2 * B_KV + g) * B_DH, (3 * B_KV + g) * B_DH, bias_fn,
                              carries[c]))
        return tuple(out)

    sel_out = lax.fori_loop(0, n_causal, sel_step, (init,) * len(chains))
    w0 = [pl.multiple_of(jnp.maximum(t0 + tq - tw, 0), tq) for t0 in tile0]
    win_bias = []
    for s in range(n_qsub):
        per_sub = []
        for u in range(tw // sub):
            d = qpos_s[s] - (w0[s] + u * sub + srow)
            per_sub.append(jnp.where((d >= 0) & (d < WINDOW), 0.0, NEG))
        win_bias.append(per_sub)

    win_out = [attend(c, win_ref, w0[s], tw, g * B_DH, (B_KV + g) * B_DH, lambda u, s=s: win_bias[s][u], init)
               for c, (s, g) in enumerate(chains)]
    for s in range(n_qsub):
        out_rows = []
        for c, (cs, g) in enumerate(chains):
            if cs != s:
                continue
            acc_w, acc_s = win_out[c][1], sel_out[c][1]
            o_win = acc_w[:B_DH] / acc_w[B_DH:B_DH + 1]
            o_sel = acc_s[:B_DH] / acc_s[B_DH:B_DH + 1]
            gt = gates_t[:, s * tq:(s + 1) * tq]
            for h in range(B_HPG):
                c0 = GB_LANE0 + (g * B_HPG + h) * N_BRANCH
                ls = slice(h * tq, (h + 1) * tq)
                out_rows.append(gt[c0:c0 + 1, :] * o_cmp[c][:, ls] + gt[c0 + 1:c0 + 2, :] * o_sel[:, ls]
                                + gt[c0 + 2:c0 + 3, :] * o_win[:, ls])
        o_ref[s * tq:(s + 1) * tq, :] = jnp.concatenate(out_rows, axis=0).T


def _nsa_prompt(qb, rows, win, small):
    b, t_total, _ = qb.shape
    tk = min(NSA_TK, t_total)
    rows_per_step = NSA_TQ * NSA_QSUB
    assert t_total % tk == 0 and tk % NSA_TQ == 0 and tk % SEL_BLK == 0 and WINDOW % NSA_TQ == 0
    assert t_total % rows_per_step == 0
    nq = t_total // rows_per_step
    n_sel = t_total // SEL_BLK
    n_chains = NSA_QSUB * B_KV
    tile = lambda w: pl.BlockSpec((None, rows_per_step, w), lambda bi, i: (bi, i, 0))
    seq = lambda w: pl.BlockSpec((None, t_total, w), lambda bi, i: (bi, 0, 0))
    return pl.pallas_call(
        functools.partial(_nsa_prompt_kernel, t_total=t_total), grid=(b, nq),
        in_specs=[tile(B_WIDTH), seq(4 * B_KV * B_DH), seq(2 * B_KV * B_DH), tile(LANES)],
        out_specs=tile(B_WIDTH), out_shape=jax.ShapeDtypeStruct((b, t_total, B_WIDTH), f32),
        scratch_shapes=[pltpu.VMEM((2 * n_sel, 2 * B_KV * B_DH), f32), pltpu.VMEM((n_chains, n_sel, NSA_TQ), f32),
                        pltpu.VMEM((n_chains, max(tk, min(WINDOW + NSA_TQ, t_total)), B_HPG * NSA_TQ), f32)],
        compiler_params=_cparams(("parallel", "arbitrary")), name="nsa_prompt",
    )(qb, rows, win, small)


DECODE_POOL_PAGES = 8


def _nsa_decode_cmp_kernel(pt_ref, q_ref, wnew_ref, wbuf_ref, pool_ref, cache_ref, ocmp_ref, owin_ref, idx_ref,
                           pages, acc_ref, sem, *, n_pages, past):
    b = pl.program_id(0)
    nb = pl.num_programs(0)
    slot = b % 2
    n_hist_blk = past // SEL_BLK
    n_sel = -(-(past + 1) // SEL_BLK)
    half = n_hist_blk
    lanes_sel = 2 * half
    wb = wbuf_ref.shape[0]
    n_cmp_rows = 2 * B_KV * B_DH

    def page_copy(bb, p, sl):
        return pltpu.make_async_copy(cache_ref.at[pt_ref[bb, p], pl.ds(0, 2)], pages.at[sl, p], sem.at[sl])

    def fetch(bb, sl):
        for p in range(n_pages):
            page_copy(bb, p, sl).start()

    @pl.when(b == 0)
    def _():
        fetch(0, 0)

    @pl.when(b + 1 < nb)
    def _():
        fetch(b + 1, 1 - slot)

    for p in range(n_pages):
        page_copy(b, p, slot).wait()

    acc_ref[...] = jnp.zeros(acc_ref.shape, f32)
    ppc = math.gcd(n_pages, DECODE_POOL_PAGES)

    def pool_body(c, carry):
        x = jnp.concatenate([pages[slot, c * ppc + i].reshape(n_cmp_rows, PAGE_SIZE) for i in range(ppc)], axis=1)
        hi, lo = _split2(x)
        w = pool_ref[pl.ds(pl.multiple_of(c * ppc * PAGE_SIZE, ppc * PAGE_SIZE), ppc * PAGE_SIZE), :]
        acc_ref[...] += _dot(hi, w) + _dot(lo, w)
        return carry

    lax.fori_loop(0, n_pages // ppc, pool_body, 0)

    qpos = past
    row8 = lax.broadcasted_iota(jnp.int32, (B_HEADS, 1), 0)
    q8 = jnp.concatenate([q_ref[:, h * B_DH:(h + 1) * B_DH] for h in range(B_HEADS)], axis=0) * B_SCALE
    wnew = wnew_ref[...]
    lane = lax.broadcasted_iota(jnp.int32, (1, lanes_sel), 1)
    o_cmp, o_win, idx_rows = [], [], []
    for g in range(B_KV):
        in_g = (row8 // B_HPG) == g
        kc_t = acc_ref[g * B_DH:(g + 1) * B_DH, :]
        vc_t = acc_ref[(B_KV + g) * B_DH:(B_KV + g + 1) * B_DH, :]
        sc = _dot(q8, kc_t)
        cidx = _split_cmp_index(lane, half)
        p = _masked_softmax(sc, (cidx + 1) * CMP_BLK <= qpos + 1)
        o_cmp.append(_dot_nt(p, vc_t))
        imp = jnp.sum(jnp.where(in_g, p, 0.0), axis=0, keepdims=True)
        imp = imp[:, :half] + imp[:, half:]
        imp = jnp.concatenate([imp, jnp.zeros((1, lanes_sel - half), f32)], axis=1)
        rank = _topk_rank(_block_scores(imp, qpos, n_sel), n_sel)
        lane_f = lane.astype(f32)
        out_lane = lax.broadcasted_iota(jnp.int32, (1, LANES), 1)
        picked = jnp.zeros((1, LANES), f32)
        for r in range(min(SEL_TOPK, n_sel)):
            blk_r = jnp.sum(jnp.where((rank == r) & (lane < n_sel), lane_f, 0.0), axis=-1, keepdims=True)
            picked = picked + jnp.where(out_lane == r, blk_r, 0.0)
        idx_rows.append(picked)
        kw = wbuf_ref[:, g * B_DH:(g + 1) * B_DH]
        vw = wbuf_ref[:, (B_KV + g) * B_DH:(B_KV + g + 1) * B_DH]
        kw_new = wnew[:, g * B_DH:(g + 1) * B_DH]
        vw_new = wnew[:, (B_KV + g) * B_DH:(B_KV + g + 1) * B_DH]
        kpos = qpos - wb + lax.broadcasted_iota(jnp.int32, (1, wb), 1)
        d = qpos - kpos
        wmask = (d >= 0) & (d < WINDOW) & (kpos >= 0)
        sw = jnp.where(wmask, _dot_nt(q8, kw), NEG)
        swn = jnp.sum(q8 * kw_new, -1, keepdims=True)
        m = jnp.maximum(jnp.max(sw, -1, keepdims=True), swn)
        ew = jnp.where(wmask, jnp.exp(sw - m), 0.0)
        ewn = jnp.exp(swn - m)
        o_win.append((_dot(ew, vw) + ewn * vw_new) / (jnp.sum(ew, -1, keepdims=True) + ewn))
    lower = row8 < B_HPG
    ocmp_ref[...] = jnp.where(lower, o_cmp[0], o_cmp[1])
    owin_ref[...] = jnp.where(lower, o_win[0], o_win[1])
    idx_ref[...] = jnp.concatenate(idx_rows + [jnp.zeros((8 - B_KV, LANES), f32)], axis=0).astype(jnp.int32)


def _nsa_decode_sel_kernel(pt_ref, sel_ref, q_ref, rnew_ref, small_ref, ocmp_ref, owin_ref, cache_ref,
                           o_ref, blocks, sem, *, past):
    b = pl.program_id(0)
    nb = pl.num_programs(0)
    slot = b % 2
    n_hist_blk = past // SEL_BLK
    n_pick = sel_ref.shape[2]
    blk_per_page = PAGE_SIZE // SEL_BLK
    tok_lane = lax.broadcasted_iota(jnp.int32, (1, PAGE_SIZE), 1)

    def hist_block(bb, g, j):
        return jnp.minimum(sel_ref[bb, g, j], n_hist_blk - 1)

    def block_copy(bb, g, j, sl):
        page = pt_ref[bb, hist_block(bb, g, j) // blk_per_page]
        return pltpu.make_async_copy(cache_ref.at[page, pl.ds(2, 2), g], blocks.at[sl, g * n_pick + j], sem.at[sl])

    def fetch(bb, sl):
        for g in range(B_KV):
            for j in range(n_pick):
                block_copy(bb, g, j, sl).start()

    @pl.when(b == 0)
    def _():
        fetch(0, 0)

    @pl.when(b + 1 < nb)
    def _():
        fetch(b + 1, 1 - slot)

    for g in range(B_KV):
        for j in range(n_pick):
            block_copy(b, g, j, slot).wait()

    row8 = lax.broadcasted_iota(jnp.int32, (B_HEADS, 1), 0)
    q8 = jnp.concatenate([q_ref[:, h * B_DH:(h + 1) * B_DH] for h in range(B_HEADS)], axis=0) * B_SCALE
    rnew = rnew_ref[...]
    o_sel = []
    for g in range(B_KV):
        ks_t = jnp.concatenate([blocks[slot, g * n_pick + j, 0] for j in range(n_pick)], axis=1)
        vs_t = jnp.concatenate([blocks[slot, g * n_pick + j, 1] for j in range(n_pick)], axis=1)
        picks = [sel_ref[b, g, j] for j in range(n_pick)]
        live = jnp.concatenate(
            [(tok_lane // SEL_BLK == hist_block(b, g, j) % blk_per_page) & (picks[j] < n_hist_blk)
             for j in range(n_pick)], axis=1)
        new_live = functools.reduce(jnp.logical_or, [pk >= n_hist_blk for pk in picks])
        k_new = rnew[:, (2 * B_KV + g) * B_DH:(2 * B_KV + g + 1) * B_DH]
        v_new = rnew[:, (3 * B_KV + g) * B_DH:(3 * B_KV + g + 1) * B_DH]
        sh = jnp.where(live, _dot(q8, ks_t), NEG)
        sn = jnp.sum(q8 * k_new, -1, keepdims=True) + jnp.where(new_live, 0.0, NEG).astype(f32)
        m = jnp.maximum(jnp.max(sh, -1, keepdims=True), sn)
        eh = jnp.where(live, jnp.exp(sh - m), 0.0)
        en = jnp.where(new_live, jnp.exp(sn - m), 0.0)
        o_sel.append((_dot_nt(eh, vs_t) + en * v_new) / (jnp.sum(eh, -1, keepdims=True) + en))
    os_ = jnp.where(row8 < B_HPG, o_sel[0], o_sel[1])
    oc = ocmp_ref[...]
    ow = owin_ref[...]
    gates = jax.nn.sigmoid(small_ref[...])
    outs = []
    for hh in range(B_HEADS):
        c0 = GB_LANE0 + hh * N_BRANCH
        outs.append(gates[:, c0:c0 + 1] * oc[hh:hh + 1] + gates[:, c0 + 1:c0 + 2] * os_[hh:hh + 1]
                    + gates[:, c0 + 2:c0 + 3] * ow[hh:hh + 1])
    o_ref[...] = jnp.concatenate(outs, axis=1)


def _block_pool_matrix(past):
    blk = jnp.arange(past) // CMP_BLK
    col = (blk % 2) * (past // SEL_BLK) + blk // 2
    return jnp.where(col[:, None] == jnp.arange(2 * (past // SEL_BLK))[None, :], 1.0 / CMP_BLK, 0.0).astype(bf16)


def _nsa_decode(page_table, qb, rows_new, win_new, small, win_buf, cache_t):
    nb, n_pages = page_table.shape
    past = n_pages * PAGE_SIZE
    assert past % SEL_BLK == 0 and cache_t.shape[1:] == (4, B_KV, B_DH, PAGE_SIZE)
    n_blk = past // SEL_BLK
    n_pick = min(SEL_TOPK, n_blk + 1)
    pool = _block_pool_matrix(past)
    one1 = lambda shape: pl.BlockSpec((None,) + shape, lambda i, pt: (i, 0, 0))
    cmp_spec = pltpu.PrefetchScalarGridSpec(
        num_scalar_prefetch=1, grid=(nb,),
        in_specs=[one1((1, B_WIDTH)), one1((1, 2 * B_KV * B_DH)), one1(win_buf.shape[1:]),
                  pl.BlockSpec(pool.shape, lambda i, pt: (0, 0)), pl.BlockSpec(memory_space=pl.ANY)],
        out_specs=[one1((B_HEADS, B_DH)), one1((B_HEADS, B_DH)), one1((8, LANES))],
        scratch_shapes=[pltpu.VMEM((2, n_pages, 2, B_KV, B_DH, PAGE_SIZE), f32),
                        pltpu.VMEM((2 * B_KV * B_DH, 2 * n_blk), f32), pltpu.SemaphoreType.DMA((2,))])
    o_cmp, o_win, idx = pl.pallas_call(
        functools.partial(_nsa_decode_cmp_kernel, n_pages=n_pages, past=past), grid_spec=cmp_spec,
        out_shape=[jax.ShapeDtypeStruct((nb, B_HEADS, B_DH), f32), jax.ShapeDtypeStruct((nb, B_HEADS, B_DH), f32),
                   jax.ShapeDtypeStruct((nb, 8, LANES), jnp.int32)],
        compiler_params=_cparams(("arbitrary",)), name="nsa_decode_cmp",
    )(page_table, qb, win_new, win_buf, pool, cache_t)
    picks = idx[:, :B_KV, :n_pick]
    one2 = lambda shape: pl.BlockSpec((None,) + shape, lambda i, pt, sel: (i, 0, 0))
    sel_spec = pltpu.PrefetchScalarGridSpec(
        num_scalar_prefetch=2, grid=(nb,),
        in_specs=[one2((1, B_WIDTH)), one2((1, 4 * B_KV * B_DH)), one2((1, LANES)),
                  one2((B_HEADS, B_DH)), one2((B_HEADS, B_DH)), pl.BlockSpec(memory_space=pl.ANY)],
        out_specs=one2((1, B_WIDTH)),
        scratch_shapes=[pltpu.VMEM((2, B_KV * n_pick, 2, B_DH, PAGE_SIZE), f32), pltpu.SemaphoreType.DMA((2,))])
    return pl.pallas_call(
        functools.partial(_nsa_decode_sel_kernel, past=past), grid_spec=sel_spec,
        out_shape=jax.ShapeDtypeStruct((nb, 1, B_WIDTH), f32),
        compiler_params=_cparams(("arbitrary",)), name="nsa_decode_sel",
    )(page_table, picks, qb, rows_new, small, o_cmp, o_win, cache_t)


SAMPLE_PAD = DELTA_CHUNK


def _even_weights(w_in):
    o = 0
    cols = {}
    for name, n in (("qkv", A_CONV_CH), ("beta", A_HEADS), ("a", A_HEADS), ("gate", A_WIDTH), ("qb", B_WIDTH),
                    ("kv", N_BRANCH * 2 * B_KV * B_DH), ("gb", B_HEADS * N_BRANCH)):
        cols[name] = w_in[:, o:o + n]
        o += n
    n_rows = 4 * B_KV * B_DH
    small = jnp.concatenate([cols["beta"], cols["a"], cols["gb"]], axis=1)
    small = jnp.pad(small, ((0, 0), (0, LANES - small.shape[1])))
    ws = [cols["qkv"], cols["gate"], cols["qb"], cols["kv"][:, :n_rows], cols["kv"][:, n_rows:], small]
    return [w.astype(bf16) for w in ws]


def _odd_weights(w_in):
    o = 0
    ws = []
    for n in (C_QK, C_QK, C_WIDTH, C_LOWRANK, C_WIDTH):
        ws.append(w_in[:, o:o + n])
        o += n
    ws[3] = jnp.pad(ws[3], ((0, 0), (0, LANES - C_LOWRANK)))
    return [w.astype(bf16) for w in ws]


def _lane_param(v, lane0):
    return jnp.zeros((1, LANES), f32).at[0, lane0:lane0 + v.shape[0]].set(v.astype(f32))


def _pad_rows(a, nb):
    return jnp.pad(a.reshape(nb, 1, -1), ((0, 0), (0, SAMPLE_PAD - 1), (0, 0)))


def kernel(x_prompt, x_sample, cache_nsa_kv, state_nsa_win, state_delta_conv, state_delta_S, state_gla_S,
           page_table, w_in_even, conv_w_delta, delta_A_log, delta_dt_bias, delta_norm_g, w_out_even,
           w_in_odd, w_gla_gate2, b_gla_gate2, gla_norm_g, w_out_odd, w_ffn_gate, w_ffn_up, w_ffn_down,
           ln_g, ln_b):
    bp, t, d = x_prompt.shape
    bs = x_sample.shape[0]
    assert x_sample.shape[1] == 1 and w_in_even.shape[0] == 1 and w_in_odd.shape[0] == 1
    n_pages = page_table.shape[1]
    past = n_pages * PAGE_SIZE
    xp = x_prompt.reshape(bp * t, d)
    xs = x_sample.reshape(bs, d)
    ln = lambda layer, j: (ln_g[layer, j].reshape(1, d), ln_b[layer, j].reshape(1, d))
    ffn_w = lambda layer: (w_ffn_gate[layer].astype(bf16), w_ffn_up[layer].astype(bf16),
                           w_ffn_down[layer].astype(bf16))

    ws = _even_weights(w_in_even[0])
    conv_w = conv_w_delta[0]
    alog = _lane_param(delta_A_log[0], A_HEADS)
    dtb = _lane_param(delta_dt_bias[0], A_HEADS)
    dng = delta_norm_g[0].reshape(1, A_DV)
    wo = w_out_even[0].astype(bf16)
    wo_parts = [wo[:A_WIDTH], wo[A_WIDTH:]]

    qkv_p, gate_p, qb_p, rows_p, win_p, small_p, rows_planes_p = _proj(xp, ws, planes_of=3)
    r3 = lambda a: a.reshape(bp, t, -1)
    o_a_p, ds_p = _delta(r3(qkv_p), r3(small_p), r3(gate_p), jnp.zeros((bp, 8, A_CONV_CH), f32), conv_w,
                         alog, dtb, dng, jnp.zeros((bp, A_HEADS, A_DK, A_DV), f32), t_valid=t)
    o_b_p = _nsa_prompt(r3(qb_p), r3(rows_p), r3(win_p), r3(small_p))
    xp = _outproj_ln([o_a_p.reshape(bp * t, -1), o_b_p.reshape(bp * t, -1)], wo_parts, xp, *ln(0, 0))
    xp = _ffn_ln(xp, *ffn_w(0), *ln(0, 1))

    qkv_s, gate_s, qb_s, rows_s, win_s, small_s = _proj(xs, ws)
    conv_s = state_delta_conv[0]
    o_a_s, ds_s = _delta(_pad_rows(qkv_s, bs), _pad_rows(small_s, bs), _pad_rows(gate_s, bs),
                         jnp.pad(conv_s, ((0, 0), (8 - (CONV_W - 1), 0), (0, 0))), conv_w, alog, dtb, dng,
                         state_delta_S[0], t_valid=1)
    win_buf = state_nsa_win[0]
    wb = win_buf.shape[1]
    r1 = lambda a: a.reshape(bs, 1, -1)
    o_b_s = _nsa_decode(page_table, r1(qb_s), r1(rows_s), r1(win_s), r1(small_s), win_buf.reshape(bs, wb, -1),
                        cache_nsa_kv[:, 0].transpose(0, 2, 3, 4, 1))
    xs = _outproj_ln([o_a_s[:, 0], o_b_s.reshape(bs, -1)], wo_parts, xs, *ln(0, 0))
    xs = _ffn_ln(xs, *ffn_w(0), *ln(0, 1))

    kvd = (B_KV, B_DH)
    nsa_rows_p = rows_planes_p.reshape((1, bp, t, 4) + kvd)
    nsa_win_p = r3(win_p)[:, -min(WINDOW, t):].reshape((1, bp, min(WINDOW, t), 2) + kvd)
    delta_conv_p = jnp.concatenate([jnp.zeros((bp, CONV_W - 1, A_CONV_CH), f32), r3(qkv_p)], axis=1)[:, -(CONV_W - 1):][None]
    nsa_rows_s = rows_s.reshape((1, bs, 1, 4) + kvd)
    win_cat = jnp.concatenate([win_buf, win_s.reshape((bs, 1, 2) + kvd)], axis=1)
    nsa_win_s = win_cat[:, -min(WINDOW, wb + 1):][None]
    delta_conv_s = jnp.concatenate([conv_s, qkv_s[:, None, :]], axis=1)[:, -(CONV_W - 1):][None]

    wq, wk, wv, wg1, wr = _odd_weights(w_in_odd[0])
    wg2 = jnp.pad(w_gla_gate2[0], ((0, LANES - C_LOWRANK), (0, 0))).astype(bf16)
    bg2 = b_gla_gate2[0].reshape(1, C_QK)
    gng = gla_norm_g[0].reshape(1, C_DV)
    wo1 = [w_out_odd[0].astype(bf16)]

    q_p, k_p, v_p, g1_p, rr_p = _proj(xp, [wq, wk, wv, wg1, wr])
    o_c_p, gs_p = _gla(r3(q_p), r3(k_p), r3(v_p), r3(rr_p), r3(g1_p), wg2, bg2, gng,
                       jnp.zeros((bp, C_HEADS, C_DK, C_DV), f32), t_valid=t)
    xp = _outproj_ln([o_c_p.reshape(bp * t, -1)], wo1, xp, *ln(1, 0))
    xp = _ffn_ln(xp, *ffn_w(1), *ln(1, 1))

    q_s, k_s, v_s, g1_s, rr_s = _proj(xs, [wq, wk, wv, wg1, wr])
    o_c_s, gs_s = _gla(_pad_rows(q_s, bs), _pad_rows(k_s, bs), _pad_rows(v_s, bs), _pad_rows(rr_s, bs),
                       _pad_rows(g1_s, bs), wg2, bg2, gng, state_gla_S[0], t_valid=1)
    xs = _outproj_ln([o_c_s[:, 0]], wo1, xs, *ln(1, 0))
    xs = _ffn_ln(xs, *ffn_w(1), *ln(1, 1))

    return (xp.reshape(bp, t, d), xs.reshape(bs, 1, d),
            nsa_rows_p, nsa_win_p, delta_conv_p, ds_p[None], gs_p[None],
            nsa_rows_s, nsa_win_s, delta_conv_s, ds_s[None], gs_s[None])
```

```python
import functools
import math

import jax
import jax.numpy as jnp
from jax import lax
from jax.experimental import pallas as pl
from jax.experimental.pallas import tpu as pltpu

f32 = jnp.float32
bf16 = jnp.bfloat16
HI = lax.Precision.HIGHEST

D_MODEL = 1024
DEPTH = 2
PAGE_SIZE = 128

A_HEADS = 4
A_DK = 128
A_DV = 128
A_QK = A_HEADS * A_DK
A_WIDTH = A_HEADS * A_DV
A_CONV_CH = 2 * A_QK + A_WIDTH
CONV_W = 4
DELTA_CHUNK = 64

B_HEADS = 8
B_KV = 2
B_HPG = B_HEADS // B_KV
B_DH = 64
B_WIDTH = B_HEADS * B_DH
N_BRANCH = 3
CMP_BLK = 32
SEL_BLK = 64
SEL_TOPK = 16
WINDOW = 512
B_SCALE = B_DH ** -0.5

C_HEADS = 4
C_DK = 128
C_DV = 256
C_QK = C_HEADS * C_DK
C_WIDTH = C_HEADS * C_DV
C_LOWRANK = 16
GLA_TAU = 16.0
GLA_CHUNK = 64
GLA_SUB = 16
GLA_PREP_CHUNKS = 2

D_FF = -(-(8 * D_MODEL) // (3 * 256)) * 256
ALPHA = (2 * DEPTH) ** 0.25
NEG = -1e30
FORCE = 1e9
LOG2E = 1.4426950408889634

LANES = 128
VMEM_LIMIT = 56 * 1024 * 1024


def _cparams(sem):
    return pltpu.CompilerParams(dimension_semantics=sem, vmem_limit_bytes=VMEM_LIMIT)


def _dot(a, b, precision=None):
    return jnp.dot(a, b, preferred_element_type=f32, precision=precision)


def _dot_nt(a, b, precision=None):
    return lax.dot_general(a, b, (((1,), (1,)), ((), ())), preferred_element_type=f32, precision=precision)


def _dot_tn(a, b, precision=None):
    return lax.dot_general(a, b, (((0,), (0,)), ((), ())), preferred_element_type=f32, precision=precision)


def _silu(x):
    return x * jax.nn.sigmoid(x)


def _layer_norm(x, g, b, eps=1e-5):
    mu = jnp.mean(x, -1, keepdims=True)
    xc = x - mu
    var = jnp.mean(xc * xc, -1, keepdims=True)
    return xc * lax.rsqrt(var + eps) * g + b


def _row_tile(m, pref):
    t = min(pref, m)
    assert m % t == 0
    return t


def _proj_kernel(x_ref, *refs, n, planes_of):
    xb = x_ref[...].astype(bf16)
    tm = x_ref.shape[0]
    for idx, (w_ref, o_ref) in enumerate(zip(refs[:n], refs[n:2 * n])):
        val = _dot(xb, w_ref[...])
        o_ref[...] = val
        if idx == planes_of:
            p_ref = refs[2 * n]
            n_planes = val.shape[1] // B_DH
            for j in range(n_planes):
                p_ref[pl.ds(j, tm, stride=n_planes), :] = val[:, j * B_DH:(j + 1) * B_DH]


def _proj(x, weights, tm=512, planes_of=None):
    m, k = x.shape
    tm = _row_tile(m, tm)
    in_specs = [pl.BlockSpec((tm, k), lambda i: (i, 0))]
    in_specs += [pl.BlockSpec(w.shape, lambda i: (0, 0)) for w in weights]
    out_specs = [pl.BlockSpec((tm, w.shape[1]), lambda i: (i, 0)) for w in weights]
    out_shape = [jax.ShapeDtypeStruct((m, w.shape[1]), f32) for w in weights]
    if planes_of is not None:
        n_planes = weights[planes_of].shape[1] // B_DH
        out_specs.append(pl.BlockSpec((tm * n_planes, B_DH), lambda i: (i, 0)))
        out_shape.append(jax.ShapeDtypeStruct((m * n_planes, B_DH), f32))
    return pl.pallas_call(
        functools.partial(_proj_kernel, n=len(weights), planes_of=planes_of),
        grid=(m // tm,), in_specs=in_specs, out_specs=out_specs, out_shape=out_shape,
        compiler_params=_cparams(("parallel",)), name="in_proj")(x, *weights)


FFN_COLS = 256


def _mix_ffn_kernel(*refs, n_parts):
    parts = refs[:n_parts]
    ws = refs[n_parts:2 * n_parts]
    x_ref, g0_ref, b0_ref, wg_ref, wu_ref, wd_ref, g1_ref, b1_ref, o_ref = refs[2 * n_parts:]
    y = None
    for p_ref, w_ref in zip(parts, ws):
        d = _dot(p_ref[...].astype(bf16), w_ref[...])
        y = d if y is None else y + d
    x = _layer_norm(ALPHA * x_ref[...] + y, g0_ref[...], b0_ref[...])
    xb = x.astype(bf16)
    acc = jnp.zeros(x.shape, f32)
    for c in range(0, D_FF, FFN_COLS):
        hg = _dot(xb, wg_ref[:, c:c + FFN_COLS])
        hu = _dot(xb, wu_ref[:, c:c + FFN_COLS])
        h = (_silu(hg) * hu).astype(bf16)
        acc = acc + _dot(h, wd_ref[c:c + FFN_COLS, :])
    o_ref[...] = _layer_norm(ALPHA * x + acc, g1_ref[...], b1_ref[...])


def _mix_ffn(parts, weights, x, ln0, ffn_w, ln1, tm=512):
    m, d = x.shape
    tm = _row_tile(m, tm)
    n = len(parts)
    row = lambda w: pl.BlockSpec((tm, w), lambda i: (i, 0))
    const = lambda a: pl.BlockSpec(a.shape, lambda i: (0, 0), pipeline_mode=pl.Buffered(1))
    in_specs = [row(p.shape[1]) for p in parts] + [const(w) for w in weights] + [row(d)]
    in_specs += [const(a) for a in (*ln0, *ffn_w, *ln1)]
    return pl.pallas_call(
        functools.partial(_mix_ffn_kernel, n_parts=n), grid=(m // tm,), in_specs=in_specs,
        out_specs=row(d), out_shape=jax.ShapeDtypeStruct((m, d), f32),
        compiler_params=_cparams(("parallel",)), name="mix_ffn")(*parts, *weights, x, *ln0, *ffn_w, *ln1)


DELTA_INV_BLK = 16


def _tri_masks(c):
    r = lax.broadcasted_iota(jnp.int32, (c, c), 0)
    s = lax.broadcasted_iota(jnp.int32, (c, c), 1)
    return r, s


def _split2(x):
    hi = x.astype(bf16)
    return hi, (x - hi.astype(f32)).astype(bf16)


def _split3(x):
    hi = x.astype(bf16)
    rest = x - hi.astype(f32)
    mid = rest.astype(bf16)
    return hi, mid, (rest - mid.astype(f32)).astype(bf16)


def _dot_x3(a, b):
    ah, al = _split2(a)
    bh, bl = _split2(b)
    return _dot(ah, bh) + (_dot(ah, bl) + _dot(al, bh))


def _dot_ones(a_ones, b):
    a16 = a_ones.astype(bf16)
    b1, b2, b3 = _split3(b)
    return _dot(a16, b1) + (_dot(a16, b2) + _dot(a16, b3))


def _unit_lower_inverses(mats, r, s):
    c = mats[0].shape[0]
    eye = (r == s).astype(f32)
    same_blk = (r // DELTA_INV_BLK) == (s // DELTA_INV_BLK)
    diag = [jnp.where(same_blk, a, 0.0) for a in mats]
    low = [a - d for a, d in zip(mats, diag)]
    pw = [-d for d in diag]
    p = [eye + x for x in pw]
    k = 2
    while k < DELTA_INV_BLK:
        pw = [_dot_x3(x, x) for x in pw]
        p = [pi + _dot_x3(pi, x) for pi, x in zip(p, pw)]
        k *= 2
    pw = [_dot_x3(pi, lo) for pi, lo in zip(p, low)]
    q = [eye - x for x in pw]
    k = 2
    while k < c // DELTA_INV_BLK:
        pw = [_dot_x3(x, x) for x in pw]
        q = [qi + _dot_x3(qi, x) for qi, x in zip(q, pw)]
        k *= 2
    return [_dot_x3(qi, pi) for qi, pi in zip(q, p)]


DELTA_PREP_CHUNKS = 2


def _delta_kernel(qkv_ref, small_ref, gate_ref, cbuf_ref, cw_ref, alog_ref, dtb_ref, dng_ref, s0_ref,
                  o_ref, s_ref, xbuf, ybuf, tail, u_ref, w_ref, qg_ref, kd_ref, qk_ref, gl_ref,
                  *, tt, live_row):
    t = pl.program_id(1)
    c = DELTA_CHUNK
    n_chunks = tt // c
    cpi = math.gcd(DELTA_PREP_CHUNKS, n_chunks)

    @pl.when(t == 0)
    def _():
        s_ref[...] = s0_ref[...]
        tail[...] = cbuf_ref[...]

    xbuf[0:8, :] = tail[...]
    xbuf[8:8 + tt, :] = qkv_ref[...]
    tail[...] = xbuf[tt:tt + 8, :]

    def row0(chunk):
        return chunk * c if isinstance(chunk, int) else pl.multiple_of(chunk * c, c)

    def conv_group(gi):
        for cc in range(cpi):
            r0 = row0(gi * cpi + cc)
            for col in range(0, A_CONV_CH, A_QK):
                cols = slice(col, col + A_QK)
                x = xbuf[pl.ds(r0, c + 8), cols]
                first = 8 - (CONV_W - 1)
                y = x[first:first + c] * cw_ref[0:1, cols]
                for j in range(1, CONV_W):
                    y = y + x[first + j:first + j + c] * cw_ref[j:j + 1, cols]
                ybuf[pl.ds(r0, c), cols] = _silu(y)

    r, s = _tri_masks(c)
    causal = r >= s
    strict = r > s
    lower_ones = causal.astype(f32)
    upper4 =jnp.concatenate([(r <= s).astype(f32)] * A_HEADS, axis=1)
    all_ones = jnp.ones((c, c), f32)
    alog = alog_ref[...]
    dtb = dtb_ref[...]
    dng = dng_ref[...]

    def prep_group(gi):
        items = []
        for cc in range(cpi):
            r0 = row0(gi * cpi + cc)
            sm = small_ref[pl.ds(r0, c), :]
            beta_all = jax.nn.sigmoid(sm)
            g_all = -jnp.exp(alog) * jax.nn.softplus(sm + dtb)
            if live_row is not None:
                live = lax.broadcasted_iota(jnp.int32, (c, 1), 0) == live_row
                beta_all = jnp.where(live, beta_all, 0.0)
                g_all = jnp.where(live, g_all, 0.0)
            gam_c_all = _dot_ones(lower_ones, g_all)
            g_rows = jnp.concatenate(
                [jnp.broadcast_to(g_all[:, A_HEADS + h:A_HEADS + h + 1], (c, c)) for h in range(A_HEADS)], axis=1)
            gam_r_all = _dot_ones(all_ones, g_rows * upper4)
            for h in range(A_HEADS):
                qh = ybuf[pl.ds(r0, c), h * A_DK:(h + 1) * A_DK]
                kh = ybuf[pl.ds(r0, c), A_QK + h * A_DK:A_QK + (h + 1) * A_DK]
                vh = ybuf[pl.ds(r0, c), 2 * A_QK + h * A_DV:2 * A_QK + (h + 1) * A_DV]
                qh = qh * lax.rsqrt(jnp.sum(qh * qh, -1, keepdims=True) + 1e-6) * (A_DK ** -0.5)
                kh = kh * lax.rsqrt(jnp.sum(kh * kh, -1, keepdims=True) + 1e-6)
                if live_row is not None:
                    qh = jnp.where(live, qh, 0.0)
                    kh = jnp.where(live, kh, 0.0)
                    vh = jnp.where(live, vh, 0.0)
                beta = beta_all[:, h:h + 1]
                gam_c = jnp.broadcast_to(gam_c_all[:, A_HEADS + h:A_HEADS + h + 1], (c, A_DK))
                diff = gam_c[:, :c] - gam_r_all[:, h * c:(h + 1) * c]
                ldec = jnp.where(causal, jnp.exp(jnp.where(causal, diff, 0.0)), 0.0)
                items.append(dict(r0=r0, h=h, q=qh, k=kh, v=vh, beta=beta, gam=gam_c, ldec=ldec, kb=kh * beta))
        amats = [jnp.where(strict, _dot_nt(it["kb"], it["k"]) * it["ldec"], 0.0) for it in items]
        tms = _unit_lower_inverses(amats, r, s)
        for it, tm in zip(items, tms):
            r0, h = it["r0"], it["h"]
            cols = slice(h * A_DK, (h + 1) * A_DK)
            eg = jnp.exp(it["gam"])
            g_last = it["gam"][c - 1:c, :]
            u_ref[pl.ds(r0, c), cols] = _dot(tm, it["v"] * it["beta"])
            w_ref[pl.ds(r0, c), cols] = _dot(tm, it["kb"] * eg)
            qg_ref[pl.ds(r0, c), cols] = it["q"] * eg
            kd_ref[pl.ds(r0, c), cols] = it["k"] * jnp.exp(g_last - it["gam"])
            qk_ref[pl.ds(r0, c), h * c:(h + 1) * c] = _dot_nt(it["q"], it["k"]) * it["ldec"]
            gl_ref[pl.ds(r0, 1), cols] = jnp.exp(g_last)

    def scan_group(gi):
        for cc in range(cpi):
            chunk = gi * cpi + cc
            r0 = row0(chunk)
            for h in range(A_HEADS):
                cols = slice(h * A_DK, (h + 1) * A_DK)
                sidx = (h,) if live_row is None else (chunk, h)
                st = s_ref[sidx]
                v_new = u_ref[pl.ds(r0, c), cols] - _dot(w_ref[pl.ds(r0, c), cols], st)
                o = _dot(qg_ref[pl.ds(r0, c), cols], st) + _dot(qk_ref[pl.ds(r0, c), h * c:(h + 1) * c], v_new)
                s_ref[sidx] = st * gl_ref[pl.ds(r0, 1), cols] + _dot_tn(kd_ref[pl.ds(r0, c), cols], v_new)
                o = o * lax.rsqrt(jnp.mean(o * o, -1, keepdims=True) + 1e-6) * dng
                gt = gate_ref[pl.ds(r0, c), h * A_DV:(h + 1) * A_DV]
                o_ref[pl.ds(r0, c), h * A_DV:(h + 1) * A_DV] = o * _silu(gt)

    n_groups = n_chunks // cpi
    conv_group(0)
    if n_groups > 1:
        prep_group(0)
        conv_group(1)

        def steady(gi, carry):
            scan_group(gi - 1)
            prep_group(gi)
            conv_group(gi + 1)
            return carry

        lax.fori_loop(1, n_groups - 1, steady, 0)
        scan_group(n_groups - 2)
    prep_group(n_groups - 1)
    scan_group(n_groups - 1)


def _delta(qkv, small, gate, conv_buf8, conv_w, alog, dtb, dng, s0, *, live_row=None, tt=1024):
    b, t_total, _ = qkv.shape
    tt = _row_tile(t_total, tt)
    nt = t_total // tt
    assert live_row is None or (nt == 1 and s0.shape[1] == tt // DELTA_CHUNK)
    row = lambda w: pl.BlockSpec((None, tt, w), lambda i, j: (i, j, 0))
    full2 = lambda a: pl.BlockSpec(a.shape, lambda i, j: (0, 0))
    st_spec = pl.BlockSpec((None,) + s0.shape[1:], lambda i, j: (i,) + (0,) * (s0.ndim - 1))
    kern = functools.partial(_delta_kernel, tt=tt, live_row=live_row)
    return pl.pallas_call(
        kern, grid=(b, nt),
        in_specs=[row(A_CONV_CH), row(LANES), row(A_WIDTH),
                  pl.BlockSpec((None, 8, A_CONV_CH), lambda i, j: (i, 0, 0)),
                  full2(conv_w), full2(alog), full2(dtb), full2(dng), st_spec],
        out_specs=[row(A_WIDTH), st_spec],
        out_shape=[jax.ShapeDtypeStruct((b, t_total, A_WIDTH), f32), jax.ShapeDtypeStruct(s0.shape, f32)],
        scratch_shapes=[pltpu.VMEM((tt + 8, A_CONV_CH), f32), pltpu.VMEM((tt, A_CONV_CH), f32),
                        pltpu.VMEM((8, A_CONV_CH), f32)]
        + [pltpu.VMEM((tt, A_QK), f32)] * 4
        + [pltpu.VMEM((tt, A_HEADS * DELTA_CHUNK), f32), pltpu.VMEM((tt, A_QK), f32)],
        compiler_params=_cparams(("parallel", "arbitrary")), name="delta_mixer",
    )(qkv, small, gate, conv_buf8, conv_w, alog, dtb, dng, s0)


def _gla_kernel(q_ref, k_ref, v_ref, r_ref, g1_ref, wg2_ref, bg2_ref, gng_ref, s0_ref,
                o_ref, s_ref, st_ref, la_ref, qe_ref, gl_ref, ds_ref, *, tt, live_row):
    t = pl.program_id(1)
    nt = pl.num_programs(1)
    c = GLA_CHUNK
    heads = range(C_HEADS)
    kcols = [slice(h * C_DK, (h + 1) * C_DK) for h in heads]
    vcols = [slice(h * C_DV, (h + 1) * C_DV) for h in heads]
    states = [(h,) for h in heads] if live_row is None else [(cs, h) for cs in range(tt // c) for h in heads]

    @pl.when(t == 0)
    def _():
        for sidx in states:
            st_ref[sidx] = s0_ref[sidx].T

    la = jax.nn.log_sigmoid(_dot(g1_ref[...].astype(bf16), wg2_ref[...]) + bg2_ref[...]) / GLA_TAU
    if live_row is not None:
        la = jnp.where(lax.broadcasted_iota(jnp.int32, (tt, 1), 0) % c == live_row, la, 0.0)
    la_ref[...] = la

    r, s = _tri_masks(c)
    lower_ones = (r >= s).astype(f32)
    rs = lax.broadcasted_iota(jnp.int32, (GLA_SUB, c), 0)
    ss = lax.broadcasted_iota(jnp.int32, (GLA_SUB, c), 1)
    krow = lax.broadcasted_iota(jnp.int32, (c, 1), 0)
    gng = gng_ref[...]

    n_chunks = tt // c
    cpi = math.gcd(GLA_PREP_CHUNKS, n_chunks)

    def row0(chunk):
        return chunk * c if isinstance(chunk, int) else pl.multiple_of(chunk * c, c)

    def prep_group(gi):
        items = []
        for cc in range(cpi):
            chunk = gi * cpi + cc
            r0 = row0(chunk)
            rows = pl.ds(r0, c)
            bc_all = _dot_ones(lower_ones, la_ref[rows, :])
            for h in heads:
                items.append(dict(chunk=chunk, r0=r0, rows=rows, h=h, q=q_ref[rows, kcols[h]] * (C_DK ** -0.5),
                                  k=k_ref[rows, kcols[h]], v=v_ref[rows, vcols[h]], bc=bc_all[:, kcols[h]]))
        for it in items:
            qe_ref[it["rows"], kcols[it["h"]]] = it["q"] * jnp.exp(it["bc"])
        for a in range(c // GLA_SUB):
            lo, hi = a * GLA_SUB, (a + 1) * GLA_SUB
            seen = krow < hi
            att = []
            for it in items:
                bref = it["bc"][lo - 1:lo, :] if a > 0 else jnp.zeros((1, C_DK), f32)
                qa = it["q"][lo:hi, :] * jnp.exp(it["bc"][lo:hi, :] - bref)
                ka = jnp.where(seen, it["k"] * jnp.exp(jnp.where(seen, bref - it["bc"], 0.0)), 0.0)
                att.append(jnp.where(rs + lo >= ss, _dot_nt(qa, ka), 0.0))
            for it, at in zip(items, att):
                o_ref[pl.ds(it["r0"] + lo, GLA_SUB), vcols[it["h"]]] = _dot(at, it["v"])
        for it in items:
            b_last = it["bc"][c - 1:c, :]
            gl_ref[pl.ds(it["r0"], 1), kcols[it["h"]]] = jnp.exp(b_last)
            ds_ref[it["chunk"], it["h"]] = _dot_tn(it["v"], it["k"] * jnp.exp(b_last - it["bc"]))

    def scan_group(gi):
        for cc in range(cpi):
            chunk = gi * cpi + cc
            r0 = row0(chunk)
            rows = pl.ds(r0, c)
            for h in heads:
                sidx = (h,) if live_row is None else (chunk, h)
                st = st_ref[sidx]
                o = _dot_nt(qe_ref[rows, kcols[h]], st) + o_ref[rows, vcols[h]]
                st_ref[sidx] = st * gl_ref[pl.ds(r0, 1), kcols[h]] + ds_ref[chunk, h]
                o = o * lax.rsqrt(jnp.mean(o * o, -1, keepdims=True) + 1e-6) * gng
                o_ref[rows, vcols[h]] = o * _silu(r_ref[rows, vcols[h]])

    n_groups = n_chunks // cpi
    prep_group(0)

    def steady(gi, carry):
        scan_group(gi - 1)
        prep_group(gi)
        return carry

    lax.fori_loop(1, n_groups, steady, 0)
    scan_group(n_groups - 1)

    @pl.when(t == nt - 1)
    def _():
        for sidx in states:
            s_ref[sidx] = st_ref[sidx].T


def _gla(q, k, v, r, g1, wg2, bg2, gng, s0, *, live_row=None, tt=512):
    b, t_total, _ = q.shape
    tt = _row_tile(t_total, tt)
    nt = t_total // tt
    assert live_row is None or (nt == 1 and s0.shape[1] == tt // GLA_CHUNK)
    row = lambda w: pl.BlockSpec((None, tt, w), lambda i, j: (i, j, 0))
    full2 = lambda a: pl.BlockSpec(a.shape, lambda i, j: (0, 0))
    st_spec = pl.BlockSpec((None,) + s0.shape[1:], lambda i, j: (i,) + (0,) * (s0.ndim - 1))
    kern = functools.partial(_gla_kernel, tt=tt, live_row=live_row)
    return pl.pallas_call(
        kern, grid=(b, nt),
        in_specs=[row(C_QK), row(C_QK), row(C_WIDTH), row(C_WIDTH), row(LANES),
                  full2(wg2), full2(bg2), full2(gng), st_spec],
        out_specs=[row(C_WIDTH), st_spec],
        out_shape=[jax.ShapeDtypeStruct((b, t_total, C_WIDTH), f32), jax.ShapeDtypeStruct(s0.shape, f32)],
        scratch_shapes=[pltpu.VMEM(s0.shape[1:-2] + (C_DV, C_DK), f32), pltpu.VMEM((tt, C_QK), f32),
                        pltpu.VMEM((tt, C_QK), f32), pltpu.VMEM((tt, C_QK), f32),
                        pltpu.VMEM((tt // GLA_CHUNK, C_HEADS, C_DV, C_DK), f32)],
        compiler_params=_cparams(("parallel", "arbitrary")), name="gla_mixer",
    )(q, k, v, r, g1, wg2, bg2, gng, s0)


GB_LANE0 = 2 * A_HEADS


def _masked_softmax(s, mask):
    sm = jnp.where(mask, s, NEG)
    e = jnp.exp(sm - jnp.max(sm, -1, keepdims=True))
    p = e / jnp.sum(e, -1, keepdims=True)
    return jnp.where(mask, p, 0.0)


def _topk_rank(score, n_cand):
    lane = lax.broadcasted_iota(jnp.int32, score.shape, score.ndim - 1)
    rank = jnp.zeros(score.shape, jnp.int32)
    for i in range(n_cand):
        si = score[..., i:i + 1]
        ahead = (si > score) | ((si == score) & (i < lane))
        rank = rank + ahead.astype(jnp.int32)
    return rank


def _block_scores(imp, qpos, n_blk):
    blk = lax.broadcasted_iota(jnp.int32, imp.shape, imp.ndim - 1)
    cur = qpos // SEL_BLK
    valid = blk * SEL_BLK <= qpos
    forced = (blk == 0) | (blk == cur) | (blk == cur - 1)
    return jnp.where(forced, FORCE, jnp.where(valid, imp, NEG))


def _split_cmp_index(col, half):
    return jnp.where(col < half, 2 * col, 2 * (col - half) + 1)


NSA_TQ = 128
NSA_QSUB = 2
NSA_TK = 512
NSA_SUB = 128


def _topk_mask_rows(score, k):
    n = score.shape[0]
    row = lax.broadcasted_iota(jnp.int32, score.shape, 0)
    rank = jnp.zeros(score.shape, jnp.int32)
    for i in range(n):
        si = score[i:i + 1, :]
        ahead = (si > score) | ((si == score) & (i < row))
        rank = rank + ahead.astype(jnp.int32)
    return (rank < k).astype(f32)


def _nsa_prompt_kernel(q_ref, rows_ref, win_ref, small_ref, o_ref, cb_ref, sel_ref, sc_ref, *, t_total):
    i = pl.program_id(1)
    tq = NSA_TQ
    tk = min(NSA_TK, t_total)
    tw = min(WINDOW + tq, t_total)
    n_sel = t_total // SEL_BLK
    half = n_sel
    hq = B_HPG * tq
    blk_per_tile = tk // SEL_BLK

    @pl.when(i == 0)
    def _():
        x = rows_ref[:, 0:2 * B_KV * B_DH].reshape(n_sel, SEL_BLK, 2 * B_KV * B_DH)
        cb_ref[0:half, :] = jnp.sum(x[:, :CMP_BLK, :], axis=1) * (1.0 / CMP_BLK)
        cb_ref[half:2 * half, :] = jnp.sum(x[:, CMP_BLK:, :], axis=1) * (1.0 / CMP_BLK)

    n_qsub = o_ref.shape[0] // tq
    chains = [(s, g) for s in range(n_qsub) for g in range(B_KV)]
    tile0 = [(i * n_qsub + s) * tq for s in range(n_qsub)]
    qpos_s = [t0 + lax.broadcasted_iota(jnp.int32, (1, tq), 1) for t0 in tile0]
    gates_t = jax.nn.sigmoid(small_ref[...].T)
    n_causal = (tile0[-1] + tq - 1) // tk + 1
    q4 = [jnp.concatenate([q_ref[s * tq:(s + 1) * tq, (g * B_HPG + h) * B_DH:(g * B_HPG + h + 1) * B_DH]
                           for h in range(B_HPG)], axis=0) * (B_SCALE * LOG2E)
          for s, g in chains]

    o_cmp = []
    for c, (s, g) in enumerate(chains):
        qpos = qpos_s[s]
        qpos4 = jnp.concatenate([qpos] * B_HPG, axis=1)
        kc = cb_ref[:, g * B_DH:(g + 1) * B_DH]
        vc = cb_ref[:, (B_KV + g) * B_DH:(B_KV + g + 1) * B_DH]
        st = _dot_nt(kc, q4[c])
        cidx = _split_cmp_index(lax.broadcasted_iota(jnp.int32, (2 * half, 1), 0), half)
        cmask = (cidx + 1) * CMP_BLK <= qpos4 + 1
        sm = jnp.where(cmask, st, NEG)
        e = jnp.exp2(sm - jnp.max(sm, axis=0, keepdims=True))
        p = jnp.where(cmask, e / jnp.sum(e, axis=0, keepdims=True), 0.0)
        o_cmp.append(_dot_tn(vc, p))
        imp = p[:, 0:tq]
        for h in range(1, B_HPG):
            imp = imp + p[:, h * tq:(h + 1) * tq]
        imp = imp[:half] + imp[half:]
        blk = lax.broadcasted_iota(jnp.int32, (n_sel, 1), 0)
        cur = qpos // SEL_BLK
        forced = (blk == 0) | (blk == cur) | (blk == cur - 1)
        score = jnp.where(forced, FORCE, jnp.where(blk * SEL_BLK <= qpos, imp, NEG))
        picked = (_topk_mask_rows(score, min(SEL_TOPK, n_sel)) > 0.5) & (blk * SEL_BLK <= qpos)
        sel_ref[c] = jnp.where(picked, 0.0, NEG)

    sub = min(NSA_SUB, tk)

    def attend(c, kv_ref, k0, n_keys, kl, vl, bias_fn, carry):
        m, acc = carry
        m_new = m
        for u in range(n_keys // sub):
            k = kv_ref[pl.ds(k0 + u * sub, sub), kl:kl + B_DH]
            bias = bias_fn(u)
            sc = _dot_nt(k, q4[c]) + jnp.concatenate([bias] * B_HPG, axis=1)
            sc_ref[c, u * sub:(u + 1) * sub, :] = sc
            m_new = jnp.maximum(m_new, jnp.max(sc, axis=0, keepdims=True))
        acc = jnp.exp2(m - m_new) * acc
        for u in range(n_keys // sub):
            pe = jnp.exp2(sc_ref[c, u * sub:(u + 1) * sub, :] - m_new)
            v1 = jnp.concatenate([kv_ref[pl.ds(k0 + u * sub, sub), vl:vl + B_DH], ones_blk], axis=1)
            acc = acc + _dot_tn(v1, pe)
        return m_new, acc

    ones_blk = jnp.ones((sub, B_DH), f32)
    init = (jnp.full((1, hq), NEG, f32), jnp.zeros((2 * B_DH, hq), f32))
    srow = lax.broadcasted_iota(jnp.int32, (sub, 1), 0)
    blk_per_sub = sub // SEL_BLK
    assert sub == tq
    own_block_bias = jnp.where(srow <= lax.broadcasted_iota(jnp.int32, (1, tq), 1), 0.0, NEG)

    def sel_step(j, carries):
        k0 = pl.multiple_of(j * tk, tk)
        out = []
        for c, (s, g) in enumerate(chains):
            blk_bias = sel_ref[c, pl.ds(pl.multiple_of(j * blk_per_tile, blk_per_tile), blk_per_tile), :]

            def bias_fn(u, blk_bias=blk_bias, s=s):
                rows = jnp.concatenate(
                    [jnp.broadcast_to(blk_bias[u * blk_per_sub + v:u * blk_per_sub + v + 1, :], (SEL_BLK, tq))
                     for v in range(blk_per_sub)], axis=0)
                return rows + jnp.where(k0 + u * sub == tile0[s], own_block_bias, 0.0)

            out.append(attend(c, rows_ref, k0, tk, (2 * B_KV + g) * B_DH, (3 * B_KV + g) * B_DH, bias_fn,
                              carries[c]))
        return tuple(out)

    sel_out = lax.fori_loop(0, n_causal, sel_step, (init,) * len(chains))
    w0 = [pl.multiple_of(jnp.maximum(t0 + tq - tw, 0), tq) for t0 in tile0]
    win_bias = []
    for s in range(n_qsub):
        per_sub = []
        for u in range(tw // sub):
            d = qpos_s[s] - (w0[s] + u * sub + srow)
            per_sub.append(jnp.where((d >= 0) & (d < WINDOW), 0.0, NEG))
        win_bias.append(per_sub)

    win_out = [attend(c, win_ref, w0[s], tw, g * B_DH, (B_KV + g) * B_DH, lambda u, s=s: win_bias[s][u], init)
               for c, (s, g) in enumerate(chains)]
    for s in range(n_qsub):
        out_rows = []
        for c, (cs, g) in enumerate(chains):
            if cs != s:
                continue
            acc_w, acc_s = win_out[c][1], sel_out[c][1]
            o_win = acc_w[:B_DH] / acc_w[B_DH:B_DH + 1]
            o_sel = acc_s[:B_DH] / acc_s[B_DH:B_DH + 1]
            gt = gates_t[:, s * tq:(s + 1) * tq]
            for h in range(B_HPG):
                c0 = GB_LANE0 + (g * B_HPG + h) * N_BRANCH
                ls = slice(h * tq, (h + 1) * tq)
                out_rows.append(gt[c0:c0 + 1, :] * o_cmp[c][:, ls] + gt[c0 + 1:c0 + 2, :] * o_sel[:, ls]
                                + gt[c0 + 2:c0 + 3, :] * o_win[:, ls])
        o_ref[s * tq:(s + 1) * tq, :] = jnp.concatenate(out_rows, axis=0).T


def _nsa_prompt(qb, rows, win, small):
    b, t_total, _ = qb.shape
    tk = min(NSA_TK, t_total)
    rows_per_step = NSA_TQ * NSA_QSUB
    assert t_total % tk == 0 and tk % NSA_TQ == 0 and tk % SEL_BLK == 0 and WINDOW % NSA_TQ == 0
    assert t_total % rows_per_step == 0
    nq = t_total // rows_per_step
    n_sel = t_total // SEL_BLK
    n_chains = NSA_QSUB * B_KV
    tile = lambda w: pl.BlockSpec((None, rows_per_step, w), lambda bi, i: (bi, i, 0))
    seq = lambda w: pl.BlockSpec((None, t_total, w), lambda bi, i: (bi, 0, 0))
    return pl.pallas_call(
        functools.partial(_nsa_prompt_kernel, t_total=t_total), grid=(b, nq),
        in_specs=[tile(B_WIDTH), seq(4 * B_KV * B_DH), seq(2 * B_KV * B_DH), tile(LANES)],
        out_specs=tile(B_WIDTH), out_shape=jax.ShapeDtypeStruct((b, t_total, B_WIDTH), f32),
        scratch_shapes=[pltpu.VMEM((2 * n_sel, 2 * B_KV * B_DH), f32), pltpu.VMEM((n_chains, n_sel, NSA_TQ), f32),
                        pltpu.VMEM((n_chains, max(tk, min(WINDOW + NSA_TQ, t_total)), B_HPG * NSA_TQ), f32)],
        compiler_params=_cparams(("parallel", "arbitrary")), name="nsa_prompt",
    )(qb, rows, win, small)


DECODE_POOL_PAGES = 8


def _nsa_decode_cmp_kernel(pt_ref, q_ref, wnew_ref, wbuf_ref, pool_ref, cache_ref, ocmp_ref, owin_ref, idx_ref,
                           pages, acc_ref, sem, *, n_pages, past):
    b = pl.program_id(0)
    nb = pl.num_programs(0)
    slot = b % 2
    n_hist_blk = past // SEL_BLK
    n_sel = -(-(past + 1) // SEL_BLK)
    half = n_hist_blk
    lanes_sel = 2 * half
    wb = wbuf_ref.shape[0]
    n_cmp_rows = 2 * B_KV * B_DH

    def page_copy(bb, p, sl):
        return pltpu.make_async_copy(cache_ref.at[pt_ref[bb, p], pl.ds(0, 2)], pages.at[sl, p], sem.at[sl])

    def fetch(bb, sl):
        for p in range(n_pages):
            page_copy(bb, p, sl).start()

    @pl.when(b == 0)
    def _():
        fetch(0, 0)

    @pl.when(b + 1 < nb)
    def _():
        fetch(b + 1, 1 - slot)

    for p in range(n_pages):
        page_copy(b, p, slot).wait()

    acc_ref[...] = jnp.zeros(acc_ref.shape, f32)
    ppc = math.gcd(n_pages, DECODE_POOL_PAGES)

    def pool_body(c, carry):
        x = jnp.concatenate([pages[slot, c * ppc + i].reshape(n_cmp_rows, PAGE_SIZE) for i in range(ppc)], axis=1)
        hi, lo = _split2(x)
        w = pool_ref[pl.ds(pl.multiple_of(c * ppc * PAGE_SIZE, ppc * PAGE_SIZE), ppc * PAGE_SIZE), :]
        acc_ref[...] += _dot(hi, w) + _dot(lo, w)
        return carry

    lax.fori_loop(0, n_pages // ppc, pool_body, 0)

    qpos = past
    row8 = lax.broadcasted_iota(jnp.int32, (B_HEADS, 1), 0)
    q8 = jnp.concatenate([q_ref[:, h * B_DH:(h + 1) * B_DH] for h in range(B_HEADS)], axis=0) * B_SCALE
    wnew = wnew_ref[...]
    lane = lax.broadcasted_iota(jnp.int32, (1, lanes_sel), 1)
    o_cmp, o_win, idx_rows = [], [], []
    for g in range(B_KV):
        in_g = (row8 // B_HPG) == g
        kc_t = acc_ref[g * B_DH:(g + 1) * B_DH, :]
        vc_t = acc_ref[(B_KV + g) * B_DH:(B_KV + g + 1) * B_DH, :]
        sc = _dot(q8, kc_t)
        cidx = _split_cmp_index(lane, half)
        p = _masked_softmax(sc, (cidx + 1) * CMP_BLK <= qpos + 1)
        o_cmp.append(_dot_nt(p, vc_t))
        imp = jnp.sum(jnp.where(in_g, p, 0.0), axis=0, keepdims=True)
        imp = imp[:, :half] + imp[:, half:]
        imp = jnp.concatenate([imp, jnp.zeros((1, lanes_sel - half), f32)], axis=1)
        rank = _topk_rank(_block_scores(imp, qpos, n_sel), n_sel)
        lane_f = lane.astype(f32)
        out_lane = lax.broadcasted_iota(jnp.int32, (1, LANES), 1)
        picked = jnp.zeros((1, LANES), f32)
        for r in range(min(SEL_TOPK, n_sel)):
            blk_r = jnp.sum(jnp.where((rank == r) & (lane < n_sel), lane_f, 0.0), axis=-1, keepdims=True)
            picked = picked + jnp.where(out_lane == r, blk_r, 0.0)
        idx_rows.append(picked)
        kw = wbuf_ref[:, g * B_DH:(g + 1) * B_DH]
        vw = wbuf_ref[:, (B_KV + g) * B_DH:(B_KV + g + 1) * B_DH]
        kw_new = wnew[:, g * B_DH:(g + 1) * B_DH]
        vw_new = wnew[:, (B_KV + g) * B_DH:(B_KV + g + 1) * B_DH]
        kpos = qpos - wb + lax.broadcasted_iota(jnp.int32, (1, wb), 1)
        d = qpos - kpos
        wmask = (d >= 0) & (d < WINDOW) & (kpos >= 0)
        sw = jnp.where(wmask, _dot_nt(q8, kw), NEG)
        swn = jnp.sum(q8 * kw_new, -1, keepdims=True)
        m = jnp.maximum(jnp.max(sw, -1, keepdims=True), swn)
        ew = jnp.where(wmask, jnp.exp(sw - m), 0.0)
        ewn = jnp.exp(swn - m)
        o_win.append((_dot(ew, vw) + ewn * vw_new) / (jnp.sum(ew, -1, keepdims=True) + ewn))
    lower = row8 < B_HPG
    ocmp_ref[...] = jnp.where(lower, o_cmp[0], o_cmp[1])
    owin_ref[...] = jnp.where(lower, o_win[0], o_win[1])
    idx_ref[...] = jnp.concatenate(idx_rows + [jnp.zeros((8 - B_KV, LANES), f32)], axis=0).astype(jnp.int32)


def _nsa_decode_sel_kernel(pt_ref, sel_ref, q_ref, rnew_ref, small_ref, ocmp_ref, owin_ref, cache_ref,
                           o_ref, blocks, sem, *, past):
    b = pl.program_id(0)
    nb = pl.num_programs(0)
    slot = b % 2
    n_hist_blk = past // SEL_BLK
    n_pick = sel_ref.shape[2]
    blk_per_page = PAGE_SIZE // SEL_BLK
    tok_lane = lax.broadcasted_iota(jnp.int32, (1, PAGE_SIZE), 1)

    def hist_block(bb, g, j):
        return jnp.minimum(sel_ref[bb, g, j], n_hist_blk - 1)

    def block_copy(bb, g, j, sl):
        page = pt_ref[bb, hist_block(bb, g, j) // blk_per_page]
        return pltpu.make_async_copy(cache_ref.at[page, pl.ds(2, 2), g], blocks.at[sl, g * n_pick + j], sem.at[sl])

    def fetch(bb, sl):
        for g in range(B_KV):
            for j in range(n_pick):
                block_copy(bb, g, j, sl).start()

    @pl.when(b == 0)
    def _():
        fetch(0, 0)

    @pl.when(b + 1 < nb)
    def _():
        fetch(b + 1, 1 - slot)

    for g in range(B_KV):
        for j in range(n_pick):
            block_copy(b, g, j, slot).wait()

    row8 = lax.broadcasted_iota(jnp.int32, (B_HEADS, 1), 0)
    q8 = jnp.concatenate([q_ref[:, h * B_DH:(h + 1) * B_DH] for h in range(B_HEADS)], axis=0) * B_SCALE
    rnew = rnew_ref[...]
    o_sel = []
    for g in range(B_KV):
        ks_t = jnp.concatenate([blocks[slot, g * n_pick + j, 0] for j in range(n_pick)], axis=1)
        vs_t = jnp.concatenate([blocks[slot, g * n_pick + j, 1] for j in range(n_pick)], axis=1)
        picks = [sel_ref[b, g, j] for j in range(n_pick)]
        live = jnp.concatenate(
            [(tok_lane // SEL_BLK == hist_block(b, g, j) % blk_per_page) & (picks[j] < n_hist_blk)
             for j in range(n_pick)], axis=1)
        new_live = functools.reduce(jnp.logical_or, [pk >= n_hist_blk for pk in picks])
        k_new = rnew[:, (2 * B_KV + g) * B_DH:(2 * B_KV + g + 1) * B_DH]
        v_new = rnew[:, (3 * B_KV + g) * B_DH:(3 * B_KV + g + 1) * B_DH]
        sh = jnp.where(live, _dot(q8, ks_t), NEG)
        sn = jnp.sum(q8 * k_new, -1, keepdims=True) + jnp.where(new_live, 0.0, NEG).astype(f32)
        m = jnp.maximum(jnp.max(sh, -1, keepdims=True), sn)
        eh = jnp.where(live, jnp.exp(sh - m), 0.0)
        en = jnp.where(new_live, jnp.exp(sn - m), 0.0)
        o_sel.append((_dot_nt(eh, vs_t) + en * v_new) / (jnp.sum(eh, -1, keepdims=True) + en))
    os_ = jnp.where(row8 < B_HPG, o_sel[0], o_sel[1])
    oc = ocmp_ref[...]
    ow = owin_ref[...]
    gates = jax.nn.sigmoid(small_ref[...])
    outs = []
    for hh in range(B_HEADS):
        c0 = GB_LANE0 + hh * N_BRANCH
        outs.append(gates[:, c0:c0 + 1] * oc[hh:hh + 1] + gates[:, c0 + 1:c0 + 2] * os_[hh:hh + 1]
                    + gates[:, c0 + 2:c0 + 3] * ow[hh:hh + 1])
    o_ref[...] = jnp.concatenate(outs, axis=1)


def _block_pool_matrix(past):
    blk = jnp.arange(past) // CMP_BLK
    col = (blk % 2) * (past // SEL_BLK) + blk // 2
    return jnp.where(col[:, None] == jnp.arange(2 * (past // SEL_BLK))[None, :], 1.0 / CMP_BLK, 0.0).astype(bf16)


def _nsa_decode(page_table, qb, rows_new, win_new, small, win_buf, cache_t):
    nb, n_pages = page_table.shape
    past = n_pages * PAGE_SIZE
    assert past % SEL_BLK == 0 and cache_t.shape[1:] == (4, B_KV, B_DH, PAGE_SIZE)
    n_blk = past // SEL_BLK
    n_pick = min(SEL_TOPK, n_blk + 1)
    pool = _block_pool_matrix(past)
    one1 = lambda shape: pl.BlockSpec((None,) + shape, lambda i, pt: (i, 0, 0))
    cmp_spec = pltpu.PrefetchScalarGridSpec(
        num_scalar_prefetch=1, grid=(nb,),
        in_specs=[one1((1, B_WIDTH)), one1((1, 2 * B_KV * B_DH)), one1(win_buf.shape[1:]),
                  pl.BlockSpec(pool.shape, lambda i, pt: (0, 0)), pl.BlockSpec(memory_space=pl.ANY)],
        out_specs=[one1((B_HEADS, B_DH)), one1((B_HEADS, B_DH)), one1((8, LANES))],
        scratch_shapes=[pltpu.VMEM((2, n_pages, 2, B_KV, B_DH, PAGE_SIZE), f32),
                        pltpu.VMEM((2 * B_KV * B_DH, 2 * n_blk), f32), pltpu.SemaphoreType.DMA((2,))])
    o_cmp, o_win, idx = pl.pallas_call(
        functools.partial(_nsa_decode_cmp_kernel, n_pages=n_pages, past=past), grid_spec=cmp_spec,
        out_shape=[jax.ShapeDtypeStruct((nb, B_HEADS, B_DH), f32), jax.ShapeDtypeStruct((nb, B_HEADS, B_DH), f32),
                   jax.ShapeDtypeStruct((nb, 8, LANES), jnp.int32)],
        compiler_params=_cparams(("arbitrary",)), name="nsa_decode_cmp",
    )(page_table, qb, win_new, win_buf, pool, cache_t)
    picks = idx[:, :B_KV, :n_pick]
    one2 = lambda shape: pl.BlockSpec((None,) + shape, lambda i, pt, sel: (i, 0, 0))
    sel_spec = pltpu.PrefetchScalarGridSpec(
        num_scalar_prefetch=2, grid=(nb,),
        in_specs=[one2((1, B_WIDTH)), one2((1, 4 * B_KV * B_DH)), one2((1, LANES)),
                  one2((B_HEADS, B_DH)), one2((B_HEADS, B_DH)), pl.BlockSpec(memory_space=pl.ANY)],
        out_specs=one2((1, B_WIDTH)),
        scratch_shapes=[pltpu.VMEM((2, B_KV * n_pick, 2, B_DH, PAGE_SIZE), f32), pltpu.SemaphoreType.DMA((2,))])
    return pl.pallas_call(
        functools.partial(_nsa_decode_sel_kernel, past=past), grid_spec=sel_spec,
        out_shape=jax.ShapeDtypeStruct((nb, 1, B_WIDTH), f32),
        compiler_params=_cparams(("arbitrary",)), name="nsa_decode_sel",
    )(page_table, picks, qb, rows_new, small, o_cmp, o_win, cache_t)


SAMPLE_PAD = DELTA_CHUNK
SAMPLE_SEQS = 8
SAMPLE_ROW = 8


def _even_weights(w_in):
    o = 0
    cols = {}
    for name, n in (("qkv", A_CONV_CH), ("beta", A_HEADS), ("a", A_HEADS), ("gate", A_WIDTH), ("qb", B_WIDTH),
                    ("kv", N_BRANCH * 2 * B_KV * B_DH), ("gb", B_HEADS * N_BRANCH)):
        cols[name] = w_in[:, o:o + n]
        o += n
    n_rows = 4 * B_KV * B_DH
    small = jnp.concatenate([cols["beta"], cols["a"], cols["gb"]], axis=1)
    small = jnp.pad(small, ((0, 0), (0, LANES - small.shape[1])))
    ws = [cols["qkv"], cols["gate"], cols["qb"], cols["kv"][:, :n_rows], cols["kv"][:, n_rows:], small]
    return [w.astype(bf16) for w in ws]


def _odd_weights(w_in):
    o = 0
    ws = []
    for n in (C_QK, C_QK, C_WIDTH, C_LOWRANK, C_WIDTH):
        ws.append(w_in[:, o:o + n])
        o += n
    ws[3] = jnp.pad(ws[3], ((0, 0), (0, LANES - C_LOWRANK)))
    return [w.astype(bf16) for w in ws]


def _lane_param(v, lane0):
    return jnp.zeros((1, LANES), f32).at[0, lane0:lane0 + v.shape[0]].set(v.astype(f32))


def _pack_rows(a, nb, history=None):
    a = a.reshape(nb, 1, -1)
    before = SAMPLE_ROW
    if history is not None:
        a = jnp.concatenate([history, a], axis=1)
        before -= history.shape[1]
    a = jnp.pad(a, ((0, 0), (before, SAMPLE_PAD - before - a.shape[1]), (0, 0)))
    return a.reshape(nb // SAMPLE_SEQS, SAMPLE_SEQS * SAMPLE_PAD, -1)


def _unpack_rows(a, nb):
    return a.reshape(nb, SAMPLE_PAD, -1)[:, SAMPLE_ROW]


def _pack_state(s, nb):
    return s.reshape((nb // SAMPLE_SEQS, SAMPLE_SEQS) + s.shape[1:])


def kernel(x_prompt, x_sample, cache_nsa_kv, state_nsa_win, state_delta_conv, state_delta_S, state_gla_S,
           page_table, w_in_even, conv_w_delta, delta_A_log, delta_dt_bias, delta_norm_g, w_out_even,
           w_in_odd, w_gla_gate2, b_gla_gate2, gla_norm_g, w_out_odd, w_ffn_gate, w_ffn_up, w_ffn_down,
           ln_g, ln_b):
    bp, t, d = x_prompt.shape
    bs = x_sample.shape[0]
    assert x_sample.shape[1] == 1 and w_in_even.shape[0] == 1 and w_in_odd.shape[0] == 1
    n_pages = page_table.shape[1]
    past = n_pages * PAGE_SIZE
    xp = x_prompt.reshape(bp * t, d)
    xs = x_sample.reshape(bs, d)
    ln = lambda layer, j: (ln_g[layer, j].reshape(1, d), ln_b[layer, j].reshape(1, d))
    ffn_w = lambda layer: (w_ffn_gate[layer].astype(bf16), w_ffn_up[layer].astype(bf16),
                           w_ffn_down[layer].astype(bf16))

    ws = _even_weights(w_in_even[0])
    conv_w = conv_w_delta[0]
    alog = _lane_param(delta_A_log[0], A_HEADS)
    dtb = _lane_param(delta_dt_bias[0], A_HEADS)
    dng = delta_norm_g[0].reshape(1, A_DV)
    wo = w_out_even[0].astype(bf16)
    wo_parts = [wo[:A_WIDTH], wo[A_WIDTH:]]

    qkv_p, gate_p, qb_p, rows_p, win_p, small_p, rows_planes_p = _proj(xp, ws, planes_of=3)
    r3 = lambda a: a.reshape(bp, t, -1)
    o_a_p, ds_p = _delta(r3(qkv_p), r3(small_p), r3(gate_p), jnp.zeros((bp, 8, A_CONV_CH), f32), conv_w,
                         alog, dtb, dng, jnp.zeros((bp, A_HEADS, A_DK, A_DV), f32))
    o_b_p = _nsa_prompt(r3(qb_p), r3(rows_p), r3(win_p), r3(small_p))
    xp = _mix_ffn([o_a_p.reshape(bp * t, -1), o_b_p.reshape(bp * t, -1)], wo_parts, xp, ln(0, 0), ffn_w(0), ln(0, 1))

    qkv_s, gate_s, qb_s, rows_s, win_s, small_s = _proj(xs, ws)
    conv_s = state_delta_conv[0]
    assert bs % SAMPLE_SEQS == 0
    o_a_s, ds_s = _delta(_pack_rows(qkv_s, bs, history=conv_s), _pack_rows(small_s, bs), _pack_rows(gate_s, bs),
                         jnp.zeros((bs // SAMPLE_SEQS, 8, A_CONV_CH), f32), conv_w, alog, dtb, dng,
                         _pack_state(state_delta_S[0], bs), live_row=SAMPLE_ROW)
    o_a_s = _unpack_rows(o_a_s, bs)
    ds_s = ds_s.reshape(state_delta_S[0].shape)
    win_buf = state_nsa_win[0]
    wb = win_buf.shape[1]
    r1 = lambda a: a.reshape(bs, 1, -1)
    o_b_s = _nsa_decode(page_table, r1(qb_s), r1(rows_s), r1(win_s), r1(small_s), win_buf.reshape(bs, wb, -1),
                        cache_nsa_kv[:, 0].transpose(0, 2, 3, 4, 1))
    xs = _mix_ffn([o_a_s, o_b_s.reshape(bs, -1)], wo_parts, xs, ln(0, 0), ffn_w(0), ln(0, 1))

    kvd = (B_KV, B_DH)
    nsa_rows_p = rows_planes_p.reshape((1, bp, t, 4) + kvd)
    nsa_win_p = r3(win_p)[:, -min(WINDOW, t):].reshape((1, bp, min(WINDOW, t), 2) + kvd)
    delta_conv_p = jnp.concatenate([jnp.zeros((bp, CONV_W - 1, A_CONV_CH), f32), r3(qkv_p)], axis=1)[:, -(CONV_W - 1):][None]
    nsa_rows_s = rows_s.reshape((1, bs, 1, 4) + kvd)
    win_cat = jnp.concatenate([win_buf, win_s.reshape((bs, 1, 2) + kvd)], axis=1)
    nsa_win_s = win_cat[:, -min(WINDOW, wb + 1):][None]
    delta_conv_s = jnp.concatenate([conv_s, qkv_s[:, None, :]], axis=1)[:, -(CONV_W - 1):][None]

    wq, wk, wv, wg1, wr = _odd_weights(w_in_odd[0])
    wg2 = jnp.pad(w_gla_gate2[0], ((0, LANES - C_LOWRANK), (0, 0))).astype(bf16)
    bg2 = b_gla_gate2[0].reshape(1, C_QK)
    gng = gla_norm_g[0].reshape(1, C_DV)
    wo1 = [w_out_odd[0].astype(bf16)]

    q_p, k_p, v_p, g1_p, rr_p = _proj(xp, [wq, wk, wv, wg1, wr])
    o_c_p, gs_p = _gla(r3(q_p), r3(k_p), r3(v_p), r3(rr_p), r3(g1_p), wg2, bg2, gng,
                       jnp.zeros((bp, C_HEADS, C_DK, C_DV), f32))
    xp = _mix_ffn([o_c_p.reshape(bp * t, -1)], wo1, xp, ln(1, 0), ffn_w(1), ln(1, 1))

    q_s, k_s, v_s, g1_s, rr_s = _proj(xs, [wq, wk, wv, wg1, wr])
    o_c_s, gs_s = _gla(_pack_rows(q_s, bs), _pack_rows(k_s, bs), _pack_rows(v_s, bs), _pack_rows(rr_s, bs),
                       _pack_rows(g1_s, bs), wg2, bg2, gng, _pack_state(state_gla_S[0], bs), live_row=SAMPLE_ROW)
    gs_s = gs_s.reshape(state_gla_S[0].shape)
    xs = _mix_ffn([_unpack_rows(o_c_s, bs)], wo1, xs, ln(1, 0), ffn_w(1), ln(1, 1))

    return (xp.reshape(bp, t, d), xs.reshape(bs, 1, d),
            nsa_rows_p, nsa_win_p, delta_conv_p, ds_p[None], gs_p[None],
            nsa_rows_s, nsa_win_s, delta_conv_s, ds_s[None], gs_s[None])
```

```python
import functools
import math

import jax
import jax.numpy as jnp
from jax import lax
from jax.experimental import pallas as pl
from jax.experimental.pallas import tpu as pltpu

f32 = jnp.float32
bf16 = jnp.bfloat16
HI = lax.Precision.HIGHEST

D_MODEL = 1024
DEPTH = 2
PAGE_SIZE = 128

A_HEADS = 4
A_DK = 128
A_DV = 128
A_QK = A_HEADS * A_DK
A_WIDTH = A_HEADS * A_DV
A_CONV_CH = 2 * A_QK + A_WIDTH
CONV_W = 4
DELTA_CHUNK = 64

B_HEADS = 8
B_KV = 2
B_HPG = B_HEADS // B_KV
B_DH = 64
B_WIDTH = B_HEADS * B_DH
N_BRANCH = 3
CMP_BLK = 32
SEL_BLK = 64
SEL_TOPK = 16
WINDOW = 512
B_SCALE = B_DH ** -0.5

C_HEADS = 4
C_DK = 128
C_DV = 256
C_QK = C_HEADS * C_DK
C_WIDTH = C_HEADS * C_DV
C_LOWRANK = 16
GLA_TAU = 16.0
GLA_CHUNK = 64
GLA_SUB = 16
GLA_PREP_CHUNKS = 4

D_FF = -(-(8 * D_MODEL) // (3 * 256)) * 256
ALPHA = (2 * DEPTH) ** 0.25
NEG = -1e30
FORCE = 1e9
LOG2E = 1.4426950408889634

LANES = 128
VMEM_LIMIT = 56 * 1024 * 1024


def _cparams(sem):
    return pltpu.CompilerParams(dimension_semantics=sem, vmem_limit_bytes=VMEM_LIMIT)


def _dot(a, b, precision=None):
    return jnp.dot(a, b, preferred_element_type=f32, precision=precision)


def _dot_nt(a, b, precision=None):
    return lax.dot_general(a, b, (((1,), (1,)), ((), ())), preferred_element_type=f32, precision=precision)


def _dot_tn(a, b, precision=None):
    return lax.dot_general(a, b, (((0,), (0,)), ((), ())), preferred_element_type=f32, precision=precision)


def _silu(x):
    return x * jax.nn.sigmoid(x)


def _layer_norm(x, g, b, eps=1e-5):
    mu = jnp.mean(x, -1, keepdims=True)
    xc = x - mu
    var = jnp.mean(xc * xc, -1, keepdims=True)
    return xc * lax.rsqrt(var + eps) * g + b


def _row_tile(m, pref):
    t = min(pref, m)
    assert m % t == 0
    return t


def _proj_kernel(x_ref, *refs, n, planes_of):
    xb = x_ref[...].astype(bf16)
    tm = x_ref.shape[0]
    for idx, (w_ref, o_ref) in enumerate(zip(refs[:n], refs[n:2 * n])):
        val = _dot(xb, w_ref[...])
        o_ref[...] = val
        if idx == planes_of:
            p_ref = refs[2 * n]
            n_planes = val.shape[1] // B_DH
            for j in range(n_planes):
                p_ref[pl.ds(j, tm, stride=n_planes), :] = val[:, j * B_DH:(j + 1) * B_DH]


def _proj(x, weights, tm=512, planes_of=None):
    m, k = x.shape
    tm = _row_tile(m, tm)
    in_specs = [pl.BlockSpec((tm, k), lambda i: (i, 0))]
    in_specs += [pl.BlockSpec(w.shape, lambda i: (0, 0)) for w in weights]
    out_specs = [pl.BlockSpec((tm, w.shape[1]), lambda i: (i, 0)) for w in weights]
    out_shape = [jax.ShapeDtypeStruct((m, w.shape[1]), f32) for w in weights]
    if planes_of is not None:
        n_planes = weights[planes_of].shape[1] // B_DH
        out_specs.append(pl.BlockSpec((tm * n_planes, B_DH), lambda i: (i, 0)))
        out_shape.append(jax.ShapeDtypeStruct((m * n_planes, B_DH), f32))
    return pl.pallas_call(
        functools.partial(_proj_kernel, n=len(weights), planes_of=planes_of),
        grid=(m // tm,), in_specs=in_specs, out_specs=out_specs, out_shape=out_shape,
        compiler_params=_cparams(("parallel",)), name="in_proj")(x, *weights)


FFN_COLS = 256


def _mix_ffn_kernel(*refs, n_parts):
    parts = refs[:n_parts]
    ws = refs[n_parts:2 * n_parts]
    x_ref, g0_ref, b0_ref, wg_ref, wu_ref, wd_ref, g1_ref, b1_ref, o_ref = refs[2 * n_parts:]
    y = None
    for p_ref, w_ref in zip(parts, ws):
        d = _dot(p_ref[...].astype(bf16), w_ref[...])
        y = d if y is None else y + d
    x = _layer_norm(ALPHA * x_ref[...] + y, g0_ref[...], b0_ref[...])
    xb = x.astype(bf16)
    acc = jnp.zeros(x.shape, f32)
    for c in range(0, D_FF, FFN_COLS):
        hg = _dot(xb, wg_ref[:, c:c + FFN_COLS])
        hu = _dot(xb, wu_ref[:, c:c + FFN_COLS])
        h = (_silu(hg) * hu).astype(bf16)
        acc = acc + _dot(h, wd_ref[c:c + FFN_COLS, :])
    o_ref[...] = _layer_norm(ALPHA * x + acc, g1_ref[...], b1_ref[...])


def _mix_ffn(parts, weights, x, ln0, ffn_w, ln1, tm=512):
    m, d = x.shape
    tm = _row_tile(m, tm)
    n = len(parts)
    row = lambda w: pl.BlockSpec((tm, w), lambda i: (i, 0))
    const = lambda a: pl.BlockSpec(a.shape, lambda i: (0, 0), pipeline_mode=pl.Buffered(1))
    in_specs = [row(p.shape[1]) for p in parts] + [const(w) for w in weights] + [row(d)]
    in_specs += [const(a) for a in (*ln0, *ffn_w, *ln1)]
    return pl.pallas_call(
        functools.partial(_mix_ffn_kernel, n_parts=n), grid=(m // tm,), in_specs=in_specs,
        out_specs=row(d), out_shape=jax.ShapeDtypeStruct((m, d), f32),
        compiler_params=_cparams(("parallel",)), name="mix_ffn")(*parts, *weights, x, *ln0, *ffn_w, *ln1)


DELTA_INV_BLK = 16


def _tri_masks(c):
    r = lax.broadcasted_iota(jnp.int32, (c, c), 0)
    s = lax.broadcasted_iota(jnp.int32, (c, c), 1)
    return r, s


def _split2(x):
    hi = x.astype(bf16)
    return hi, (x - hi.astype(f32)).astype(bf16)


def _split3(x):
    hi = x.astype(bf16)
    rest = x - hi.astype(f32)
    mid = rest.astype(bf16)
    return hi, mid, (rest - mid.astype(f32)).astype(bf16)


def _dot_x3(a, b):
    ah, al = _split2(a)
    bh, bl = _split2(b)
    return _dot(ah, bh) + (_dot(ah, bl) + _dot(al, bh))


def _dot_ones(a_ones, b):
    a16 = a_ones.astype(bf16)
    b1, b2, b3 = _split3(b)
    return _dot(a16, b1) + (_dot(a16, b2) + _dot(a16, b3))


def _unit_lower_inverses(mats, r, s):
    c = mats[0].shape[0]
    eye = (r == s).astype(f32)
    same_blk = (r // DELTA_INV_BLK) == (s // DELTA_INV_BLK)
    diag = [jnp.where(same_blk, a, 0.0) for a in mats]
    low = [a - d for a, d in zip(mats, diag)]
    pw = [-d for d in diag]
    p = [eye + x for x in pw]
    k = 2
    while k < DELTA_INV_BLK:
        pw = [_dot_x3(x, x) for x in pw]
        p = [pi + _dot_x3(pi, x) for pi, x in zip(p, pw)]
        k *= 2
    pw = [_dot_x3(pi, lo) for pi, lo in zip(p, low)]
    q = [eye - x for x in pw]
    k = 2
    while k < c // DELTA_INV_BLK:
        pw = [_dot_x3(x, x) for x in pw]
        q = [qi + _dot_x3(qi, x) for qi, x in zip(q, pw)]
        k *= 2
    return [_dot_x3(qi, pi) for qi, pi in zip(q, p)]


DELTA_PREP_CHUNKS = 2
DELTA_UNROLL_GROUPS = 4


def _delta_kernel(qkv_ref, small_ref, gate_ref, cbuf_ref, cw_ref, alog_ref, dtb_ref, dng_ref, s0_ref,
                  o_ref, s_ref, xbuf, ybuf, tail, u_ref, w_ref, qg_ref, kd_ref, qk_ref, gl_ref,
                  *, tt, live_row):
    t = pl.program_id(1)
    c = DELTA_CHUNK
    n_chunks = tt // c
    cpi = math.gcd(DELTA_PREP_CHUNKS, n_chunks)

    @pl.when(t == 0)
    def _():
        s_ref[...] = s0_ref[...]
        tail[...] = cbuf_ref[...]

    xbuf[0:8, :] = tail[...]
    xbuf[8:8 + tt, :] = qkv_ref[...]
    tail[...] = xbuf[tt:tt + 8, :]

    def row0(chunk):
        return chunk * c if isinstance(chunk, int) else pl.multiple_of(chunk * c, c)

    def conv_group(gi):
        for cc in range(cpi):
            r0 = row0(gi * cpi + cc)
            for col in range(0, A_CONV_CH, A_QK):
                cols = slice(col, col + A_QK)
                x = xbuf[pl.ds(r0, c + 8), cols]
                first = 8 - (CONV_W - 1)
                y = x[first:first + c] * cw_ref[0:1, cols]
                for j in range(1, CONV_W):
                    y = y + x[first + j:first + j + c] * cw_ref[j:j + 1, cols]
                ybuf[pl.ds(r0, c), cols] = _silu(y)

    r, s = _tri_masks(c)
    causal = r >= s
    strict = r > s
    lower_ones = causal.astype(f32)
    upper4 =jnp.concatenate([(r <= s).astype(f32)] * A_HEADS, axis=1)
    all_ones = jnp.ones((c, c), f32)
    alog = alog_ref[...]
    dtb = dtb_ref[...]
    dng = dng_ref[...]

    def prep_group(gi):
        items = []
        for cc in range(cpi):
            r0 = row0(gi * cpi + cc)
            sm = small_ref[pl.ds(r0, c), :]
            beta_all = jax.nn.sigmoid(sm)
            g_all = -jnp.exp(alog) * jax.nn.softplus(sm + dtb)
            if live_row is not None:
                live = lax.broadcasted_iota(jnp.int32, (c, 1), 0) == live_row
                beta_all = jnp.where(live, beta_all, 0.0)
                g_all = jnp.where(live, g_all, 0.0)
            gam_c_all = _dot_ones(lower_ones, g_all)
            g_rows = jnp.concatenate(
                [jnp.broadcast_to(g_all[:, A_HEADS + h:A_HEADS + h + 1], (c, c)) for h in range(A_HEADS)], axis=1)
            gam_r_all = _dot_ones(all_ones, g_rows * upper4)
            for h in range(A_HEADS):
                qh = ybuf[pl.ds(r0, c), h * A_DK:(h + 1) * A_DK]
                kh = ybuf[pl.ds(r0, c), A_QK + h * A_DK:A_QK + (h + 1) * A_DK]
                vh = ybuf[pl.ds(r0, c), 2 * A_QK + h * A_DV:2 * A_QK + (h + 1) * A_DV]
                qh = qh * lax.rsqrt(jnp.sum(qh * qh, -1, keepdims=True) + 1e-6) * (A_DK ** -0.5)
                kh = kh * lax.rsqrt(jnp.sum(kh * kh, -1, keepdims=True) + 1e-6)
                if live_row is not None:
                    qh = jnp.where(live, qh, 0.0)
                    kh = jnp.where(live, kh, 0.0)
                    vh = jnp.where(live, vh, 0.0)
                beta = beta_all[:, h:h + 1]
                gam_c = jnp.broadcast_to(gam_c_all[:, A_HEADS + h:A_HEADS + h + 1], (c, A_DK))
                diff = gam_c[:, :c] - gam_r_all[:, h * c:(h + 1) * c]
                ldec = jnp.where(causal, jnp.exp(jnp.where(causal, diff, 0.0)), 0.0)
                items.append(dict(r0=r0, h=h, q=qh, k=kh, v=vh, beta=beta, gam=gam_c, ldec=ldec, kb=kh * beta))
        amats = [jnp.where(strict, _dot_nt(it["kb"], it["k"]) * it["ldec"], 0.0) for it in items]
        tms = _unit_lower_inverses(amats, r, s)
        for it, tm in zip(items, tms):
            r0, h = it["r0"], it["h"]
            cols = slice(h * A_DK, (h + 1) * A_DK)
            eg = jnp.exp(it["gam"])
            g_last = it["gam"][c - 1:c, :]
            u_ref[pl.ds(r0, c), cols] = _dot(tm, it["v"] * it["beta"])
            w_ref[pl.ds(r0, c), cols] = _dot(tm, it["kb"] * eg)
            qg_ref[pl.ds(r0, c), cols] = it["q"] * eg
            kd_ref[pl.ds(r0, c), cols] = it["k"] * jnp.exp(g_last - it["gam"])
            qk_ref[pl.ds(r0, c), h * c:(h + 1) * c] = _dot_nt(it["q"], it["k"]) * it["ldec"]
            gl_ref[pl.ds(r0, 1), cols] = jnp.exp(g_last)

    def scan_group(gi):
        for cc in range(cpi):
            chunk = gi * cpi + cc
            r0 = row0(chunk)
            for h in range(A_HEADS):
                cols = slice(h * A_DK, (h + 1) * A_DK)
                sidx = (h,) if live_row is None else (chunk, h)
                st = s_ref[sidx]
                v_new = u_ref[pl.ds(r0, c), cols] - _dot(w_ref[pl.ds(r0, c), cols], st)
                o = _dot(qg_ref[pl.ds(r0, c), cols], st) + _dot(qk_ref[pl.ds(r0, c), h * c:(h + 1) * c], v_new)
                s_ref[sidx] = st * gl_ref[pl.ds(r0, 1), cols] + _dot_tn(kd_ref[pl.ds(r0, c), cols], v_new)
                o = o * lax.rsqrt(jnp.mean(o * o, -1, keepdims=True) + 1e-6) * dng
                gt = gate_ref[pl.ds(r0, c), h * A_DV:(h + 1) * A_DV]
                o_ref[pl.ds(r0, c), h * A_DV:(h + 1) * A_DV] = o * _silu(gt)

    n_groups = n_chunks // cpi
    conv_group(0)
    if n_groups > 1:
        prep_group(0)
        conv_group(1)

        def steady(gi, carry):
            scan_group(gi - 1)
            prep_group(gi)
            conv_group(gi + 1)
            return carry

        if n_groups <= DELTA_UNROLL_GROUPS:
            for gi in range(1, n_groups - 1):
                steady(gi, 0)
        else:
            lax.fori_loop(1, n_groups - 1, steady, 0)
        scan_group(n_groups - 2)
    prep_group(n_groups - 1)
    scan_group(n_groups - 1)


def _delta(qkv, small, gate, conv_buf8, conv_w, alog, dtb, dng, s0, *, live_row=None, tt=512):
    b, t_total, _ = qkv.shape
    tt = _row_tile(t_total, tt)
    nt = t_total // tt
    assert live_row is None or (nt == 1 and s0.shape[1] == tt // DELTA_CHUNK)
    row = lambda w: pl.BlockSpec((None, tt, w), lambda i, j: (i, j, 0))
    full2 = lambda a: pl.BlockSpec(a.shape, lambda i, j: (0, 0))
    st_spec = pl.BlockSpec((None,) + s0.shape[1:], lambda i, j: (i,) + (0,) * (s0.ndim - 1))
    kern = functools.partial(_delta_kernel, tt=tt, live_row=live_row)
    return pl.pallas_call(
        kern, grid=(b, nt),
        in_specs=[row(A_CONV_CH), row(LANES), row(A_WIDTH),
                  pl.BlockSpec((None, 8, A_CONV_CH), lambda i, j: (i, 0, 0)),
                  full2(conv_w), full2(alog), full2(dtb), full2(dng), st_spec],
        out_specs=[row(A_WIDTH), st_spec],
        out_shape=[jax.ShapeDtypeStruct((b, t_total, A_WIDTH), f32), jax.ShapeDtypeStruct(s0.shape, f32)],
        scratch_shapes=[pltpu.VMEM((tt + 8, A_CONV_CH), f32), pltpu.VMEM((tt, A_CONV_CH), f32),
                        pltpu.VMEM((8, A_CONV_CH), f32)]
        + [pltpu.VMEM((tt, A_QK), f32)] * 4
        + [pltpu.VMEM((tt, A_HEADS * DELTA_CHUNK), f32), pltpu.VMEM((tt, A_QK), f32)],
        compiler_params=_cparams(("parallel", "arbitrary")), name="delta_mixer",
    )(qkv, small, gate, conv_buf8, conv_w, alog, dtb, dng, s0)


def _gla_kernel(q_ref, k_ref, v_ref, r_ref, g1_ref, wg2_ref, bg2_ref, gng_ref, s0_ref,
                o_ref, s_ref, st_ref, la_ref, qe_ref, gl_ref, ds_ref, *, tt, live_row):
    t = pl.program_id(1)
    nt = pl.num_programs(1)
    c = GLA_CHUNK
    heads = range(C_HEADS)
    kcols = [slice(h * C_DK, (h + 1) * C_DK) for h in heads]
    vcols = [slice(h * C_DV, (h + 1) * C_DV) for h in heads]
    states = [(h,) for h in heads] if live_row is None else [(cs, h) for cs in range(tt // c) for h in heads]

    @pl.when(t == 0)
    def _():
        for sidx in states:
            st_ref[sidx] = s0_ref[sidx].T

    la = jax.nn.log_sigmoid(_dot(g1_ref[...].astype(bf16), wg2_ref[...]) + bg2_ref[...]) / GLA_TAU
    if live_row is not None:
        la = jnp.where(lax.broadcasted_iota(jnp.int32, (tt, 1), 0) % c == live_row, la, 0.0)
    la_ref[...] = la

    r, s = _tri_masks(c)
    lower_ones = (r >= s).astype(f32)
    rs = lax.broadcasted_iota(jnp.int32, (GLA_SUB, c), 0)
    ss = lax.broadcasted_iota(jnp.int32, (GLA_SUB, c), 1)
    krow = lax.broadcasted_iota(jnp.int32, (c, 1), 0)
    gng = gng_ref[...]

    n_chunks = tt // c
    cpi = math.gcd(GLA_PREP_CHUNKS, n_chunks)

    def row0(chunk):
        return chunk * c if isinstance(chunk, int) else pl.multiple_of(chunk * c, c)

    def prep_group(gi):
        items = []
        for cc in range(cpi):
            chunk = gi * cpi + cc
            r0 = row0(chunk)
            rows = pl.ds(r0, c)
            bc_all = _dot_ones(lower_ones, la_ref[rows, :])
            for h in heads:
                items.append(dict(chunk=chunk, r0=r0, rows=rows, h=h, q=q_ref[rows, kcols[h]] * (C_DK ** -0.5),
                                  k=k_ref[rows, kcols[h]], v=v_ref[rows, vcols[h]], bc=bc_all[:, kcols[h]]))
        for it in items:
            qe_ref[it["rows"], kcols[it["h"]]] = it["q"] * jnp.exp(it["bc"])
        for a in range(c // GLA_SUB):
            lo, hi = a * GLA_SUB, (a + 1) * GLA_SUB
            seen = krow < hi
            att = []
            for it in items:
                bref = it["bc"][lo - 1:lo, :] if a > 0 else jnp.zeros((1, C_DK), f32)
                qa = it["q"][lo:hi, :] * jnp.exp(it["bc"][lo:hi, :] - bref)
                ka = jnp.where(seen, it["k"] * jnp.exp(jnp.where(seen, bref - it["bc"], 0.0)), 0.0)
                att.append(jnp.where(rs + lo >= ss, _dot_nt(qa, ka), 0.0))
            for it, at in zip(items, att):
                o_ref[pl.ds(it["r0"] + lo, GLA_SUB), vcols[it["h"]]] = _dot(at, it["v"])
        for it in items:
            b_last = it["bc"][c - 1:c, :]
            gl_ref[pl.ds(it["r0"], 1), kcols[it["h"]]] = jnp.exp(b_last)
            ds_ref[it["chunk"], it["h"]] = _dot_tn(it["v"], it["k"] * jnp.exp(b_last - it["bc"]))

    def scan_group(gi):
        for cc in range(cpi):
            chunk = gi * cpi + cc
            r0 = row0(chunk)
            rows = pl.ds(r0, c)
            for h in heads:
                sidx = (h,) if live_row is None else (chunk, h)
                st = st_ref[sidx]
                o = _dot_nt(qe_ref[rows, kcols[h]], st) + o_ref[rows, vcols[h]]
                st_ref[sidx] = st * gl_ref[pl.ds(r0, 1), kcols[h]] + ds_ref[chunk, h]
                o = o * lax.rsqrt(jnp.mean(o * o, -1, keepdims=True) + 1e-6) * gng
                o_ref[rows, vcols[h]] = o * _silu(r_ref[rows, vcols[h]])

    n_groups = n_chunks // cpi
    prep_group(0)

    def steady(gi, carry):
        scan_group(gi - 1)
        prep_group(gi)
        return carry

    lax.fori_loop(1, n_groups, steady, 0)
    scan_group(n_groups - 1)

    @pl.when(t == nt - 1)
    def _():
        for sidx in states:
            s_ref[sidx] = st_ref[sidx].T


def _gla(q, k, v, r, g1, wg2, bg2, gng, s0, *, live_row=None, tt=512):
    b, t_total, _ = q.shape
    tt = _row_tile(t_total, tt)
    nt = t_total // tt
    assert live_row is None or (nt == 1 and s0.shape[1] == tt // GLA_CHUNK)
    row = lambda w: pl.BlockSpec((None, tt, w), lambda i, j: (i, j, 0))
    full2 = lambda a: pl.BlockSpec(a.shape, lambda i, j: (0, 0))
    st_spec = pl.BlockSpec((None,) + s0.shape[1:], lambda i, j: (i,) + (0,) * (s0.ndim - 1))
    kern = functools.partial(_gla_kernel, tt=tt, live_row=live_row)
    return pl.pallas_call(
        kern, grid=(b, nt),
        in_specs=[row(C_QK), row(C_QK), row(C_WIDTH), row(C_WIDTH), row(LANES),
                  full2(wg2), full2(bg2), full2(gng), st_spec],
        out_specs=[row(C_WIDTH), st_spec],
        out_shape=[jax.ShapeDtypeStruct((b, t_total, C_WIDTH), f32), jax.ShapeDtypeStruct(s0.shape, f32)],
        scratch_shapes=[pltpu.VMEM(s0.shape[1:-2] + (C_DV, C_DK), f32), pltpu.VMEM((tt, C_QK), f32),
                        pltpu.VMEM((tt, C_QK), f32), pltpu.VMEM((tt, C_QK), f32),
                        pltpu.VMEM((tt // GLA_CHUNK, C_HEADS, C_DV, C_DK), f32)],
        compiler_params=_cparams(("parallel", "arbitrary")), name="gla_mixer",
    )(q, k, v, r, g1, wg2, bg2, gng, s0)


GB_LANE0 = 2 * A_HEADS


def _masked_softmax(s, mask):
    sm = jnp.where(mask, s, NEG)
    e = jnp.exp(sm - jnp.max(sm, -1, keepdims=True))
    p = e / jnp.sum(e, -1, keepdims=True)
    return jnp.where(mask, p, 0.0)


def _topk_rank(score, n_cand):
    lane = lax.broadcasted_iota(jnp.int32, score.shape, score.ndim - 1)
    rank = jnp.zeros(score.shape, jnp.int32)
    for i in range(n_cand):
        si = score[..., i:i + 1]
        ahead = (si > score) | ((si == score) & (i < lane))
        rank = rank + ahead.astype(jnp.int32)
    return rank


def _block_scores(imp, qpos, n_blk):
    blk = lax.broadcasted_iota(jnp.int32, imp.shape, imp.ndim - 1)
    cur = qpos // SEL_BLK
    valid = blk * SEL_BLK <= qpos
    forced = (blk == 0) | (blk == cur) | (blk == cur - 1)
    return jnp.where(forced, FORCE, jnp.where(valid, imp, NEG))


def _split_cmp_index(col, half):
    return jnp.where(col < half, 2 * col, 2 * (col - half) + 1)


NSA_TQ = 128
NSA_QSUB = 2
NSA_TK = 512
NSA_SUB = 128


def _topk_mask_rows(score, k):
    n = score.shape[0]
    row = lax.broadcasted_iota(jnp.int32, score.shape, 0)
    rank = jnp.zeros(score.shape, jnp.int32)
    for i in range(n):
        si = score[i:i + 1, :]
        ahead = (si > score) | ((si == score) & (i < row))
        rank = rank + ahead.astype(jnp.int32)
    return (rank < k).astype(f32)


def _nsa_prompt_kernel(q_ref, rows_ref, win_ref, small_ref, o_ref, cb_ref, sel_ref, sc_ref, *, t_total):
    i = pl.program_id(1)
    tq = NSA_TQ
    tk = min(NSA_TK, t_total)
    tw = min(WINDOW + tq, t_total)
    n_sel = t_total // SEL_BLK
    half = n_sel
    hq = B_HPG * tq
    blk_per_tile = tk // SEL_BLK
    kv_w = B_KV * B_DH

    @pl.when(i == 0)
    def _():
        x = rows_ref[:, 0:2 * kv_w].reshape(n_sel, SEL_BLK, 2 * kv_w)
        cb_ref[0:half, :] = jnp.sum(x[:, :CMP_BLK, :], axis=1) * (1.0 / CMP_BLK)
        cb_ref[half:2 * half, :] = jnp.sum(x[:, CMP_BLK:, :], axis=1) * (1.0 / CMP_BLK)

    n_qsub = o_ref.shape[0] // tq
    chains = [(s, g) for s in range(n_qsub) for g in range(B_KV)]
    tile0 = [(i * n_qsub + s) * tq for s in range(n_qsub)]
    qpos_s = [t0 + lax.broadcasted_iota(jnp.int32, (1, tq), 1) for t0 in tile0]
    gates_t = jax.nn.sigmoid(small_ref[...].T)
    n_causal = (tile0[-1] + tq - 1) // tk + 1
    q4 = [jnp.concatenate([q_ref[s * tq:(s + 1) * tq, (g * B_HPG + h) * B_DH:(g * B_HPG + h + 1) * B_DH]
                           for h in range(B_HPG)], axis=0) * (B_SCALE * LOG2E)
          for s, g in chains]

    o_cmp = []
    for c, (s, g) in enumerate(chains):
        qpos = qpos_s[s]
        qpos4 = jnp.concatenate([qpos] * B_HPG, axis=1)
        kc = cb_ref[:, g * B_DH:(g + 1) * B_DH]
        vc = cb_ref[:, (B_KV + g) * B_DH:(B_KV + g + 1) * B_DH]
        st = _dot_nt(kc, q4[c])
        cidx = _split_cmp_index(lax.broadcasted_iota(jnp.int32, (2 * half, 1), 0), half)
        cmask = (cidx + 1) * CMP_BLK <= qpos4 + 1
        sm = jnp.where(cmask, st, NEG)
        e = jnp.exp2(sm - jnp.max(sm, axis=0, keepdims=True))
        p = jnp.where(cmask, e / jnp.sum(e, axis=0, keepdims=True), 0.0)
        o_cmp.append(_dot_tn(vc, p))
        imp = p[:, 0:tq]
        for h in range(1, B_HPG):
            imp = imp + p[:, h * tq:(h + 1) * tq]
        imp = imp[:half] + imp[half:]
        blk = lax.broadcasted_iota(jnp.int32, (n_sel, 1), 0)
        cur = qpos // SEL_BLK
        forced = (blk == 0) | (blk == cur) | (blk == cur - 1)
        score = jnp.where(forced, FORCE, jnp.where(blk * SEL_BLK <= qpos, imp, NEG))
        picked = (_topk_mask_rows(score, min(SEL_TOPK, n_sel)) > 0.5) & (blk * SEL_BLK <= qpos)
        sel_ref[c] = jnp.where(picked, 0.0, NEG)

    sub = min(NSA_SUB, tk)

    def attend(c, kv_ref, k0, n_keys, kl, vl, bias_fn, carry):
        m, acc = carry
        m_new = m
        for u in range(n_keys // sub):
            k = kv_ref[pl.ds(k0 + u * sub, sub), kl:kl + B_DH]
            bias = bias_fn(u)
            sc = _dot_nt(k, q4[c]) + jnp.concatenate([bias] * B_HPG, axis=1)
            sc_ref[c, u * sub:(u + 1) * sub, :] = sc
            m_new = jnp.maximum(m_new, jnp.max(sc, axis=0, keepdims=True))
        acc = jnp.exp2(m - m_new) * acc
        for u in range(n_keys // sub):
            pe = jnp.exp2(sc_ref[c, u * sub:(u + 1) * sub, :] - m_new)
            v1 = jnp.concatenate([kv_ref[pl.ds(k0 + u * sub, sub), vl:vl + B_DH], ones_blk], axis=1)
            acc = acc + _dot_tn(v1, pe)
        return m_new, acc

    ones_blk = jnp.ones((sub, B_DH), f32)
    init = (jnp.full((1, hq), NEG, f32), jnp.zeros((2 * B_DH, hq), f32))
    srow = lax.broadcasted_iota(jnp.int32, (sub, 1), 0)
    blk_per_sub = sub // SEL_BLK
    assert sub == tq
    own_block_bias = jnp.where(srow <= lax.broadcasted_iota(jnp.int32, (1, tq), 1), 0.0, NEG)

    def sel_step(j, carries):
        k0 = pl.multiple_of(j * tk, tk)
        out = []
        for c, (s, g) in enumerate(chains):
            blk_bias = sel_ref[c, pl.ds(pl.multiple_of(j * blk_per_tile, blk_per_tile), blk_per_tile), :]

            def bias_fn(u, blk_bias=blk_bias, s=s):
                rows = jnp.concatenate(
                    [jnp.broadcast_to(blk_bias[u * blk_per_sub + v:u * blk_per_sub + v + 1, :], (SEL_BLK, tq))
                     for v in range(blk_per_sub)], axis=0)
                return rows + jnp.where(k0 + u * sub == tile0[s], own_block_bias, 0.0)

            out.append(attend(c, rows_ref, k0, tk, (2 * B_KV + g) * B_DH, (3 * B_KV + g) * B_DH, bias_fn,
                              carries[c]))
        return tuple(out)

    sel_out = lax.fori_loop(0, n_causal, sel_step, (init,) * len(chains))
    w0 = [pl.multiple_of(jnp.maximum(t0 + tq - tw, 0), tq) for t0 in tile0]
    win_bias = []
    for s in range(n_qsub):
        per_sub = []
        for u in range(tw // sub):
            d = qpos_s[s] - (w0[s] + u * sub + srow)
            per_sub.append(jnp.where((d >= 0) & (d < WINDOW), 0.0, NEG))
        win_bias.append(per_sub)

    win_out = [attend(c, win_ref, w0[s], tw, g * B_DH, (B_KV + g) * B_DH, lambda u, s=s: win_bias[s][u], init)
               for c, (s, g) in enumerate(chains)]
    for s in range(n_qsub):
        out_rows = []
        for c, (cs, g) in enumerate(chains):
            if cs != s:
                continue
            acc_w, acc_s = win_out[c][1], sel_out[c][1]
            o_win = acc_w[:B_DH] / acc_w[B_DH:B_DH + 1]
            o_sel = acc_s[:B_DH] / acc_s[B_DH:B_DH + 1]
            gt = gates_t[:, s * tq:(s + 1) * tq]
            for h in range(B_HPG):
                c0 = GB_LANE0 + (g * B_HPG + h) * N_BRANCH
                ls = slice(h * tq, (h + 1) * tq)
                out_rows.append(gt[c0:c0 + 1, :] * o_cmp[c][:, ls] + gt[c0 + 1:c0 + 2, :] * o_sel[:, ls]
                                + gt[c0 + 2:c0 + 3, :] * o_win[:, ls])
        o_ref[s * tq:(s + 1) * tq, :] = jnp.concatenate(out_rows, axis=0).T


def _nsa_prompt(qb, rows, win, small):
    b, t_total, _ = qb.shape
    tk = min(NSA_TK, t_total)
    rows_per_step = NSA_TQ * NSA_QSUB
    assert t_total % tk == 0 and tk % NSA_TQ == 0 and tk % SEL_BLK == 0 and WINDOW % NSA_TQ == 0
    assert t_total % rows_per_step == 0
    nq = t_total // rows_per_step
    n_sel = t_total // SEL_BLK
    n_chains = NSA_QSUB * B_KV
    tile = lambda w: pl.BlockSpec((None, rows_per_step, w), lambda bi, i: (bi, i, 0))
    seq = lambda w: pl.BlockSpec((None, t_total, w), lambda bi, i: (bi, 0, 0))
    return pl.pallas_call(
        functools.partial(_nsa_prompt_kernel, t_total=t_total), grid=(b, nq),
        in_specs=[tile(B_WIDTH), seq(4 * B_KV * B_DH), seq(2 * B_KV * B_DH), tile(LANES)],
        out_specs=tile(B_WIDTH), out_shape=jax.ShapeDtypeStruct((b, t_total, B_WIDTH), f32),
        scratch_shapes=[pltpu.VMEM((2 * n_sel, 2 * B_KV * B_DH), f32), pltpu.VMEM((n_chains, n_sel, NSA_TQ), f32),
                        pltpu.VMEM((n_chains, max(tk, min(WINDOW + NSA_TQ, t_total)), B_HPG * NSA_TQ), f32)],
        compiler_params=_cparams(("parallel", "arbitrary")), name="nsa_prompt",
    )(qb, rows, win, small)


DECODE_POOL_PAGES = 8


def _nsa_decode_cmp_kernel(pt_ref, q_ref, wnew_ref, wbuf_ref, pool_ref, cache_ref, ocmp_ref, owin_ref, idx_ref,
                           pages, acc_ref, sem, *, n_pages, past):
    b = pl.program_id(0)
    nb = pl.num_programs(0)
    slot = b % 2
    n_hist_blk = past // SEL_BLK
    n_sel = -(-(past + 1) // SEL_BLK)
    half = n_hist_blk
    lanes_sel = 2 * half
    wb = wbuf_ref.shape[0]
    n_cmp_rows = 2 * B_KV * B_DH

    def page_copy(bb, p, sl):
        return pltpu.make_async_copy(cache_ref.at[pt_ref[bb, p], pl.ds(0, 2)], pages.at[sl, p], sem.at[sl])

    def fetch(bb, sl):
        for p in range(n_pages):
            page_copy(bb, p, sl).start()

    @pl.when(b == 0)
    def _():
        fetch(0, 0)

    @pl.when(b + 1 < nb)
    def _():
        fetch(b + 1, 1 - slot)

    for p in range(n_pages):
        page_copy(b, p, slot).wait()

    acc_ref[...] = jnp.zeros(acc_ref.shape, f32)
    ppc = math.gcd(n_pages, DECODE_POOL_PAGES)

    def pool_body(c, carry):
        x = jnp.concatenate([pages[slot, c * ppc + i].reshape(n_cmp_rows, PAGE_SIZE) for i in range(ppc)], axis=1)
        hi, lo = _split2(x)
        w = pool_ref[pl.ds(pl.multiple_of(c * ppc * PAGE_SIZE, ppc * PAGE_SIZE), ppc * PAGE_SIZE), :]
        acc_ref[...] += _dot(hi, w) + _dot(lo, w)
        return carry

    lax.fori_loop(0, n_pages // ppc, pool_body, 0)

    qpos = past
    row8 = lax.broadcasted_iota(jnp.int32, (B_HEADS, 1), 0)
    q8 = jnp.concatenate([q_ref[:, h * B_DH:(h + 1) * B_DH] for h in range(B_HEADS)], axis=0) * B_SCALE
    wnew = wnew_ref[...]
    lane = lax.broadcasted_iota(jnp.int32, (1, lanes_sel), 1)
    o_cmp, o_win, idx_rows = [], [], []
    for g in range(B_KV):
        in_g = (row8 // B_HPG) == g
        kc_t = acc_ref[g * B_DH:(g + 1) * B_DH, :]
        vc_t = acc_ref[(B_KV + g) * B_DH:(B_KV + g + 1) * B_DH, :]
        sc = _dot(q8, kc_t)
        cidx = _split_cmp_index(lane, half)
        p = _masked_softmax(sc, (cidx + 1) * CMP_BLK <= qpos + 1)
        o_cmp.append(_dot_nt(p, vc_t))
        imp = jnp.sum(jnp.where(in_g, p, 0.0), axis=0, keepdims=True)
        imp = imp[:, :half] + imp[:, half:]
        imp = jnp.concatenate([imp, jnp.zeros((1, lanes_sel - half), f32)], axis=1)
        rank = _topk_rank(_block_scores(imp, qpos, n_sel), n_sel)
        lane_f = lane.astype(f32)
        out_lane = lax.broadcasted_iota(jnp.int32, (1, LANES), 1)
        picked = jnp.zeros((1, LANES), f32)
        for r in range(min(SEL_TOPK, n_sel)):
            blk_r = jnp.sum(jnp.where((rank == r) & (lane < n_sel), lane_f, 0.0), axis=-1, keepdims=True)
            picked = picked + jnp.where(out_lane == r, blk_r, 0.0)
        idx_rows.append(picked)
        kw = wbuf_ref[:, g * B_DH:(g + 1) * B_DH]
        vw = wbuf_ref[:, (B_KV + g) * B_DH:(B_KV + g + 1) * B_DH]
        kw_new = wnew[:, g * B_DH:(g + 1) * B_DH]
        vw_new = wnew[:, (B_KV + g) * B_DH:(B_KV + g + 1) * B_DH]
        kpos = qpos - wb + lax.broadcasted_iota(jnp.int32, (1, wb), 1)
        d = qpos - kpos
        wmask = (d >= 0) & (d < WINDOW) & (kpos >= 0)
        sw = jnp.where(wmask, _dot_nt(q8, kw), NEG)
        swn = jnp.sum(q8 * kw_new, -1, keepdims=True)
        m = jnp.maximum(jnp.max(sw, -1, keepdims=True), swn)
        ew = jnp.where(wmask, jnp.exp(sw - m), 0.0)
        ewn = jnp.exp(swn - m)
        o_win.append((_dot(ew, vw) + ewn * vw_new) / (jnp.sum(ew, -1, keepdims=True) + ewn))
    lower = row8 < B_HPG
    ocmp_ref[...] = jnp.where(lower, o_cmp[0], o_cmp[1])
    owin_ref[...] = jnp.where(lower, o_win[0], o_win[1])
    idx_ref[...] = jnp.concatenate(idx_rows + [jnp.zeros((8 - B_KV, LANES), f32)], axis=0).astype(jnp.int32)


def _nsa_decode_sel_kernel(pt_ref, sel_ref, q_ref, rnew_ref, small_ref, ocmp_ref, owin_ref, cache_ref,
                           o_ref, blocks, sem, *, past):
    b = pl.program_id(0)
    nb = pl.num_programs(0)
    slot = b % 2
    n_hist_blk = past // SEL_BLK
    n_pick = sel_ref.shape[2]
    blk_per_page = PAGE_SIZE // SEL_BLK
    tok_lane = lax.broadcasted_iota(jnp.int32, (1, PAGE_SIZE), 1)

    def hist_block(bb, g, j):
        return jnp.minimum(sel_ref[bb, g, j], n_hist_blk - 1)

    def block_copy(bb, g, j, sl):
        page = pt_ref[bb, hist_block(bb, g, j) // blk_per_page]
        return pltpu.make_async_copy(cache_ref.at[page, pl.ds(2, 2), g], blocks.at[sl, g * n_pick + j], sem.at[sl])

    def fetch(bb, sl):
        for g in range(B_KV):
            for j in range(n_pick):
                block_copy(bb, g, j, sl).start()

    @pl.when(b == 0)
    def _():
        fetch(0, 0)

    @pl.when(b + 1 < nb)
    def _():
        fetch(b + 1, 1 - slot)

    for g in range(B_KV):
        for j in range(n_pick):
            block_copy(b, g, j, slot).wait()

    row8 = lax.broadcasted_iota(jnp.int32, (B_HEADS, 1), 0)
    q8 = jnp.concatenate([q_ref[:, h * B_DH:(h + 1) * B_DH] for h in range(B_HEADS)], axis=0) * B_SCALE
    rnew = rnew_ref[...]
    o_sel = []
    for g in range(B_KV):
        ks_t = jnp.concatenate([blocks[slot, g * n_pick + j, 0] for j in range(n_pick)], axis=1)
        vs_t = jnp.concatenate([blocks[slot, g * n_pick + j, 1] for j in range(n_pick)], axis=1)
        picks = [sel_ref[b, g, j] for j in range(n_pick)]
        live = jnp.concatenate(
            [(tok_lane // SEL_BLK == hist_block(b, g, j) % blk_per_page) & (picks[j] < n_hist_blk)
             for j in range(n_pick)], axis=1)
        new_live = functools.reduce(jnp.logical_or, [pk >= n_hist_blk for pk in picks])
        k_new = rnew[:, (2 * B_KV + g) * B_DH:(2 * B_KV + g + 1) * B_DH]
        v_new = rnew[:, (3 * B_KV + g) * B_DH:(3 * B_KV + g + 1) * B_DH]
        sh = jnp.where(live, _dot(q8, ks_t), NEG)
        sn = jnp.sum(q8 * k_new, -1, keepdims=True) + jnp.where(new_live, 0.0, NEG).astype(f32)
        m = jnp.maximum(jnp.max(sh, -1, keepdims=True), sn)
        eh = jnp.where(live, jnp.exp(sh - m), 0.0)
        en = jnp.where(new_live, jnp.exp(sn - m), 0.0)
        o_sel.append((_dot_nt(eh, vs_t) + en * v_new) / (jnp.sum(eh, -1, keepdims=True) + en))
    os_ = jnp.where(row8 < B_HPG, o_sel[0], o_sel[1])
    oc = ocmp_ref[...]
    ow = owin_ref[...]
    gates = jax.nn.sigmoid(small_ref[...])
    outs = []
    for hh in range(B_HEADS):
        c0 = GB_LANE0 + hh * N_BRANCH
        outs.append(gates[:, c0:c0 + 1] * oc[hh:hh + 1] + gates[:, c0 + 1:c0 + 2] * os_[hh:hh + 1]
                    + gates[:, c0 + 2:c0 + 3] * ow[hh:hh + 1])
    o_ref[...] = jnp.concatenate(outs, axis=1)


def _block_pool_matrix(past):
    blk = jnp.arange(past) // CMP_BLK
    col = (blk % 2) * (past // SEL_BLK) + blk // 2
    return jnp.where(col[:, None] == jnp.arange(2 * (past // SEL_BLK))[None, :], 1.0 / CMP_BLK, 0.0).astype(bf16)


def _nsa_decode(page_table, qb, rows_new, win_new, small, win_buf, cache_t):
    nb, n_pages = page_table.shape
    past = n_pages * PAGE_SIZE
    assert past % SEL_BLK == 0 and cache_t.shape[1:] == (4, B_KV, B_DH, PAGE_SIZE)
    n_blk = past // SEL_BLK
    n_pick = min(SEL_TOPK, n_blk + 1)
    pool = _block_pool_matrix(past)
    one1 = lambda shape: pl.BlockSpec((None,) + shape, lambda i, pt: (i, 0, 0))
    cmp_spec = pltpu.PrefetchScalarGridSpec(
        num_scalar_prefetch=1, grid=(nb,),
        in_specs=[one1((1, B_WIDTH)), one1((1, 2 * B_KV * B_DH)), one1(win_buf.shape[1:]),
                  pl.BlockSpec(pool.shape, lambda i, pt: (0, 0)), pl.BlockSpec(memory_space=pl.ANY)],
        out_specs=[one1((B_HEADS, B_DH)), one1((B_HEADS, B_DH)), one1((8, LANES))],
        scratch_shapes=[pltpu.VMEM((2, n_pages, 2, B_KV, B_DH, PAGE_SIZE), f32),
                        pltpu.VMEM((2 * B_KV * B_DH, 2 * n_blk), f32), pltpu.SemaphoreType.DMA((2,))])
    o_cmp, o_win, idx = pl.pallas_call(
        functools.partial(_nsa_decode_cmp_kernel, n_pages=n_pages, past=past), grid_spec=cmp_spec,
        out_shape=[jax.ShapeDtypeStruct((nb, B_HEADS, B_DH), f32), jax.ShapeDtypeStruct((nb, B_HEADS, B_DH), f32),
                   jax.ShapeDtypeStruct((nb, 8, LANES), jnp.int32)],
        compiler_params=_cparams(("arbitrary",)), name="nsa_decode_cmp",
    )(page_table, qb, win_new, win_buf, pool, cache_t)
    picks = idx[:, :B_KV, :n_pick]
    one2 = lambda shape: pl.BlockSpec((None,) + shape, lambda i, pt, sel: (i, 0, 0))
    sel_spec = pltpu.PrefetchScalarGridSpec(
        num_scalar_prefetch=2, grid=(nb,),
        in_specs=[one2((1, B_WIDTH)), one2((1, 4 * B_KV * B_DH)), one2((1, LANES)),
                  one2((B_HEADS, B_DH)), one2((B_HEADS, B_DH)), pl.BlockSpec(memory_space=pl.ANY)],
        out_specs=one2((1, B_WIDTH)),
        scratch_shapes=[pltpu.VMEM((2, B_KV * n_pick, 2, B_DH, PAGE_SIZE), f32), pltpu.SemaphoreType.DMA((2,))])
    return pl.pallas_call(
        functools.partial(_nsa_decode_sel_kernel, past=past), grid_spec=sel_spec,
        out_shape=jax.ShapeDtypeStruct((nb, 1, B_WIDTH), f32),
        compiler_params=_cparams(("arbitrary",)), name="nsa_decode_sel",
    )(page_table, picks, qb, rows_new, small, o_cmp, o_win, cache_t)


SAMPLE_PAD = DELTA_CHUNK
SAMPLE_SEQS = 8
SAMPLE_ROW = 8


def _even_weights(w_in):
    o = 0
    cols = {}
    for name, n in (("qkv", A_CONV_CH), ("beta", A_HEADS), ("a", A_HEADS), ("gate", A_WIDTH), ("qb", B_WIDTH),
                    ("kv", N_BRANCH * 2 * B_KV * B_DH), ("gb", B_HEADS * N_BRANCH)):
        cols[name] = w_in[:, o:o + n]
        o += n
    n_rows = 4 * B_KV * B_DH
    small = jnp.concatenate([cols["beta"], cols["a"], cols["gb"]], axis=1)
    small = jnp.pad(small, ((0, 0), (0, LANES - small.shape[1])))
    ws = [cols["qkv"], cols["gate"], cols["qb"], cols["kv"][:, :n_rows], cols["kv"][:, n_rows:], small]
    return [w.astype(bf16) for w in ws]


def _odd_weights(w_in):
    o = 0
    ws = []
    for n in (C_QK, C_QK, C_WIDTH, C_LOWRANK, C_WIDTH):
        ws.append(w_in[:, o:o + n])
        o += n
    ws[3] = jnp.pad(ws[3], ((0, 0), (0, LANES - C_LOWRANK)))
    return [w.astype(bf16) for w in ws]


def _lane_param(v, lane0):
    return jnp.zeros((1, LANES), f32).at[0, lane0:lane0 + v.shape[0]].set(v.astype(f32))


def _pack_rows(a, nb, history=None):
    a = a.reshape(nb, 1, -1)
    before = SAMPLE_ROW
    if history is not None:
        a = jnp.concatenate([history, a], axis=1)
        before -= history.shape[1]
    a = jnp.pad(a, ((0, 0), (before, SAMPLE_PAD - before - a.shape[1]), (0, 0)))
    return a.reshape(nb // SAMPLE_SEQS, SAMPLE_SEQS * SAMPLE_PAD, -1)


def _unpack_rows(a, nb):
    return a.reshape(nb, SAMPLE_PAD, -1)[:, SAMPLE_ROW]


def _pack_state(s, nb):
    return s.reshape((nb // SAMPLE_SEQS, SAMPLE_SEQS) + s.shape[1:])


def kernel(x_prompt, x_sample, cache_nsa_kv, state_nsa_win, state_delta_conv, state_delta_S, state_gla_S,
           page_table, w_in_even, conv_w_delta, delta_A_log, delta_dt_bias, delta_norm_g, w_out_even,
           w_in_odd, w_gla_gate2, b_gla_gate2, gla_norm_g, w_out_odd, w_ffn_gate, w_ffn_up, w_ffn_down,
           ln_g, ln_b):
    bp, t, d = x_prompt.shape
    bs = x_sample.shape[0]
    assert x_sample.shape[1] == 1 and w_in_even.shape[0] == 1 and w_in_odd.shape[0] == 1
    n_pages = page_table.shape[1]
    past = n_pages * PAGE_SIZE
    xp = x_prompt.reshape(bp * t, d)
    xs = x_sample.reshape(bs, d)
    ln = lambda layer, j: (ln_g[layer, j].reshape(1, d), ln_b[layer, j].reshape(1, d))
    ffn_w = lambda layer: (w_ffn_gate[layer].astype(bf16), w_ffn_up[layer].astype(bf16),
                           w_ffn_down[layer].astype(bf16))

    ws = _even_weights(w_in_even[0])
    conv_w = conv_w_delta[0]
    alog = _lane_param(delta_A_log[0], A_HEADS)
    dtb = _lane_param(delta_dt_bias[0], A_HEADS)
    dng = delta_norm_g[0].reshape(1, A_DV)
    wo = w_out_even[0].astype(bf16)
    wo_parts = [wo[:A_WIDTH], wo[A_WIDTH:]]

    qkv_p, gate_p, qb_p, rows_p, win_p, small_p, rows_planes_p = _proj(xp, ws, planes_of=3)
    r3 = lambda a: a.reshape(bp, t, -1)
    o_a_p, ds_p = _delta(r3(qkv_p), r3(small_p), r3(gate_p), jnp.zeros((bp, 8, A_CONV_CH), f32), conv_w,
                         alog, dtb, dng, jnp.zeros((bp, A_HEADS, A_DK, A_DV), f32))
    o_b_p = _nsa_prompt(r3(qb_p), r3(rows_p), r3(win_p), r3(small_p))
    xp = _mix_ffn([o_a_p.reshape(bp * t, -1), o_b_p.reshape(bp * t, -1)], wo_parts, xp, ln(0, 0), ffn_w(0), ln(0, 1))

    qkv_s, gate_s, qb_s, rows_s, win_s, small_s = _proj(xs, ws)
    conv_s = state_delta_conv[0]
    assert bs % SAMPLE_SEQS == 0
    o_a_s, ds_s = _delta(_pack_rows(qkv_s, bs, history=conv_s), _pack_rows(small_s, bs), _pack_rows(gate_s, bs),
                         jnp.zeros((bs // SAMPLE_SEQS, 8, A_CONV_CH), f32), conv_w, alog, dtb, dng,
                         _pack_state(state_delta_S[0], bs), live_row=SAMPLE_ROW)
    o_a_s = _unpack_rows(o_a_s, bs)
    ds_s = ds_s.reshape(state_delta_S[0].shape)
    win_buf = state_nsa_win[0]
    wb = win_buf.shape[1]
    r1 = lambda a: a.reshape(bs, 1, -1)
    o_b_s = _nsa_decode(page_table, r1(qb_s), r1(rows_s), r1(win_s), r1(small_s), win_buf.reshape(bs, wb, -1),
                        cache_nsa_kv[:, 0].transpose(0, 2, 3, 4, 1))
    xs = _mix_ffn([o_a_s, o_b_s.reshape(bs, -1)], wo_parts, xs, ln(0, 0), ffn_w(0), ln(0, 1))

    kvd = (B_KV, B_DH)
    nsa_rows_p = rows_planes_p.reshape((1, bp, t, 4) + kvd)
    nsa_win_p = r3(win_p)[:, -min(WINDOW, t):].reshape((1, bp, min(WINDOW, t), 2) + kvd)
    delta_conv_p = jnp.concatenate([jnp.zeros((bp, CONV_W - 1, A_CONV_CH), f32), r3(qkv_p)], axis=1)[:, -(CONV_W - 1):][None]
    nsa_rows_s = rows_s.reshape((1, bs, 1, 4) + kvd)
    win_cat = jnp.concatenate([win_buf, win_s.reshape((bs, 1, 2) + kvd)], axis=1)
    nsa_win_s = win_cat[:, -min(WINDOW, wb + 1):][None]
    delta_conv_s = jnp.concatenate([conv_s, qkv_s[:, None, :]], axis=1)[:, -(CONV_W - 1):][None]

    wq, wk, wv, wg1, wr = _odd_weights(w_in_odd[0])
    wg2 = jnp.pad(w_gla_gate2[0], ((0, LANES - C_LOWRANK), (0, 0))).astype(bf16)
    bg2 = b_gla_gate2[0].reshape(1, C_QK)
    gng = gla_norm_g[0].reshape(1, C_DV)
    wo1 = [w_out_odd[0].astype(bf16)]

    q_p, k_p, v_p, g1_p, rr_p = _proj(xp, [wq, wk, wv, wg1, wr])
    o_c_p, gs_p = _gla(r3(q_p), r3(k_p), r3(v_p), r3(rr_p), r3(g1_p), wg2, bg2, gng,
                       jnp.zeros((bp, C_HEADS, C_DK, C_DV), f32))
    xp = _mix_ffn([o_c_p.reshape(bp * t, -1)], wo1, xp, ln(1, 0), ffn_w(1), ln(1, 1))

    q_s, k_s, v_s, g1_s, rr_s = _proj(xs, [wq, wk, wv, wg1, wr])
    o_c_s, gs_s = _gla(_pack_rows(q_s, bs), _pack_rows(k_s, bs), _pack_rows(v_s, bs), _pack_rows(rr_s, bs),
                       _pack_rows(g1_s, bs), wg2, bg2, gng, _pack_state(state_gla_S[0], bs), live_row=SAMPLE_ROW)
    gs_s = gs_s.reshape(state_gla_S[0].shape)
    xs = _mix_ffn([_unpack_rows(o_c_s, bs)], wo1, xs, ln(1, 0), ffn_w(1), ln(1, 1))

    return (xp.reshape(bp, t, d), xs.reshape(bs, 1, d),
            nsa_rows_p, nsa_win_p, delta_conv_p, ds_p[None], gs_p[None],
            nsa_rows_s, nsa_win_s, delta_conv_s, ds_s[None], gs_s[None])
```

```python
import functools
import math

import jax
import jax.numpy as jnp
from jax import lax
from jax.experimental import pallas as pl
from jax.experimental.pallas import tpu as pltpu

f32 = jnp.float32
bf16 = jnp.bfloat16
HI = lax.Precision.HIGHEST

D_MODEL = 1024
DEPTH = 2
PAGE_SIZE = 128

A_HEADS = 4
A_DK = 128
A_DV = 128
A_QK = A_HEADS * A_DK
A_WIDTH = A_HEADS * A_DV
A_CONV_CH = 2 * A_QK + A_WIDTH
CONV_W = 4
DELTA_CHUNK = 64

B_HEADS = 8
B_KV = 2
B_HPG = B_HEADS // B_KV
B_DH = 64
B_WIDTH = B_HEADS * B_DH
N_BRANCH = 3
CMP_BLK = 32
SEL_BLK = 64
SEL_TOPK = 16
WINDOW = 512
B_SCALE = B_DH ** -0.5

C_HEADS = 4
C_DK = 128
C_DV = 256
C_QK = C_HEADS * C_DK
C_WIDTH = C_HEADS * C_DV
C_LOWRANK = 16
GLA_TAU = 16.0
GLA_CHUNK = 64
GLA_SUB = 16
GLA_PREP_CHUNKS = 4

D_FF = -(-(8 * D_MODEL) // (3 * 256)) * 256
ALPHA = (2 * DEPTH) ** 0.25
NEG = -1e30
FORCE = 1e9
LOG2E = 1.4426950408889634

LANES = 128
VMEM_LIMIT = 56 * 1024 * 1024


def _cparams(sem):
    return pltpu.CompilerParams(dimension_semantics=sem, vmem_limit_bytes=VMEM_LIMIT)


def _dot(a, b, precision=None):
    return jnp.dot(a, b, preferred_element_type=f32, precision=precision)


def _dot_nt(a, b, precision=None):
    return lax.dot_general(a, b, (((1,), (1,)), ((), ())), preferred_element_type=f32, precision=precision)


def _dot_tn(a, b, precision=None):
    return lax.dot_general(a, b, (((0,), (0,)), ((), ())), preferred_element_type=f32, precision=precision)


def _silu(x):
    return x * jax.nn.sigmoid(x)


def _layer_norm(x, g, b, eps=1e-5):
    mu = jnp.mean(x, -1, keepdims=True)
    xc = x - mu
    var = jnp.mean(xc * xc, -1, keepdims=True)
    return xc * lax.rsqrt(var + eps) * g + b


def _row_tile(m, pref):
    t = min(pref, m)
    assert m % t == 0
    return t


def _proj_kernel(x_ref, *refs, n, planes_of):
    xb = x_ref[...].astype(bf16)
    tm = x_ref.shape[0]
    for idx, (w_ref, o_ref) in enumerate(zip(refs[:n], refs[n:2 * n])):
        val = _dot(xb, w_ref[...])
        o_ref[...] = val
        if idx == planes_of:
            p_ref = refs[2 * n]
            n_planes = val.shape[1] // B_DH
            for j in range(n_planes):
                p_ref[pl.ds(j, tm, stride=n_planes), :] = val[:, j * B_DH:(j + 1) * B_DH]


def _proj(x, weights, tm=512, planes_of=None):
    m, k = x.shape
    tm = _row_tile(m, tm)
    in_specs = [pl.BlockSpec((tm, k), lambda i: (i, 0))]
    in_specs += [pl.BlockSpec(w.shape, lambda i: (0, 0)) for w in weights]
    out_specs = [pl.BlockSpec((tm, w.shape[1]), lambda i: (i, 0)) for w in weights]
    out_shape = [jax.ShapeDtypeStruct((m, w.shape[1]), f32) for w in weights]
    if planes_of is not None:
        n_planes = weights[planes_of].shape[1] // B_DH
        out_specs.append(pl.BlockSpec((tm * n_planes, B_DH), lambda i: (i, 0)))
        out_shape.append(jax.ShapeDtypeStruct((m * n_planes, B_DH), f32))
    return pl.pallas_call(
        functools.partial(_proj_kernel, n=len(weights), planes_of=planes_of),
        grid=(m // tm,), in_specs=in_specs, out_specs=out_specs, out_shape=out_shape,
        compiler_params=_cparams(("parallel",)), name="in_proj")(x, *weights)


FFN_COLS = 256


def _mix_ffn_kernel(*refs, n_parts):
    parts = refs[:n_parts]
    ws = refs[n_parts:2 * n_parts]
    x_ref, g0_ref, b0_ref, wg_ref, wu_ref, wd_ref, g1_ref, b1_ref, o_ref = refs[2 * n_parts:]
    y = None
    for p_ref, w_ref in zip(parts, ws):
        d = _dot(p_ref[...].astype(bf16), w_ref[...])
        y = d if y is None else y + d
    x = _layer_norm(ALPHA * x_ref[...] + y, g0_ref[...], b0_ref[...])
    xb = x.astype(bf16)
    acc = jnp.zeros(x.shape, f32)
    for c in range(0, D_FF, FFN_COLS):
        hg = _dot(xb, wg_ref[:, c:c + FFN_COLS])
        hu = _dot(xb, wu_ref[:, c:c + FFN_COLS])
        h = (_silu(hg) * hu).astype(bf16)
        acc = acc + _dot(h, wd_ref[c:c + FFN_COLS, :])
    o_ref[...] = _layer_norm(ALPHA * x + acc, g1_ref[...], b1_ref[...])


def _mix_ffn(parts, weights, x, ln0, ffn_w, ln1, tm=512):
    m, d = x.shape
    tm = _row_tile(m, tm)
    n = len(parts)
    row = lambda w: pl.BlockSpec((tm, w), lambda i: (i, 0))
    const = lambda a: pl.BlockSpec(a.shape, lambda i: (0, 0), pipeline_mode=pl.Buffered(1))
    in_specs = [row(p.shape[1]) for p in parts] + [const(w) for w in weights] + [row(d)]
    in_specs += [const(a) for a in (*ln0, *ffn_w, *ln1)]
    return pl.pallas_call(
        functools.partial(_mix_ffn_kernel, n_parts=n), grid=(m // tm,), in_specs=in_specs,
        out_specs=row(d), out_shape=jax.ShapeDtypeStruct((m, d), f32),
        compiler_params=_cparams(("parallel",)), name="mix_ffn")(*parts, *weights, x, *ln0, *ffn_w, *ln1)


DELTA_INV_BLK = 16


def _tri_masks(c):
    r = lax.broadcasted_iota(jnp.int32, (c, c), 0)
    s = lax.broadcasted_iota(jnp.int32, (c, c), 1)
    return r, s


def _split2(x):
    hi = x.astype(bf16)
    return hi, (x - hi.astype(f32)).astype(bf16)


def _split3(x):
    hi = x.astype(bf16)
    rest = x - hi.astype(f32)
    mid = rest.astype(bf16)
    return hi, mid, (rest - mid.astype(f32)).astype(bf16)


def _dot_x3(a, b):
    ah, al = _split2(a)
    bh, bl = _split2(b)
    return _dot(ah, bh) + (_dot(ah, bl) + _dot(al, bh))


def _dot_ones(a_ones, b):
    a16 = a_ones.astype(bf16)
    b1, b2, b3 = _split3(b)
    return _dot(a16, b1) + (_dot(a16, b2) + _dot(a16, b3))


def _unit_lower_inverses(mats, r, s):
    c = mats[0].shape[0]
    eye = (r == s).astype(f32)
    same_blk = (r // DELTA_INV_BLK) == (s // DELTA_INV_BLK)
    diag = [jnp.where(same_blk, a, 0.0) for a in mats]
    low = [a - d for a, d in zip(mats, diag)]
    pw = [-d for d in diag]
    p = [eye + x for x in pw]
    k = 2
    while k < DELTA_INV_BLK:
        pw = [_dot_x3(x, x) for x in pw]
        p = [pi + _dot_x3(pi, x) for pi, x in zip(p, pw)]
        k *= 2
    pw = [_dot_x3(pi, lo) for pi, lo in zip(p, low)]
    q = [eye - x for x in pw]
    k = 2
    while k < c // DELTA_INV_BLK:
        pw = [_dot_x3(x, x) for x in pw]
        q = [qi + _dot_x3(qi, x) for qi, x in zip(q, pw)]
        k *= 2
    return [_dot_x3(qi, pi) for qi, pi in zip(q, p)]


DELTA_PREP_CHUNKS = 2
DELTA_UNROLL_GROUPS = 4


def _delta_kernel(qkv_ref, small_ref, gate_ref, cbuf_ref, cw_ref, alog_ref, dtb_ref, dng_ref, s0_ref,
                  o_ref, s_ref, xbuf, ybuf, tail, u_ref, w_ref, qg_ref, kd_ref, qk_ref, gl_ref,
                  *, tt, live_row):
    t = pl.program_id(1)
    c = DELTA_CHUNK
    n_chunks = tt // c
    cpi = math.gcd(DELTA_PREP_CHUNKS, n_chunks)

    @pl.when(t == 0)
    def _():
        s_ref[...] = s0_ref[...]
        tail[...] = cbuf_ref[...]

    xbuf[0:8, :] = tail[...]
    xbuf[8:8 + tt, :] = qkv_ref[...]
    tail[...] = xbuf[tt:tt + 8, :]

    def row0(chunk):
        return chunk * c if isinstance(chunk, int) else pl.multiple_of(chunk * c, c)

    def conv_group(gi):
        for cc in range(cpi):
            r0 = row0(gi * cpi + cc)
            for col in range(0, A_CONV_CH, A_QK):
                cols = slice(col, col + A_QK)
                x = xbuf[pl.ds(r0, c + 8), cols]
                first = 8 - (CONV_W - 1)
                y = x[first:first + c] * cw_ref[0:1, cols]
                for j in range(1, CONV_W):
                    y = y + x[first + j:first + j + c] * cw_ref[j:j + 1, cols]
                ybuf[pl.ds(r0, c), cols] = _silu(y)

    r, s = _tri_masks(c)
    causal = r >= s
    strict = r > s
    lower_ones = causal.astype(f32)
    upper4 =jnp.concatenate([(r <= s).astype(f32)] * A_HEADS, axis=1)
    all_ones = jnp.ones((c, c), f32)
    alog = alog_ref[...]
    dtb = dtb_ref[...]
    dng = dng_ref[...]

    def prep_group(gi):
        items = []
        for cc in range(cpi):
            r0 = row0(gi * cpi + cc)
            sm = small_ref[pl.ds(r0, c), :]
            beta_all = jax.nn.sigmoid(sm)
            g_all = -jnp.exp(alog) * jax.nn.softplus(sm + dtb)
            if live_row is not None:
                live = lax.broadcasted_iota(jnp.int32, (c, 1), 0) == live_row
                beta_all = jnp.where(live, beta_all, 0.0)
                g_all = jnp.where(live, g_all, 0.0)
            gam_c_all = _dot_ones(lower_ones, g_all)
            g_rows = jnp.concatenate(
                [jnp.broadcast_to(g_all[:, A_HEADS + h:A_HEADS + h + 1], (c, c)) for h in range(A_HEADS)], axis=1)
            gam_r_all = _dot_ones(all_ones, g_rows * upper4)
            for h in range(A_HEADS):
                qh = ybuf[pl.ds(r0, c), h * A_DK:(h + 1) * A_DK]
                kh = ybuf[pl.ds(r0, c), A_QK + h * A_DK:A_QK + (h + 1) * A_DK]
                vh = ybuf[pl.ds(r0, c), 2 * A_QK + h * A_DV:2 * A_QK + (h + 1) * A_DV]
                qh = qh * lax.rsqrt(jnp.sum(qh * qh, -1, keepdims=True) + 1e-6) * (A_DK ** -0.5)
                kh = kh * lax.rsqrt(jnp.sum(kh * kh, -1, keepdims=True) + 1e-6)
                if live_row is not None:
                    qh = jnp.where(live, qh, 0.0)
                    kh = jnp.where(live, kh, 0.0)
                    vh = jnp.where(live, vh, 0.0)
                beta = beta_all[:, h:h + 1]
                gam_c = jnp.broadcast_to(gam_c_all[:, A_HEADS + h:A_HEADS + h + 1], (c, A_DK))
                diff = gam_c[:, :c] - gam_r_all[:, h * c:(h + 1) * c]
                ldec = jnp.where(causal, jnp.exp(jnp.where(causal, diff, 0.0)), 0.0)
                items.append(dict(r0=r0, h=h, q=qh, k=kh, v=vh, beta=beta, gam=gam_c, ldec=ldec, kb=kh * beta))
        amats = [jnp.where(strict, _dot_nt(it["kb"], it["k"]) * it["ldec"], 0.0) for it in items]
        tms = _unit_lower_inverses(amats, r, s)
        for it, tm in zip(items, tms):
            r0, h = it["r0"], it["h"]
            cols = slice(h * A_DK, (h + 1) * A_DK)
            eg = jnp.exp(it["gam"])
            g_last = it["gam"][c - 1:c, :]
            u_ref[pl.ds(r0, c), cols] = _dot(tm, it["v"] * it["beta"])
            w_ref[pl.ds(r0, c), cols] = _dot(tm, it["kb"] * eg)
            qg_ref[pl.ds(r0, c), cols] = it["q"] * eg
            kd_ref[pl.ds(r0, c), cols] = it["k"] * jnp.exp(g_last - it["gam"])
            qk_ref[pl.ds(r0, c), h * c:(h + 1) * c] = _dot_nt(it["q"], it["k"]) * it["ldec"]
            gl_ref[pl.ds(r0, 1), cols] = jnp.exp(g_last)

    def scan_group(gi):
        for cc in range(cpi):
            chunk = gi * cpi + cc
            r0 = row0(chunk)
            for h in range(A_HEADS):
                cols = slice(h * A_DK, (h + 1) * A_DK)
                sidx = (h,) if live_row is None else (chunk, h)
                st = s_ref[sidx]
                v_new = u_ref[pl.ds(r0, c), cols] - _dot(w_ref[pl.ds(r0, c), cols], st)
                o = _dot(qg_ref[pl.ds(r0, c), cols], st) + _dot(qk_ref[pl.ds(r0, c), h * c:(h + 1) * c], v_new)
                s_ref[sidx] = st * gl_ref[pl.ds(r0, 1), cols] + _dot_tn(kd_ref[pl.ds(r0, c), cols], v_new)
                o = o * lax.rsqrt(jnp.mean(o * o, -1, keepdims=True) + 1e-6) * dng
                gt = gate_ref[pl.ds(r0, c), h * A_DV:(h + 1) * A_DV]
                o_ref[pl.ds(r0, c), h * A_DV:(h + 1) * A_DV] = o * _silu(gt)

    n_groups = n_chunks // cpi
    conv_group(0)
    if n_groups > 1:
        prep_group(0)
        conv_group(1)

        def steady(gi, carry):
            scan_group(gi - 1)
            prep_group(gi)
            conv_group(gi + 1)
            return carry

        if n_groups <= DELTA_UNROLL_GROUPS:
            for gi in range(1, n_groups - 1):
                steady(gi, 0)
        else:
            lax.fori_loop(1, n_groups - 1, steady, 0)
        scan_group(n_groups - 2)
    prep_group(n_groups - 1)
    scan_group(n_groups - 1)


def _delta(qkv, small, gate, conv_buf8, conv_w, alog, dtb, dng, s0, *, live_row=None, tt=512):
    b, t_total, _ = qkv.shape
    tt = _row_tile(t_total, tt)
    nt = t_total // tt
    assert live_row is None or (nt == 1 and s0.shape[1] == tt // DELTA_CHUNK)
    row = lambda w: pl.BlockSpec((None, tt, w), lambda i, j: (i, j, 0))
    full2 = lambda a: pl.BlockSpec(a.shape, lambda i, j: (0, 0))
    st_spec = pl.BlockSpec((None,) + s0.shape[1:], lambda i, j: (i,) + (0,) * (s0.ndim - 1))
    kern = functools.partial(_delta_kernel, tt=tt, live_row=live_row)
    return pl.pallas_call(
        kern, grid=(b, nt),
        in_specs=[row(A_CONV_CH), row(LANES), row(A_WIDTH),
                  pl.BlockSpec((None, 8, A_CONV_CH), lambda i, j: (i, 0, 0)),
                  full2(conv_w), full2(alog), full2(dtb), full2(dng), st_spec],
        out_specs=[row(A_WIDTH), st_spec],
        out_shape=[jax.ShapeDtypeStruct((b, t_total, A_WIDTH), f32), jax.ShapeDtypeStruct(s0.shape, f32)],
        scratch_shapes=[pltpu.VMEM((tt + 8, A_CONV_CH), f32), pltpu.VMEM((tt, A_CONV_CH), f32),
                        pltpu.VMEM((8, A_CONV_CH), f32)]
        + [pltpu.VMEM((tt, A_QK), f32)] * 4
        + [pltpu.VMEM((tt, A_HEADS * DELTA_CHUNK), f32), pltpu.VMEM((tt, A_QK), f32)],
        compiler_params=_cparams(("parallel", "arbitrary")), name="delta_mixer",
    )(qkv, small, gate, conv_buf8, conv_w, alog, dtb, dng, s0)


def _gla_kernel(q_ref, k_ref, v_ref, r_ref, g1_ref, wg2_ref, bg2_ref, gng_ref, s0_ref,
                o_ref, s_ref, st_ref, la_ref, qe_ref, gl_ref, ds_ref, *, tt, live_row):
    t = pl.program_id(1)
    nt = pl.num_programs(1)
    c = GLA_CHUNK
    heads = range(C_HEADS)
    kcols = [slice(h * C_DK, (h + 1) * C_DK) for h in heads]
    vcols = [slice(h * C_DV, (h + 1) * C_DV) for h in heads]
    states = [(h,) for h in heads] if live_row is None else [(cs, h) for cs in range(tt // c) for h in heads]

    @pl.when(t == 0)
    def _():
        for sidx in states:
            st_ref[sidx] = s0_ref[sidx].T

    la = jax.nn.log_sigmoid(_dot(g1_ref[...].astype(bf16), wg2_ref[...]) + bg2_ref[...]) / GLA_TAU
    if live_row is not None:
        la = jnp.where(lax.broadcasted_iota(jnp.int32, (tt, 1), 0) % c == live_row, la, 0.0)
    la_ref[...] = la

    r, s = _tri_masks(c)
    lower_ones = (r >= s).astype(f32)
    rs = lax.broadcasted_iota(jnp.int32, (GLA_SUB, c), 0)
    ss = lax.broadcasted_iota(jnp.int32, (GLA_SUB, c), 1)
    krow = lax.broadcasted_iota(jnp.int32, (c, 1), 0)
    gng = gng_ref[...]

    n_chunks = tt // c
    cpi = math.gcd(GLA_PREP_CHUNKS, n_chunks)

    def row0(chunk):
        return chunk * c if isinstance(chunk, int) else pl.multiple_of(chunk * c, c)

    def prep_group(gi):
        items = []
        for cc in range(cpi):
            chunk = gi * cpi + cc
            r0 = row0(chunk)
            rows = pl.ds(r0, c)
            bc_all = _dot_ones(lower_ones, la_ref[rows, :])
            for h in heads:
                items.append(dict(chunk=chunk, r0=r0, rows=rows, h=h, q=q_ref[rows, kcols[h]] * (C_DK ** -0.5),
                                  k=k_ref[rows, kcols[h]], v=v_ref[rows, vcols[h]], bc=bc_all[:, kcols[h]]))
        for it in items:
            qe_ref[it["rows"], kcols[it["h"]]] = it["q"] * jnp.exp(it["bc"])
        for a in range(c // GLA_SUB):
            lo, hi = a * GLA_SUB, (a + 1) * GLA_SUB
            seen = krow < hi
            att = []
            for it in items:
                bref = it["bc"][lo - 1:lo, :] if a > 0 else jnp.zeros((1, C_DK), f32)
                qa = it["q"][lo:hi, :] * jnp.exp(it["bc"][lo:hi, :] - bref)
                ka = jnp.where(seen, it["k"] * jnp.exp(jnp.where(seen, bref - it["bc"], 0.0)), 0.0)
                att.append(jnp.where(rs + lo >= ss, _dot_nt(qa, ka), 0.0))
            for it, at in zip(items, att):
                o_ref[pl.ds(it["r0"] + lo, GLA_SUB), vcols[it["h"]]] = _dot(at, it["v"])
        for it in items:
            b_last = it["bc"][c - 1:c, :]
            gl_ref[pl.ds(it["r0"], 1), kcols[it["h"]]] = jnp.exp(b_last)
            ds_ref[it["chunk"], it["h"]] = _dot_tn(it["v"], it["k"] * jnp.exp(b_last - it["bc"]))

    def scan_group(gi):
        for cc in range(cpi):
            chunk = gi * cpi + cc
            r0 = row0(chunk)
            rows = pl.ds(r0, c)
            for h in heads:
                sidx = (h,) if live_row is None else (chunk, h)
                st = st_ref[sidx]
                o = _dot_nt(qe_ref[rows, kcols[h]], st) + o_ref[rows, vcols[h]]
                st_ref[sidx] = st * gl_ref[pl.ds(r0, 1), kcols[h]] + ds_ref[chunk, h]
                o = o * lax.rsqrt(jnp.mean(o * o, -1, keepdims=True) + 1e-6) * gng
                o_ref[rows, vcols[h]] = o * _silu(r_ref[rows, vcols[h]])

    n_groups = n_chunks // cpi
    prep_group(0)

    def steady(gi, carry):
        scan_group(gi - 1)
        prep_group(gi)
        return carry

    lax.fori_loop(1, n_groups, steady, 0)
    scan_group(n_groups - 1)

    @pl.when(t == nt - 1)
    def _():
        for sidx in states:
            s_ref[sidx] = st_ref[sidx].T


def _gla(q, k, v, r, g1, wg2, bg2, gng, s0, *, live_row=None, tt=512):
    b, t_total, _ = q.shape
    tt = _row_tile(t_total, tt)
    nt = t_total // tt
    assert live_row is None or (nt == 1 and s0.shape[1] == tt // GLA_CHUNK)
    row = lambda w: pl.BlockSpec((None, tt, w), lambda i, j: (i, j, 0))
    full2 = lambda a: pl.BlockSpec(a.shape, lambda i, j: (0, 0))
    st_spec = pl.BlockSpec((None,) + s0.shape[1:], lambda i, j: (i,) + (0,) * (s0.ndim - 1))
    kern = functools.partial(_gla_kernel, tt=tt, live_row=live_row)
    return pl.pallas_call(
        kern, grid=(b, nt),
        in_specs=[row(C_QK), row(C_QK), row(C_WIDTH), row(C_WIDTH), row(LANES),
                  full2(wg2), full2(bg2), full2(gng), st_spec],
        out_specs=[row(C_WIDTH), st_spec],
        out_shape=[jax.ShapeDtypeStruct((b, t_total, C_WIDTH), f32), jax.ShapeDtypeStruct(s0.shape, f32)],
        scratch_shapes=[pltpu.VMEM(s0.shape[1:-2] + (C_DV, C_DK), f32), pltpu.VMEM((tt, C_QK), f32),
                        pltpu.VMEM((tt, C_QK), f32), pltpu.VMEM((tt, C_QK), f32),
                        pltpu.VMEM((tt // GLA_CHUNK, C_HEADS, C_DV, C_DK), f32)],
        compiler_params=_cparams(("parallel", "arbitrary")), name="gla_mixer",
    )(q, k, v, r, g1, wg2, bg2, gng, s0)


GB_LANE0 = 2 * A_HEADS


def _masked_softmax(s, mask):
    sm = jnp.where(mask, s, NEG)
    e = jnp.exp(sm - jnp.max(sm, -1, keepdims=True))
    p = e / jnp.sum(e, -1, keepdims=True)
    return jnp.where(mask, p, 0.0)


def _topk_rank(score, n_cand):
    lane = lax.broadcasted_iota(jnp.int32, score.shape, score.ndim - 1)
    rank = jnp.zeros(score.shape, jnp.int32)
    for i in range(n_cand):
        si = score[..., i:i + 1]
        ahead = (si > score) | ((si == score) & (i < lane))
        rank = rank + ahead.astype(jnp.int32)
    return rank


def _block_scores(imp, qpos, n_blk):
    blk = lax.broadcasted_iota(jnp.int32, imp.shape, imp.ndim - 1)
    cur = qpos // SEL_BLK
    valid = blk * SEL_BLK <= qpos
    forced = (blk == 0) | (blk == cur) | (blk == cur - 1)
    return jnp.where(forced, FORCE, jnp.where(valid, imp, NEG))


def _split_cmp_index(col, half):
    return jnp.where(col < half, 2 * col, 2 * (col - half) + 1)


NSA_TQ = 128
NSA_QSUB = 2
NSA_TK = 512
NSA_SUB = 128


def _topk_mask_rows(score, k):
    n = score.shape[0]
    row = lax.broadcasted_iota(jnp.int32, score.shape, 0)
    rank = jnp.zeros(score.shape, jnp.int32)
    for i in range(n):
        si = score[i:i + 1, :]
        ahead = (si > score) | ((si == score) & (i < row))
        rank = rank + ahead.astype(jnp.int32)
    return (rank < k).astype(f32)


def _nsa_prompt_kernel(q_ref, rows_ref, win_ref, small_ref, o_ref, cb_ref, sel_ref, sc_ref, *, t_total):
    i = pl.program_id(1)
    tq = NSA_TQ
    tk = min(NSA_TK, t_total)
    tw = min(WINDOW + tq, t_total)
    n_sel = t_total // SEL_BLK
    half = n_sel
    hq = B_HPG * tq
    blk_per_tile = tk // SEL_BLK
    kv_w = B_KV * B_DH

    @pl.when(i == 0)
    def _():
        x = rows_ref[:, 0:2 * kv_w].reshape(n_sel, SEL_BLK, 2 * kv_w)
        cb_ref[0:half, :] = jnp.sum(x[:, :CMP_BLK, :], axis=1) * (1.0 / CMP_BLK)
        cb_ref[half:2 * half, :] = jnp.sum(x[:, CMP_BLK:, :], axis=1) * (1.0 / CMP_BLK)

    n_qsub = o_ref.shape[0] // tq
    chains = [(s, g) for s in range(n_qsub) for g in range(B_KV)]
    tile0 = [(i * n_qsub + s) * tq for s in range(n_qsub)]
    qpos_s = [t0 + lax.broadcasted_iota(jnp.int32, (1, tq), 1) for t0 in tile0]
    gates_t = jax.nn.sigmoid(small_ref[...].T)
    n_causal = (tile0[-1] + tq - 1) // tk + 1
    q4 = [jnp.concatenate([q_ref[s * tq:(s + 1) * tq, (g * B_HPG + h) * B_DH:(g * B_HPG + h + 1) * B_DH]
                           for h in range(B_HPG)], axis=0) * (B_SCALE * LOG2E)
          for s, g in chains]

    o_cmp = []
    for c, (s, g) in enumerate(chains):
        qpos = qpos_s[s]
        qpos4 = jnp.concatenate([qpos] * B_HPG, axis=1)
        kc = cb_ref[:, g * B_DH:(g + 1) * B_DH]
        vc = cb_ref[:, (B_KV + g) * B_DH:(B_KV + g + 1) * B_DH]
        st = _dot_nt(kc, q4[c])
        cidx = _split_cmp_index(lax.broadcasted_iota(jnp.int32, (2 * half, 1), 0), half)
        cmask = (cidx + 1) * CMP_BLK <= qpos4 + 1
        sm = jnp.where(cmask, st, NEG)
        e = jnp.exp2(sm - jnp.max(sm, axis=0, keepdims=True))
        p = jnp.where(cmask, e / jnp.sum(e, axis=0, keepdims=True), 0.0)
        o_cmp.append(_dot_tn(vc, p))
        imp = p[:, 0:tq]
        for h in range(1, B_HPG):
            imp = imp + p[:, h * tq:(h + 1) * tq]
        imp = imp[:half] + imp[half:]
        blk = lax.broadcasted_iota(jnp.int32, (n_sel, 1), 0)
        cur = qpos // SEL_BLK
        forced = (blk == 0) | (blk == cur) | (blk == cur - 1)
        score = jnp.where(forced, FORCE, jnp.where(blk * SEL_BLK <= qpos, imp, NEG))
        picked = (_topk_mask_rows(score, min(SEL_TOPK, n_sel)) > 0.5) & (blk * SEL_BLK <= qpos)
        sel_ref[c] = jnp.where(picked, 0.0, NEG)

    sub = min(NSA_SUB, tk)

    def attend(c, kv_ref, k0, n_keys, kl, vl, bias_fn, carry):
        m, acc = carry
        m_new = m
        for u in range(n_keys // sub):
            k = kv_ref[pl.ds(k0 + u * sub, sub), kl:kl + B_DH]
            bias = bias_fn(u)
            sc = _dot_nt(k, q4[c]) + jnp.concatenate([bias] * B_HPG, axis=1)
            sc_ref[c, u * sub:(u + 1) * sub, :] = sc
            m_new = jnp.maximum(m_new, jnp.max(sc, axis=0, keepdims=True))
        acc = jnp.exp2(m - m_new) * acc
        for u in range(n_keys // sub):
            pe = jnp.exp2(sc_ref[c, u * sub:(u + 1) * sub, :] - m_new)
            v1 = jnp.concatenate([kv_ref[pl.ds(k0 + u * sub, sub), vl:vl + B_DH], ones_blk], axis=1)
            acc = acc + _dot_tn(v1, pe)
        return m_new, acc

    ones_blk = jnp.ones((sub, B_DH), f32)
    init = (jnp.full((1, hq), NEG, f32), jnp.zeros((2 * B_DH, hq), f32))
    srow = lax.broadcasted_iota(jnp.int32, (sub, 1), 0)
    blk_per_sub = sub // SEL_BLK
    assert sub == tq
    own_block_bias = jnp.where(srow <= lax.broadcasted_iota(jnp.int32, (1, tq), 1), 0.0, NEG)

    def sel_step(j, carries):
        k0 = j * tk
        out = []
        for c, (s, g) in enumerate(chains):
            blk_bias = sel_ref[c, j * blk_per_tile:(j + 1) * blk_per_tile, :]

            def bias_fn(u, blk_bias=blk_bias, s=s):
                rows = jnp.concatenate(
                    [jnp.broadcast_to(blk_bias[u * blk_per_sub + v:u * blk_per_sub + v + 1, :], (SEL_BLK, tq))
                     for v in range(blk_per_sub)], axis=0)
                return rows + jnp.where(k0 + u * sub == tile0[s], own_block_bias, 0.0)

            out.append(attend(c, rows_ref, k0, tk, (2 * B_KV + g) * B_DH, (3 * B_KV + g) * B_DH, bias_fn,
                              carries[c]))
        return tuple(out)

    def tiles_variant(n):
        def run():
            carries = (init,) * len(chains)
            for j in range(n):
                carries = sel_step(j, carries)
            return carries
        return run

    sel_out = lax.switch(n_causal - 1, [tiles_variant(n) for n in range(1, t_total // tk + 1)])
    w0 = [pl.multiple_of(jnp.maximum(t0 + tq - tw, 0), tq) for t0 in tile0]
    win_bias = []
    for s in range(n_qsub):
        per_sub = []
        for u in range(tw // sub):
            d = qpos_s[s] - (w0[s] + u * sub + srow)
            per_sub.append(jnp.where((d >= 0) & (d < WINDOW), 0.0, NEG))
        win_bias.append(per_sub)

    win_out = [attend(c, win_ref, w0[s], tw, g * B_DH, (B_KV + g) * B_DH, lambda u, s=s: win_bias[s][u], init)
               for c, (s, g) in enumerate(chains)]
    for s in range(n_qsub):
        out_rows = []
        for c, (cs, g) in enumerate(chains):
            if cs != s:
                continue
            acc_w, acc_s = win_out[c][1], sel_out[c][1]
            o_win = acc_w[:B_DH] / acc_w[B_DH:B_DH + 1]
            o_sel = acc_s[:B_DH] / acc_s[B_DH:B_DH + 1]
            gt = gates_t[:, s * tq:(s + 1) * tq]
            for h in range(B_HPG):
                c0 = GB_LANE0 + (g * B_HPG + h) * N_BRANCH
                ls = slice(h * tq, (h + 1) * tq)
                out_rows.append(gt[c0:c0 + 1, :] * o_cmp[c][:, ls] + gt[c0 + 1:c0 + 2, :] * o_sel[:, ls]
                                + gt[c0 + 2:c0 + 3, :] * o_win[:, ls])
        o_ref[s * tq:(s + 1) * tq, :] = jnp.concatenate(out_rows, axis=0).T


def _nsa_prompt(qb, rows, win, small):
    b, t_total, _ = qb.shape
    tk = min(NSA_TK, t_total)
    rows_per_step = NSA_TQ * NSA_QSUB
    assert t_total % tk == 0 and tk % NSA_TQ == 0 and tk % SEL_BLK == 0 and WINDOW % NSA_TQ == 0
    assert t_total % rows_per_step == 0
    nq = t_total // rows_per_step
    n_sel = t_total // SEL_BLK
    n_chains = NSA_QSUB * B_KV
    tile = lambda w: pl.BlockSpec((None, rows_per_step, w), lambda bi, i: (bi, i, 0))
    seq = lambda w: pl.BlockSpec((None, t_total, w), lambda bi, i: (bi, 0, 0))
    return pl.pallas_call(
        functools.partial(_nsa_prompt_kernel, t_total=t_total), grid=(b, nq),
        in_specs=[tile(B_WIDTH), seq(4 * B_KV * B_DH), seq(2 * B_KV * B_DH), tile(LANES)],
        out_specs=tile(B_WIDTH), out_shape=jax.ShapeDtypeStruct((b, t_total, B_WIDTH), f32),
        scratch_shapes=[pltpu.VMEM((2 * n_sel, 2 * B_KV * B_DH), f32), pltpu.VMEM((n_chains, n_sel, NSA_TQ), f32),
                        pltpu.VMEM((n_chains, max(tk, min(WINDOW + NSA_TQ, t_total)), B_HPG * NSA_TQ), f32)],
        compiler_params=_cparams(("parallel", "arbitrary")), name="nsa_prompt",
    )(qb, rows, win, small)


DECODE_POOL_PAGES = 8


def _nsa_decode_cmp_kernel(pt_ref, q_ref, wnew_ref, wbuf_ref, pool_ref, cache_ref, ocmp_ref, owin_ref, idx_ref,
                           pages, acc_ref, sem, *, n_pages, past):
    b = pl.program_id(0)
    nb = pl.num_programs(0)
    slot = b % 2
    n_hist_blk = past // SEL_BLK
    n_sel = -(-(past + 1) // SEL_BLK)
    half = n_hist_blk
    lanes_sel = 2 * half
    wb = wbuf_ref.shape[0]
    n_cmp_rows = 2 * B_KV * B_DH

    def page_copy(bb, p, sl):
        return pltpu.make_async_copy(cache_ref.at[pt_ref[bb, p], pl.ds(0, 2)], pages.at[sl, p], sem.at[sl])

    def fetch(bb, sl):
        for p in range(n_pages):
            page_copy(bb, p, sl).start()

    @pl.when(b == 0)
    def _():
        fetch(0, 0)

    @pl.when(b + 1 < nb)
    def _():
        fetch(b + 1, 1 - slot)

    for p in range(n_pages):
        page_copy(b, p, slot).wait()

    acc_ref[...] = jnp.zeros(acc_ref.shape, f32)
    ppc = math.gcd(n_pages, DECODE_POOL_PAGES)

    def pool_body(c, carry):
        x = jnp.concatenate([pages[slot, c * ppc + i].reshape(n_cmp_rows, PAGE_SIZE) for i in range(ppc)], axis=1)
        hi, lo = _split2(x)
        w = pool_ref[pl.ds(pl.multiple_of(c * ppc * PAGE_SIZE, ppc * PAGE_SIZE), ppc * PAGE_SIZE), :]
        acc_ref[...] += _dot(hi, w) + _dot(lo, w)
        return carry

    lax.fori_loop(0, n_pages // ppc, pool_body, 0)

    qpos = past
    row8 = lax.broadcasted_iota(jnp.int32, (B_HEADS, 1), 0)
    q8 = jnp.concatenate([q_ref[:, h * B_DH:(h + 1) * B_DH] for h in range(B_HEADS)], axis=0) * B_SCALE
    wnew = wnew_ref[...]
    lane = lax.broadcasted_iota(jnp.int32, (1, lanes_sel), 1)
    o_cmp, o_win, imps = [], [], []
    for g in range(B_KV):
        in_g = (row8 // B_HPG) == g
        kc_t = acc_ref[g * B_DH:(g + 1) * B_DH, :]
        vc_t = acc_ref[(B_KV + g) * B_DH:(B_KV + g + 1) * B_DH, :]
        sc = _dot(q8, kc_t)
        cidx = _split_cmp_index(lane, half)
        p = _masked_softmax(sc, (cidx + 1) * CMP_BLK <= qpos + 1)
        o_cmp.append(_dot_nt(p, vc_t))
        imp = jnp.sum(jnp.where(in_g, p, 0.0), axis=0, keepdims=True)
        imp = imp[:, :half] + imp[:, half:]
        imps.append(jnp.concatenate([imp, jnp.zeros((1, lanes_sel - half), f32)], axis=1))
        kw = wbuf_ref[:, g * B_DH:(g + 1) * B_DH]
        vw = wbuf_ref[:, (B_KV + g) * B_DH:(B_KV + g + 1) * B_DH]
        kw_new = wnew[:, g * B_DH:(g + 1) * B_DH]
        vw_new = wnew[:, (B_KV + g) * B_DH:(B_KV + g + 1) * B_DH]
        kpos = qpos - wb + lax.broadcasted_iota(jnp.int32, (1, wb), 1)
        d = qpos - kpos
        wmask = (d >= 0) & (d < WINDOW) & (kpos >= 0)
        sw = jnp.where(wmask, _dot_nt(q8, kw), NEG)
        swn = jnp.sum(q8 * kw_new, -1, keepdims=True)
        m = jnp.maximum(jnp.max(sw, -1, keepdims=True), swn)
        ew = jnp.where(wmask, jnp.exp(sw - m), 0.0)
        ewn = jnp.exp(swn - m)
        o_win.append((_dot(ew, vw) + ewn * vw_new) / (jnp.sum(ew, -1, keepdims=True) + ewn))
    rank = _topk_rank(_block_scores(jnp.concatenate(imps, axis=0), qpos, n_sel), n_sel)
    lane_f = lane.astype(f32)
    out_lane = lax.broadcasted_iota(jnp.int32, (1, LANES), 1)
    picked = jnp.zeros((B_KV, LANES), f32)
    for r in range(min(SEL_TOPK, n_sel)):
        blk_r = jnp.sum(jnp.where((rank == r) & (lane < n_sel), lane_f, 0.0), axis=-1, keepdims=True)
        picked = picked + jnp.where(out_lane == r, blk_r, 0.0)
    lower = row8 < B_HPG
    ocmp_ref[...] = jnp.where(lower, o_cmp[0], o_cmp[1])
    owin_ref[...] = jnp.where(lower, o_win[0], o_win[1])
    idx_ref[...] = jnp.concatenate([picked, jnp.zeros((8 - B_KV, LANES), f32)], axis=0).astype(jnp.int32)


def _nsa_decode_sel_kernel(pt_ref, sel_ref, q_ref, rnew_ref, small_ref, ocmp_ref, owin_ref, cache_ref,
                           o_ref, blocks, sem, *, past):
    b = pl.program_id(0)
    nb = pl.num_programs(0)
    slot = b % 2
    n_hist_blk = past // SEL_BLK
    n_pick = sel_ref.shape[2]
    blk_per_page = PAGE_SIZE // SEL_BLK
    tok_lane = lax.broadcasted_iota(jnp.int32, (1, PAGE_SIZE), 1)

    def hist_block(bb, g, j):
        return jnp.minimum(sel_ref[bb, g, j], n_hist_blk - 1)

    def block_copy(bb, g, j, sl):
        page = pt_ref[bb, hist_block(bb, g, j) // blk_per_page]
        return pltpu.make_async_copy(cache_ref.at[page, pl.ds(2, 2), g], blocks.at[sl, g * n_pick + j], sem.at[sl])

    def fetch(bb, sl):
        for g in range(B_KV):
            for j in range(n_pick):
                block_copy(bb, g, j, sl).start()

    @pl.when(b == 0)
    def _():
        fetch(0, 0)

    @pl.when(b + 1 < nb)
    def _():
        fetch(b + 1, 1 - slot)

    for g in range(B_KV):
        for j in range(n_pick):
            block_copy(b, g, j, slot).wait()

    row8 = lax.broadcasted_iota(jnp.int32, (B_HEADS, 1), 0)
    q8 = jnp.concatenate([q_ref[:, h * B_DH:(h + 1) * B_DH] for h in range(B_HEADS)], axis=0) * B_SCALE
    rnew = rnew_ref[...]
    o_sel = []
    for g in range(B_KV):
        ks_t = jnp.concatenate([blocks[slot, g * n_pick + j, 0] for j in range(n_pick)], axis=1)
        vs_t = jnp.concatenate([blocks[slot, g * n_pick + j, 1] for j in range(n_pick)], axis=1)
        picks = [sel_ref[b, g, j] for j in range(n_pick)]
        live = jnp.concatenate(
            [(tok_lane // SEL_BLK == hist_block(b, g, j) % blk_per_page) & (picks[j] < n_hist_blk)
             for j in range(n_pick)], axis=1)
        new_live = functools.reduce(jnp.logical_or, [pk >= n_hist_blk for pk in picks])
        k_new = rnew[:, (2 * B_KV + g) * B_DH:(2 * B_KV + g + 1) * B_DH]
        v_new = rnew[:, (3 * B_KV + g) * B_DH:(3 * B_KV + g + 1) * B_DH]
        sh = jnp.where(live, _dot(q8, ks_t), NEG)
        sn = jnp.sum(q8 * k_new, -1, keepdims=True) + jnp.where(new_live, 0.0, NEG).astype(f32)
        m = jnp.maximum(jnp.max(sh, -1, keepdims=True), sn)
        eh = jnp.where(live, jnp.exp(sh - m), 0.0)
        en = jnp.where(new_live, jnp.exp(sn - m), 0.0)
        o_sel.append((_dot_nt(eh, vs_t) + en * v_new) / (jnp.sum(eh, -1, keepdims=True) + en))
    os_ = jnp.where(row8 < B_HPG, o_sel[0], o_sel[1])
    oc = ocmp_ref[...]
    ow = owin_ref[...]
    gates = jax.nn.sigmoid(small_ref[...])
    outs = []
    for hh in range(B_HEADS):
        c0 = GB_LANE0 + hh * N_BRANCH
        outs.append(gates[:, c0:c0 + 1] * oc[hh:hh + 1] + gates[:, c0 + 1:c0 + 2] * os_[hh:hh + 1]
                    + gates[:, c0 + 2:c0 + 3] * ow[hh:hh + 1])
    o_ref[...] = jnp.concatenate(outs, axis=1)


def _block_pool_matrix(past):
    blk = jnp.arange(past) // CMP_BLK
    col = (blk % 2) * (past // SEL_BLK) + blk // 2
    return jnp.where(col[:, None] == jnp.arange(2 * (past // SEL_BLK))[None, :], 1.0 / CMP_BLK, 0.0).astype(bf16)


def _nsa_decode(page_table, qb, rows_new, win_new, small, win_buf, cache_t):
    nb, n_pages = page_table.shape
    past = n_pages * PAGE_SIZE
    assert past % SEL_BLK == 0 and cache_t.shape[1:] == (4, B_KV, B_DH, PAGE_SIZE)
    n_blk = past // SEL_BLK
    n_pick = min(SEL_TOPK, n_blk + 1)
    pool = _block_pool_matrix(past)
    one1 = lambda shape: pl.BlockSpec((None,) + shape, lambda i, pt: (i, 0, 0))
    cmp_spec = pltpu.PrefetchScalarGridSpec(
        num_scalar_prefetch=1, grid=(nb,),
        in_specs=[one1((1, B_WIDTH)), one1((1, 2 * B_KV * B_DH)), one1(win_buf.shape[1:]),
                  pl.BlockSpec(pool.shape, lambda i, pt: (0, 0)), pl.BlockSpec(memory_space=pl.ANY)],
        out_specs=[one1((B_HEADS, B_DH)), one1((B_HEADS, B_DH)), one1((8, LANES))],
        scratch_shapes=[pltpu.VMEM((2, n_pages, 2, B_KV, B_DH, PAGE_SIZE), f32),
                        pltpu.VMEM((2 * B_KV * B_DH, 2 * n_blk), f32), pltpu.SemaphoreType.DMA((2,))])
    o_cmp, o_win, idx = pl.pallas_call(
        functools.partial(_nsa_decode_cmp_kernel, n_pages=n_pages, past=past), grid_spec=cmp_spec,
        out_shape=[jax.ShapeDtypeStruct((nb, B_HEADS, B_DH), f32), jax.ShapeDtypeStruct((nb, B_HEADS, B_DH), f32),
                   jax.ShapeDtypeStruct((nb, 8, LANES), jnp.int32)],
        compiler_params=_cparams(("arbitrary",)), name="nsa_decode_cmp",
    )(page_table, qb, win_new, win_buf, pool, cache_t)
    picks = idx[:, :B_KV, :n_pick]
    one2 = lambda shape: pl.BlockSpec((None,) + shape, lambda i, pt, sel: (i, 0, 0))
    sel_spec = pltpu.PrefetchScalarGridSpec(
        num_scalar_prefetch=2, grid=(nb,),
        in_specs=[one2((1, B_WIDTH)), one2((1, 4 * B_KV * B_DH)), one2((1, LANES)),
                  one2((B_HEADS, B_DH)), one2((B_HEADS, B_DH)), pl.BlockSpec(memory_space=pl.ANY)],
        out_specs=one2((1, B_WIDTH)),
        scratch_shapes=[pltpu.VMEM((2, B_KV * n_pick, 2, B_DH, PAGE_SIZE), f32), pltpu.SemaphoreType.DMA((2,))])
    return pl.pallas_call(
        functools.partial(_nsa_decode_sel_kernel, past=past), grid_spec=sel_spec,
        out_shape=jax.ShapeDtypeStruct((nb, 1, B_WIDTH), f32),
        compiler_params=_cparams(("arbitrary",)), name="nsa_decode_sel",
    )(page_table, picks, qb, rows_new, small, o_cmp, o_win, cache_t)


SAMPLE_PAD = DELTA_CHUNK
SAMPLE_SEQS = 8
SAMPLE_ROW = 8


def _even_weights(w_in):
    o = 0
    cols = {}
    for name, n in (("qkv", A_CONV_CH), ("beta", A_HEADS), ("a", A_HEADS), ("gate", A_WIDTH), ("qb", B_WIDTH),
                    ("kv", N_BRANCH * 2 * B_KV * B_DH), ("gb", B_HEADS * N_BRANCH)):
        cols[name] = w_in[:, o:o + n]
        o += n
    n_rows = 4 * B_KV * B_DH
    small = jnp.concatenate([cols["beta"], cols["a"], cols["gb"]], axis=1)
    small = jnp.pad(small, ((0, 0), (0, LANES - small.shape[1])))
    ws = [cols["qkv"], cols["gate"], cols["qb"], cols["kv"][:, :n_rows], cols["kv"][:, n_rows:], small]
    return [w.astype(bf16) for w in ws]


def _odd_weights(w_in):
    o = 0
    ws = []
    for n in (C_QK, C_QK, C_WIDTH, C_LOWRANK, C_WIDTH):
        ws.append(w_in[:, o:o + n])
        o += n
    ws[3] = jnp.pad(ws[3], ((0, 0), (0, LANES - C_LOWRANK)))
    return [w.astype(bf16) for w in ws]


def _lane_param(v, lane0):
    return jnp.zeros((1, LANES), f32).at[0, lane0:lane0 + v.shape[0]].set(v.astype(f32))


def _pack_rows(a, nb, history=None):
    a = a.reshape(nb, 1, -1)
    before = SAMPLE_ROW
    if history is not None:
        a = jnp.concatenate([history, a], axis=1)
        before -= history.shape[1]
    a = jnp.pad(a, ((0, 0), (before, SAMPLE_PAD - before - a.shape[1]), (0, 0)))
    return a.reshape(nb // SAMPLE_SEQS, SAMPLE_SEQS * SAMPLE_PAD, -1)


def _unpack_rows(a, nb):
    return a.reshape(nb, SAMPLE_PAD, -1)[:, SAMPLE_ROW]


def _pack_state(s, nb):
    return s.reshape((nb // SAMPLE_SEQS, SAMPLE_SEQS) + s.shape[1:])


def kernel(x_prompt, x_sample, cache_nsa_kv, state_nsa_win, state_delta_conv, state_delta_S, state_gla_S,
           page_table, w_in_even, conv_w_delta, delta_A_log, delta_dt_bias, delta_norm_g, w_out_even,
           w_in_odd, w_gla_gate2, b_gla_gate2, gla_norm_g, w_out_odd, w_ffn_gate, w_ffn_up, w_ffn_down,
           ln_g, ln_b):
    bp, t, d = x_prompt.shape
    bs = x_sample.shape[0]
    assert x_sample.shape[1] == 1 and w_in_even.shape[0] == 1 and w_in_odd.shape[0] == 1
    n_pages = page_table.shape[1]
    past = n_pages * PAGE_SIZE
    xp = x_prompt.reshape(bp * t, d)
    xs = x_sample.reshape(bs, d)
    ln = lambda layer, j: (ln_g[layer, j].reshape(1, d), ln_b[layer, j].reshape(1, d))
    ffn_w = lambda layer: (w_ffn_gate[layer].astype(bf16), w_ffn_up[layer].astype(bf16),
                           w_ffn_down[layer].astype(bf16))

    ws = _even_weights(w_in_even[0])
    conv_w = conv_w_delta[0]
    alog = _lane_param(delta_A_log[0], A_HEADS)
    dtb = _lane_param(delta_dt_bias[0], A_HEADS)
    dng = delta_norm_g[0].reshape(1, A_DV)
    wo = w_out_even[0].astype(bf16)
    wo_parts = [wo[:A_WIDTH], wo[A_WIDTH:]]

    qkv_p, gate_p, qb_p, rows_p, win_p, small_p, rows_planes_p = _proj(xp, ws, planes_of=3)
    r3 = lambda a: a.reshape(bp, t, -1)
    o_a_p, ds_p = _delta(r3(qkv_p), r3(small_p), r3(gate_p), jnp.zeros((bp, 8, A_CONV_CH), f32), conv_w,
                         alog, dtb, dng, jnp.zeros((bp, A_HEADS, A_DK, A_DV), f32))
    o_b_p = _nsa_prompt(r3(qb_p), r3(rows_p), r3(win_p), r3(small_p))
    xp = _mix_ffn([o_a_p.reshape(bp * t, -1), o_b_p.reshape(bp * t, -1)], wo_parts, xp, ln(0, 0), ffn_w(0), ln(0, 1))

    qkv_s, gate_s, qb_s, rows_s, win_s, small_s = _proj(xs, ws)
    conv_s = state_delta_conv[0]
    assert bs % SAMPLE_SEQS == 0
    o_a_s, ds_s = _delta(_pack_rows(qkv_s, bs, history=conv_s), _pack_rows(small_s, bs), _pack_rows(gate_s, bs),
                         jnp.zeros((bs // SAMPLE_SEQS, 8, A_CONV_CH), f32), conv_w, alog, dtb, dng,
                         _pack_state(state_delta_S[0], bs), live_row=SAMPLE_ROW)
    o_a_s = _unpack_rows(o_a_s, bs)
    ds_s = ds_s.reshape(state_delta_S[0].shape)
    win_buf = state_nsa_win[0]
    wb = win_buf.shape[1]
    r1 = lambda a: a.reshape(bs, 1, -1)
    o_b_s = _nsa_decode(page_table, r1(qb_s), r1(rows_s), r1(win_s), r1(small_s), win_buf.reshape(bs, wb, -1),
                        cache_nsa_kv[:, 0].transpose(0, 2, 3, 4, 1))
    xs = _mix_ffn([o_a_s, o_b_s.reshape(bs, -1)], wo_parts, xs, ln(0, 0), ffn_w(0), ln(0, 1))

    kvd = (B_KV, B_DH)
    nsa_rows_p = rows_planes_p.reshape((1, bp, t, 4) + kvd)
    nsa_win_p = r3(win_p)[:, -min(WINDOW, t):].reshape((1, bp, min(WINDOW, t), 2) + kvd)
    delta_conv_p = jnp.concatenate([jnp.zeros((bp, CONV_W - 1, A_CONV_CH), f32), r3(qkv_p)], axis=1)[:, -(CONV_W - 1):][None]
    nsa_rows_s = rows_s.reshape((1, bs, 1, 4) + kvd)
    win_cat = jnp.concatenate([win_buf, win_s.reshape((bs, 1, 2) + kvd)], axis=1)
    nsa_win_s = win_cat[:, -min(WINDOW, wb + 1):][None]
    delta_conv_s = jnp.concatenate([conv_s, qkv_s[:, None, :]], axis=1)[:, -(CONV_W - 1):][None]

    wq, wk, wv, wg1, wr = _odd_weights(w_in_odd[0])
    wg2 = jnp.pad(w_gla_gate2[0], ((0, LANES - C_LOWRANK), (0, 0))).astype(bf16)
    bg2 = b_gla_gate2[0].reshape(1, C_QK)
    gng = gla_norm_g[0].reshape(1, C_DV)
    wo1 = [w_out_odd[0].astype(bf16)]

    q_p, k_p, v_p, g1_p, rr_p = _proj(xp, [wq, wk, wv, wg1, wr])
    o_c_p, gs_p = _gla(r3(q_p), r3(k_p), r3(v_p), r3(rr_p), r3(g1_p), wg2, bg2, gng,
                       jnp.zeros((bp, C_HEADS, C_DK, C_DV), f32))
    xp = _mix_ffn([o_c_p.reshape(bp * t, -1)], wo1, xp, ln(1, 0), ffn_w(1), ln(1, 1))

    q_s, k_s, v_s, g1_s, rr_s = _proj(xs, [wq, wk, wv, wg1, wr])
    o_c_s, gs_s = _gla(_pack_rows(q_s, bs), _pack_rows(k_s, bs), _pack_rows(v_s, bs), _pack_rows(rr_s, bs),
                       _pack_rows(g1_s, bs), wg2, bg2, gng, _pack_state(state_gla_S[0], bs), live_row=SAMPLE_ROW)
    gs_s = gs_s.reshape(state_gla_S[0].shape)
    xs = _mix_ffn([_unpack_rows(o_c_s, bs)], wo1, xs, ln(1, 0), ffn_w(1), ln(1, 1))

    return (xp.reshape(bp, t, d), xs.reshape(bs, 1, d),
            nsa_rows_p, nsa_win_p, delta_conv_p, ds_p[None], gs_p[None],
            nsa_rows_s, nsa_win_s, delta_conv_s, ds_s[None], gs_s[None])
```

```python
import functools
import math

import jax
import jax.numpy as jnp
from jax import lax
from jax.experimental import pallas as pl
from jax.experimental.pallas import tpu as pltpu

f32 = jnp.float32
bf16 = jnp.bfloat16

D_MODEL = 1024
DEPTH = 2
PAGE_SIZE = 128

A_HEADS = 4
A_DK = 128
A_DV = 128
A_QK = A_HEADS * A_DK
A_WIDTH = A_HEADS * A_DV
A_CONV_CH = 2 * A_QK + A_WIDTH
CONV_W = 4
DELTA_CHUNK = 64

B_HEADS = 8
B_KV = 2
B_HPG = B_HEADS // B_KV
B_DH = 64
B_WIDTH = B_HEADS * B_DH
N_BRANCH = 3
CMP_BLK = 32
SEL_BLK = 64
SEL_TOPK = 16
WINDOW = 512
B_SCALE = B_DH ** -0.5

C_HEADS = 4
C_DK = 128
C_DV = 256
C_QK = C_HEADS * C_DK
C_WIDTH = C_HEADS * C_DV
C_LOWRANK = 16
GLA_TAU = 16.0
GLA_CHUNK = 64
GLA_SUB = 16
GLA_PREP_CHUNKS = 4

D_FF = -(-(8 * D_MODEL) // (3 * 256)) * 256
ALPHA = (2 * DEPTH) ** 0.25
NEG = -1e30
FORCE = 1e9
LOG2E = 1.4426950408889634

LANES = 128
VMEM_LIMIT = 56 * 1024 * 1024


def _cparams(sem):
    return pltpu.CompilerParams(dimension_semantics=sem, vmem_limit_bytes=VMEM_LIMIT)


def _dot(a, b, precision=None):
    return jnp.dot(a, b, preferred_element_type=f32, precision=precision)


def _dot_nt(a, b, precision=None):
    return lax.dot_general(a, b, (((1,), (1,)), ((), ())), preferred_element_type=f32, precision=precision)


def _dot_tn(a, b, precision=None):
    return lax.dot_general(a, b, (((0,), (0,)), ((), ())), preferred_element_type=f32, precision=precision)


def _silu(x):
    return x * jax.nn.sigmoid(x)


def _layer_norm(x, g, b, eps=1e-5):
    mu = jnp.mean(x, -1, keepdims=True)
    xc = x - mu
    var = jnp.mean(xc * xc, -1, keepdims=True)
    return xc * lax.rsqrt(var + eps) * g + b


def _row_tile(m, pref):
    t = min(pref, m)
    assert m % t == 0
    return t


def _proj_kernel(x_ref, *refs, n, planes_of):
    xb = x_ref[...].astype(bf16)
    tm = x_ref.shape[0]
    for idx, (w_ref, o_ref) in enumerate(zip(refs[:n], refs[n:2 * n])):
        val = _dot(xb, w_ref[...])
        o_ref[...] = val
        if idx == planes_of:
            p_ref = refs[2 * n]
            n_planes = val.shape[1] // B_DH
            for j in range(n_planes):
                p_ref[pl.ds(j, tm, stride=n_planes), :] = val[:, j * B_DH:(j + 1) * B_DH]


def _proj(x, weights, tm=512, planes_of=None):
    m, k = x.shape
    tm = _row_tile(m, tm)
    in_specs = [pl.BlockSpec((tm, k), lambda i: (i, 0))]
    in_specs += [pl.BlockSpec(w.shape, lambda i: (0, 0)) for w in weights]
    out_specs = [pl.BlockSpec((tm, w.shape[1]), lambda i: (i, 0)) for w in weights]
    out_shape = [jax.ShapeDtypeStruct((m, w.shape[1]), f32) for w in weights]
    if planes_of is not None:
        n_planes = weights[planes_of].shape[1] // B_DH
        out_specs.append(pl.BlockSpec((tm * n_planes, B_DH), lambda i: (i, 0)))
        out_shape.append(jax.ShapeDtypeStruct((m * n_planes, B_DH), f32))
    return pl.pallas_call(
        functools.partial(_proj_kernel, n=len(weights), planes_of=planes_of),
        grid=(m // tm,), in_specs=in_specs, out_specs=out_specs, out_shape=out_shape,
        compiler_params=_cparams(("parallel",)), name="in_proj")(x, *weights)


FFN_COLS = 256


def _mix_ffn_kernel(*refs, n_parts):
    parts = refs[:n_parts]
    ws = refs[n_parts:2 * n_parts]
    x_ref, g0_ref, b0_ref, wg_ref, wu_ref, wd_ref, g1_ref, b1_ref, o_ref = refs[2 * n_parts:]
    y = None
    for p_ref, w_ref in zip(parts, ws):
        d = _dot(p_ref[...].astype(bf16), w_ref[...])
        y = d if y is None else y + d
    x = _layer_norm(ALPHA * x_ref[...] + y, g0_ref[...], b0_ref[...])
    xb = x.astype(bf16)
    acc = jnp.zeros(x.shape, f32)
    for c in range(0, D_FF, FFN_COLS):
        hg = _dot(xb, wg_ref[:, c:c + FFN_COLS])
        hu = _dot(xb, wu_ref[:, c:c + FFN_COLS])
        h = (_silu(hg) * hu).astype(bf16)
        acc = acc + _dot(h, wd_ref[c:c + FFN_COLS, :])
    o_ref[...] = _layer_norm(ALPHA * x + acc, g1_ref[...], b1_ref[...])


def _mix_ffn(parts, weights, x, ln0, ffn_w, ln1, tm=512):
    m, d = x.shape
    tm = _row_tile(m, tm)
    n = len(parts)
    row = lambda w: pl.BlockSpec((tm, w), lambda i: (i, 0))
    const = lambda a: pl.BlockSpec(a.shape, lambda i: (0, 0), pipeline_mode=pl.Buffered(1))
    in_specs = [row(p.shape[1]) for p in parts] + [const(w) for w in weights] + [row(d)]
    in_specs += [const(a) for a in (*ln0, *ffn_w, *ln1)]
    return pl.pallas_call(
        functools.partial(_mix_ffn_kernel, n_parts=n), grid=(m // tm,), in_specs=in_specs,
        out_specs=row(d), out_shape=jax.ShapeDtypeStruct((m, d), f32),
        compiler_params=_cparams(("parallel",)), name="mix_ffn")(*parts, *weights, x, *ln0, *ffn_w, *ln1)


DELTA_INV_BLK = 16


def _tri_masks(c):
    r = lax.broadcasted_iota(jnp.int32, (c, c), 0)
    s = lax.broadcasted_iota(jnp.int32, (c, c), 1)
    return r, s


def _split2(x):
    hi = x.astype(bf16)
    return hi, (x - hi.astype(f32)).astype(bf16)


def _split3(x):
    hi = x.astype(bf16)
    rest = x - hi.astype(f32)
    mid = rest.astype(bf16)
    return hi, mid, (rest - mid.astype(f32)).astype(bf16)


def _dot_x3(a, b):
    ah, al = _split2(a)
    bh, bl = _split2(b)
    return _dot(ah, bh) + (_dot(ah, bl) + _dot(al, bh))


def _dot_ones(a_ones, b):
    a16 = a_ones.astype(bf16)
    return _dot(jnp.concatenate([a16] * 3, axis=1), jnp.concatenate(_split3(b), axis=0))


def _unit_lower_inverses(mats, r, s):
    c = mats[0].shape[0]
    eye = (r == s).astype(f32)
    same_blk = (r // DELTA_INV_BLK) == (s // DELTA_INV_BLK)
    diag = [jnp.where(same_blk, a, 0.0) for a in mats]
    low = [a - d for a, d in zip(mats, diag)]
    pw = [-d for d in diag]
    p = [eye + x for x in pw]
    k = 2
    while k < DELTA_INV_BLK:
        pw = [_dot_x3(x, x) for x in pw]
        p = [pi + _dot_x3(pi, x) for pi, x in zip(p, pw)]
        k *= 2
    pw = [_dot_x3(pi, lo) for pi, lo in zip(p, low)]
    q = [eye - x for x in pw]
    k = 2
    while k < c // DELTA_INV_BLK:
        pw = [_dot_x3(x, x) for x in pw]
        q = [qi + _dot_x3(qi, x) for qi, x in zip(q, pw)]
        k *= 2
    return [_dot_x3(qi, pi) for qi, pi in zip(q, p)]


DELTA_PREP_CHUNKS = 2
DELTA_UNROLL_GROUPS = 4


def _delta_kernel(qkv_ref, small_ref, gate_ref, cbuf_ref, cw_ref, alog_ref, dtb_ref, dng_ref, s0_ref,
                  o_ref, s_ref, xbuf, ybuf, tail, u_ref, w_ref, qg_ref, kd_ref, qk_ref, gl_ref,
                  *, tt, live_row):
    t = pl.program_id(1)
    c = DELTA_CHUNK
    n_chunks = tt // c
    cpi = math.gcd(DELTA_PREP_CHUNKS, n_chunks)

    @pl.when(t == 0)
    def _():
        s_ref[...] = s0_ref[...]
        tail[...] = cbuf_ref[...]

    xbuf[0:8, :] = tail[...]
    xbuf[8:8 + tt, :] = qkv_ref[...]
    tail[...] = xbuf[tt:tt + 8, :]

    def row0(chunk):
        return chunk * c if isinstance(chunk, int) else pl.multiple_of(chunk * c, c)

    def conv_group(gi):
        for cc in range(cpi):
            r0 = row0(gi * cpi + cc)
            for col in range(0, A_CONV_CH, A_QK):
                cols = slice(col, col + A_QK)
                x = xbuf[pl.ds(r0, c + 8), cols]
                first = 8 - (CONV_W - 1)
                y = x[first:first + c] * cw_ref[0:1, cols]
                for j in range(1, CONV_W):
                    y = y + x[first + j:first + j + c] * cw_ref[j:j + 1, cols]
                ybuf[pl.ds(r0, c), cols] = _silu(y)

    r, s = _tri_masks(c)
    causal = r >= s
    strict = r > s
    lower_ones = causal.astype(f32)
    upper4 =jnp.concatenate([(r <= s).astype(f32)] * A_HEADS, axis=1)
    all_ones = jnp.ones((c, c), f32)
    alog = alog_ref[...]
    dtb = dtb_ref[...]
    dng = dng_ref[...]

    def prep_group(gi):
        items = []
        for cc in range(cpi):
            r0 = row0(gi * cpi + cc)
            sm = small_ref[pl.ds(r0, c), :]
            beta_all = jax.nn.sigmoid(sm)
            g_all = -jnp.exp(alog) * jax.nn.softplus(sm + dtb)
            if live_row is not None:
                live = lax.broadcasted_iota(jnp.int32, (c, 1), 0) == live_row
                beta_all = jnp.where(live, beta_all, 0.0)
                g_all = jnp.where(live, g_all, 0.0)
            gam_c_all = _dot_ones(lower_ones, g_all)
            g_rows = jnp.concatenate(
                [jnp.broadcast_to(g_all[:, A_HEADS + h:A_HEADS + h + 1], (c, c)) for h in range(A_HEADS)], axis=1)
            gam_r_all = _dot_ones(all_ones, g_rows * upper4)
            for h in range(A_HEADS):
                qh = ybuf[pl.ds(r0, c), h * A_DK:(h + 1) * A_DK]
                kh = ybuf[pl.ds(r0, c), A_QK + h * A_DK:A_QK + (h + 1) * A_DK]
                vh = ybuf[pl.ds(r0, c), 2 * A_QK + h * A_DV:2 * A_QK + (h + 1) * A_DV]
                qh = qh * lax.rsqrt(jnp.sum(qh * qh, -1, keepdims=True) + 1e-6) * (A_DK ** -0.5)
                kh = kh * lax.rsqrt(jnp.sum(kh * kh, -1, keepdims=True) + 1e-6)
                if live_row is not None:
                    qh = jnp.where(live, qh, 0.0)
                    kh = jnp.where(live, kh, 0.0)
                    vh = jnp.where(live, vh, 0.0)
                beta = beta_all[:, h:h + 1]
                gam_c = jnp.broadcast_to(gam_c_all[:, A_HEADS + h:A_HEADS + h + 1], (c, A_DK))
                diff = gam_c[:, :c] - gam_r_all[:, h * c:(h + 1) * c]
                ldec = jnp.where(causal, jnp.exp(jnp.where(causal, diff, 0.0)), 0.0)
                items.append(dict(r0=r0, h=h, q=qh, k=kh, v=vh, beta=beta, gam=gam_c, ldec=ldec, kb=kh * beta))
        amats = [jnp.where(strict, _dot_nt(it["kb"], it["k"]) * it["ldec"], 0.0) for it in items]
        tms = _unit_lower_inverses(amats, r, s)
        for it, tm in zip(items, tms):
            r0, h = it["r0"], it["h"]
            cols = slice(h * A_DK, (h + 1) * A_DK)
            eg = jnp.exp(it["gam"])
            g_last = it["gam"][c - 1:c, :]
            u_ref[pl.ds(r0, c), cols] = _dot(tm, it["v"] * it["beta"])
            w_ref[pl.ds(r0, c), cols] = _dot(tm, it["kb"] * eg)
            qg_ref[pl.ds(r0, c), cols] = it["q"] * eg
            kd_ref[pl.ds(r0, c), cols] = it["k"] * jnp.exp(g_last - it["gam"])
            qk_ref[pl.ds(r0, c), h * c:(h + 1) * c] = _dot_nt(it["q"], it["k"]) * it["ldec"]
            gl_ref[pl.ds(r0, 1), cols] = jnp.exp(g_last)

    def scan_group(gi):
        for cc in range(cpi):
            chunk = gi * cpi + cc
            r0 = row0(chunk)
            for h in range(A_HEADS):
                cols = slice(h * A_DK, (h + 1) * A_DK)
                sidx = (h,) if live_row is None else (chunk, h)
                st = s_ref[sidx]
                v_new = u_ref[pl.ds(r0, c), cols] - _dot(w_ref[pl.ds(r0, c), cols], st)
                o = _dot(qg_ref[pl.ds(r0, c), cols], st) + _dot(qk_ref[pl.ds(r0, c), h * c:(h + 1) * c], v_new)
                s_ref[sidx] = st * gl_ref[pl.ds(r0, 1), cols] + _dot_tn(kd_ref[pl.ds(r0, c), cols], v_new)
                o = o * lax.rsqrt(jnp.mean(o * o, -1, keepdims=True) + 1e-6) * dng
                gt = gate_ref[pl.ds(r0, c), h * A_DV:(h + 1) * A_DV]
                o_ref[pl.ds(r0, c), h * A_DV:(h + 1) * A_DV] = o * _silu(gt)

    n_groups = n_chunks // cpi
    conv_group(0)
    if n_groups > 1:
        prep_group(0)
        conv_group(1)

        def steady(gi, carry):
            scan_group(gi - 1)
            prep_group(gi)
            conv_group(gi + 1)
            return carry

        if n_groups <= DELTA_UNROLL_GROUPS:
            for gi in range(1, n_groups - 1):
                steady(gi, 0)
        else:
            lax.fori_loop(1, n_groups - 1, steady, 0)
        scan_group(n_groups - 2)
    prep_group(n_groups - 1)
    scan_group(n_groups - 1)


def _delta(qkv, small, gate, conv_buf8, conv_w, alog, dtb, dng, s0, *, live_row=None, tt=512):
    b, t_total, _ = qkv.shape
    tt = _row_tile(t_total, tt)
    nt = t_total // tt
    assert live_row is None or (nt == 1 and s0.shape[1] == tt // DELTA_CHUNK)
    row = lambda w: pl.BlockSpec((None, tt, w), lambda i, j: (i, j, 0))
    full2 = lambda a: pl.BlockSpec(a.shape, lambda i, j: (0, 0))
    st_spec = pl.BlockSpec((None,) + s0.shape[1:], lambda i, j: (i,) + (0,) * (s0.ndim - 1))
    kern = functools.partial(_delta_kernel, tt=tt, live_row=live_row)
    return pl.pallas_call(
        kern, grid=(b, nt),
        in_specs=[row(A_CONV_CH), row(LANES), row(A_WIDTH),
                  pl.BlockSpec((None, 8, A_CONV_CH), lambda i, j: (i, 0, 0)),
                  full2(conv_w), full2(alog), full2(dtb), full2(dng), st_spec],
        out_specs=[row(A_WIDTH), st_spec],
        out_shape=[jax.ShapeDtypeStruct((b, t_total, A_WIDTH), f32), jax.ShapeDtypeStruct(s0.shape, f32)],
        scratch_shapes=[pltpu.VMEM((tt + 8, A_CONV_CH), f32), pltpu.VMEM((tt, A_CONV_CH), f32),
                        pltpu.VMEM((8, A_CONV_CH), f32)]
        + [pltpu.VMEM((tt, A_QK), f32)] * 4
        + [pltpu.VMEM((tt, A_HEADS * DELTA_CHUNK), f32), pltpu.VMEM((tt, A_QK), f32)],
        compiler_params=_cparams(("parallel", "arbitrary")), name="delta_mixer",
    )(qkv, small, gate, conv_buf8, conv_w, alog, dtb, dng, s0)


def _gla_kernel(q_ref, k_ref, v_ref, r_ref, g1_ref, wg2_ref, bg2_ref, gng_ref, s0_ref,
                o_ref, s_ref, st_ref, la_ref, qe_ref, gl_ref, ds_ref, *, tt, live_row):
    t = pl.program_id(1)
    nt = pl.num_programs(1)
    c = GLA_CHUNK
    heads = range(C_HEADS)
    kcols = [slice(h * C_DK, (h + 1) * C_DK) for h in heads]
    vcols = [slice(h * C_DV, (h + 1) * C_DV) for h in heads]
    states = [(h,) for h in heads] if live_row is None else [(cs, h) for cs in range(tt // c) for h in heads]

    @pl.when(t == 0)
    def _():
        for sidx in states:
            st_ref[sidx] = s0_ref[sidx].T

    la = jax.nn.log_sigmoid(_dot(g1_ref[...].astype(bf16), wg2_ref[...]) + bg2_ref[...]) / GLA_TAU
    if live_row is not None:
        la = jnp.where(lax.broadcasted_iota(jnp.int32, (tt, 1), 0) % c == live_row, la, 0.0)
    la_ref[...] = la

    r, s = _tri_masks(c)
    lower_ones = (r >= s).astype(f32)
    rs = lax.broadcasted_iota(jnp.int32, (GLA_SUB, c), 0)
    ss = lax.broadcasted_iota(jnp.int32, (GLA_SUB, c), 1)
    krow = lax.broadcasted_iota(jnp.int32, (c, 1), 0)
    gng = gng_ref[...]

    n_chunks = tt // c
    cpi = math.gcd(GLA_PREP_CHUNKS, n_chunks)

    def row0(chunk):
        return chunk * c if isinstance(chunk, int) else pl.multiple_of(chunk * c, c)

    def prep_group(gi):
        items = []
        for cc in range(cpi):
            chunk = gi * cpi + cc
            r0 = row0(chunk)
            rows = pl.ds(r0, c)
            bc_all = _dot_ones(lower_ones, la_ref[rows, :])
            for h in heads:
                items.append(dict(chunk=chunk, r0=r0, rows=rows, h=h, q=q_ref[rows, kcols[h]] * (C_DK ** -0.5),
                                  k=k_ref[rows, kcols[h]], v=v_ref[rows, vcols[h]], bc=bc_all[:, kcols[h]]))
        for it in items:
            qe_ref[it["rows"], kcols[it["h"]]] = it["q"] * jnp.exp(it["bc"])
        for a in range(c // GLA_SUB):
            lo, hi = a * GLA_SUB, (a + 1) * GLA_SUB
            seen = krow < hi
            att = []
            for it in items:
                bref = it["bc"][lo - 1:lo, :] if a > 0 else jnp.zeros((1, C_DK), f32)
                qa = it["q"][lo:hi, :] * jnp.exp(it["bc"][lo:hi, :] - bref)
                ka = jnp.where(seen, it["k"] * jnp.exp(jnp.where(seen, bref - it["bc"], 0.0)), 0.0)
                att.append(jnp.where(rs + lo >= ss, _dot_nt(qa, ka), 0.0))
            for it, at in zip(items, att):
                o_ref[pl.ds(it["r0"] + lo, GLA_SUB), vcols[it["h"]]] = _dot(at, it["v"])
        for it in items:
            b_last = it["bc"][c - 1:c, :]
            gl_ref[pl.ds(it["r0"], 1), kcols[it["h"]]] = jnp.exp(b_last)
            ds_ref[it["chunk"], it["h"]] = _dot_tn(it["v"], it["k"] * jnp.exp(b_last - it["bc"]))

    def scan_group(gi):
        for cc in range(cpi):
            chunk = gi * cpi + cc
            r0 = row0(chunk)
            rows = pl.ds(r0, c)
            for h in heads:
                sidx = (h,) if live_row is None else (chunk, h)
                st = st_ref[sidx]
                o = _dot_nt(qe_ref[rows, kcols[h]], st) + o_ref[rows, vcols[h]]
                st_ref[sidx] = st * gl_ref[pl.ds(r0, 1), kcols[h]] + ds_ref[chunk, h]
                o = o * lax.rsqrt(jnp.mean(o * o, -1, keepdims=True) + 1e-6) * gng
                o_ref[rows, vcols[h]] = o * _silu(r_ref[rows, vcols[h]])

    n_groups = n_chunks // cpi
    prep_group(0)

    def steady(gi, carry):
        scan_group(gi - 1)
        prep_group(gi)
        return carry

    lax.fori_loop(1, n_groups, steady, 0)
    scan_group(n_groups - 1)

    @pl.when(t == nt - 1)
    def _():
        for sidx in states:
            s_ref[sidx] = st_ref[sidx].T


def _gla(q, k, v, r, g1, wg2, bg2, gng, s0, *, live_row=None, tt=512):
    b, t_total, _ = q.shape
    tt = _row_tile(t_total, tt)
    nt = t_total // tt
    assert live_row is None or (nt == 1 and s0.shape[1] == tt // GLA_CHUNK)
    row = lambda w: pl.BlockSpec((None, tt, w), lambda i, j: (i, j, 0))
    full2 = lambda a: pl.BlockSpec(a.shape, lambda i, j: (0, 0))
    st_spec = pl.BlockSpec((None,) + s0.shape[1:], lambda i, j: (i,) + (0,) * (s0.ndim - 1))
    kern = functools.partial(_gla_kernel, tt=tt, live_row=live_row)
    return pl.pallas_call(
        kern, grid=(b, nt),
        in_specs=[row(C_QK), row(C_QK), row(C_WIDTH), row(C_WIDTH), row(LANES),
                  full2(wg2), full2(bg2), full2(gng), st_spec],
        out_specs=[row(C_WIDTH), st_spec],
        out_shape=[jax.ShapeDtypeStruct((b, t_total, C_WIDTH), f32), jax.ShapeDtypeStruct(s0.shape, f32)],
        scratch_shapes=[pltpu.VMEM(s0.shape[1:-2] + (C_DV, C_DK), f32), pltpu.VMEM((tt, C_QK), f32),
                        pltpu.VMEM((tt, C_QK), f32), pltpu.VMEM((tt, C_QK), f32),
                        pltpu.VMEM((tt // GLA_CHUNK, C_HEADS, C_DV, C_DK), f32)],
        compiler_params=_cparams(("parallel", "arbitrary")), name="gla_mixer",
    )(q, k, v, r, g1, wg2, bg2, gng, s0)


GB_LANE0 = 2 * A_HEADS


def _masked_softmax(s, mask):
    sm = jnp.where(mask, s, NEG)
    e = jnp.exp(sm - jnp.max(sm, -1, keepdims=True))
    p = e / jnp.sum(e, -1, keepdims=True)
    return jnp.where(mask, p, 0.0)


def _topk_rank(score, n_cand):
    lane = lax.broadcasted_iota(jnp.int32, score.shape, score.ndim - 1)
    rank = jnp.zeros(score.shape, jnp.int32)
    for i in range(n_cand):
        si = score[..., i:i + 1]
        ahead = (si > score) | ((si == score) & (i < lane))
        rank = rank + ahead.astype(jnp.int32)
    return rank


def _block_scores(imp, qpos):
    blk = lax.broadcasted_iota(jnp.int32, imp.shape, imp.ndim - 1)
    cur = qpos // SEL_BLK
    valid = blk * SEL_BLK <= qpos
    forced = (blk == 0) | (blk == cur) | (blk == cur - 1)
    return jnp.where(forced, FORCE, jnp.where(valid, imp, NEG))


def _split_cmp_index(col, half):
    return jnp.where(col < half, 2 * col, 2 * (col - half) + 1)


NSA_TQ = 128
NSA_QSUB = 2
NSA_TK = 512
NSA_SUB = 128


def _topk_mask_rows(score, k):
    n = score.shape[0]
    row = lax.broadcasted_iota(jnp.int32, score.shape, 0)
    rank = jnp.zeros(score.shape, jnp.int32)
    for i in range(n):
        si = score[i:i + 1, :]
        ahead = (si > score) | ((si == score) & (i < row))
        rank = rank + ahead.astype(jnp.int32)
    return (rank < k).astype(f32)


def _nsa_prompt_kernel(q_ref, rows_ref, win_ref, small_ref, o_ref, cb_ref, sel_ref, sc_ref, *, t_total):
    i = pl.program_id(1)
    tq = NSA_TQ
    tk = min(NSA_TK, t_total)
    tw = min(WINDOW + tq, t_total)
    n_sel = t_total // SEL_BLK
    half = n_sel
    hq = B_HPG * tq
    blk_per_tile = tk // SEL_BLK
    kv_w = B_KV * B_DH

    @pl.when(i == 0)
    def _():
        x = rows_ref[:, 0:2 * kv_w].reshape(n_sel, SEL_BLK, 2 * kv_w)
        cb_ref[0:half, :] = jnp.sum(x[:, :CMP_BLK, :], axis=1) * (1.0 / CMP_BLK)
        cb_ref[half:2 * half, :] = jnp.sum(x[:, CMP_BLK:, :], axis=1) * (1.0 / CMP_BLK)

    n_qsub = o_ref.shape[0] // tq
    chains = [(s, g) for s in range(n_qsub) for g in range(B_KV)]
    tile0 = [(i * n_qsub + s) * tq for s in range(n_qsub)]
    qpos_s = [t0 + lax.broadcasted_iota(jnp.int32, (1, tq), 1) for t0 in tile0]
    gates_t = jax.nn.sigmoid(small_ref[...].T)
    n_causal = (tile0[-1] + tq - 1) // tk + 1
    q4 = [jnp.concatenate([q_ref[s * tq:(s + 1) * tq, (g * B_HPG + h) * B_DH:(g * B_HPG + h + 1) * B_DH]
                           for h in range(B_HPG)], axis=0) * (B_SCALE * LOG2E)
          for s, g in chains]

    o_cmp = []
    for c, (s, g) in enumerate(chains):
        qpos = qpos_s[s]
        qpos4 = jnp.concatenate([qpos] * B_HPG, axis=1)
        kc = cb_ref[:, g * B_DH:(g + 1) * B_DH]
        vc = cb_ref[:, (B_KV + g) * B_DH:(B_KV + g + 1) * B_DH]
        st = _dot_nt(kc, q4[c])
        cidx = _split_cmp_index(lax.broadcasted_iota(jnp.int32, (2 * half, 1), 0), half)
        cmask = (cidx + 1) * CMP_BLK <= qpos4 + 1
        sm = jnp.where(cmask, st, NEG)
        e = jnp.exp2(sm - jnp.max(sm, axis=0, keepdims=True))
        p = jnp.where(cmask, e / jnp.sum(e, axis=0, keepdims=True), 0.0)
        o_cmp.append(_dot_tn(vc, p))
        imp = p[:, 0:tq]
        for h in range(1, B_HPG):
            imp = imp + p[:, h * tq:(h + 1) * tq]
        imp = imp[:half] + imp[half:]
        blk = lax.broadcasted_iota(jnp.int32, (n_sel, 1), 0)
        cur = qpos // SEL_BLK
        forced = (blk == 0) | (blk == cur) | (blk == cur - 1)
        score = jnp.where(forced, FORCE, jnp.where(blk * SEL_BLK <= qpos, imp, NEG))
        picked = (_topk_mask_rows(score, min(SEL_TOPK, n_sel)) > 0.5) & (blk * SEL_BLK <= qpos)
        sel_ref[c] = jnp.where(picked, 0.0, NEG)

    sub = min(NSA_SUB, tk)

    def attend(c, kv_ref, k0, n_keys, kl, vl, bias_fn, carry):
        m, acc = carry
        m_new = m
        for u in range(n_keys // sub):
            k = kv_ref[pl.ds(k0 + u * sub, sub), kl:kl + B_DH]
            bias = bias_fn(u)
            sc = _dot_nt(k, q4[c]) + jnp.concatenate([bias] * B_HPG, axis=1)
            sc_ref[c, u * sub:(u + 1) * sub, :] = sc
            m_new = jnp.maximum(m_new, jnp.max(sc, axis=0, keepdims=True))
        acc = jnp.exp2(m - m_new) * acc
        for u in range(n_keys // sub):
            pe = jnp.exp2(sc_ref[c, u * sub:(u + 1) * sub, :] - m_new)
            v1 = jnp.concatenate([kv_ref[pl.ds(k0 + u * sub, sub), vl:vl + B_DH], ones_blk], axis=1)
            acc = acc + _dot_tn(v1, pe)
        return m_new, acc

    ones_blk = jnp.ones((sub, B_DH), f32)
    init = (jnp.full((1, hq), NEG, f32), jnp.zeros((2 * B_DH, hq), f32))
    srow = lax.broadcasted_iota(jnp.int32, (sub, 1), 0)
    blk_per_sub = sub // SEL_BLK
    assert sub == tq
    own_block_bias = jnp.where(srow <= lax.broadcasted_iota(jnp.int32, (1, tq), 1), 0.0, NEG)

    def sel_step(j, carries):
        k0 = j * tk
        out = []
        for c, (s, g) in enumerate(chains):
            blk_bias = sel_ref[c, j * blk_per_tile:(j + 1) * blk_per_tile, :]

            def bias_fn(u, blk_bias=blk_bias, s=s):
                rows = jnp.concatenate(
                    [jnp.broadcast_to(blk_bias[u * blk_per_sub + v:u * blk_per_sub + v + 1, :], (SEL_BLK, tq))
                     for v in range(blk_per_sub)], axis=0)
                return rows + jnp.where(k0 + u * sub == tile0[s], own_block_bias, 0.0)

            out.append(attend(c, rows_ref, k0, tk, (2 * B_KV + g) * B_DH, (3 * B_KV + g) * B_DH, bias_fn,
                              carries[c]))
        return tuple(out)

    def tiles_variant(n):
        def run():
            carries = (init,) * len(chains)
            for j in range(n):
                carries = sel_step(j, carries)
            return carries
        return run

    sel_out = lax.switch(n_causal - 1, [tiles_variant(n) for n in range(1, t_total // tk + 1)])
    w0 = [pl.multiple_of(jnp.maximum(t0 + tq - tw, 0), tq) for t0 in tile0]
    win_bias = []
    for s in range(n_qsub):
        per_sub = []
        for u in range(tw // sub):
            d = qpos_s[s] - (w0[s] + u * sub + srow)
            per_sub.append(jnp.where((d >= 0) & (d < WINDOW), 0.0, NEG))
        win_bias.append(per_sub)

    win_out = [attend(c, win_ref, w0[s], tw, g * B_DH, (B_KV + g) * B_DH, lambda u, s=s: win_bias[s][u], init)
               for c, (s, g) in enumerate(chains)]
    for s in range(n_qsub):
        out_rows = []
        for c, (cs, g) in enumerate(chains):
            if cs != s:
                continue
            acc_w, acc_s = win_out[c][1], sel_out[c][1]
            o_win = acc_w[:B_DH] / acc_w[B_DH:B_DH + 1]
            o_sel = acc_s[:B_DH] / acc_s[B_DH:B_DH + 1]
            gt = gates_t[:, s * tq:(s + 1) * tq]
            for h in range(B_HPG):
                c0 = GB_LANE0 + (g * B_HPG + h) * N_BRANCH
                ls = slice(h * tq, (h + 1) * tq)
                out_rows.append(gt[c0:c0 + 1, :] * o_cmp[c][:, ls] + gt[c0 + 1:c0 + 2, :] * o_sel[:, ls]
                                + gt[c0 + 2:c0 + 3, :] * o_win[:, ls])
        o_ref[s * tq:(s + 1) * tq, :] = jnp.concatenate(out_rows, axis=0).T


def _nsa_prompt(qb, rows, win, small):
    b, t_total, _ = qb.shape
    tk = min(NSA_TK, t_total)
    rows_per_step = NSA_TQ * NSA_QSUB
    assert t_total % tk == 0 and tk % NSA_TQ == 0 and tk % SEL_BLK == 0 and WINDOW % NSA_TQ == 0
    assert t_total % rows_per_step == 0
    nq = t_total // rows_per_step
    n_sel = t_total // SEL_BLK
    n_chains = NSA_QSUB * B_KV
    tile = lambda w: pl.BlockSpec((None, rows_per_step, w), lambda bi, i: (bi, i, 0))
    seq = lambda w: pl.BlockSpec((None, t_total, w), lambda bi, i: (bi, 0, 0))
    return pl.pallas_call(
        functools.partial(_nsa_prompt_kernel, t_total=t_total), grid=(b, nq),
        in_specs=[tile(B_WIDTH), seq(4 * B_KV * B_DH), seq(2 * B_KV * B_DH), tile(LANES)],
        out_specs=tile(B_WIDTH), out_shape=jax.ShapeDtypeStruct((b, t_total, B_WIDTH), f32),
        scratch_shapes=[pltpu.VMEM((2 * n_sel, 2 * B_KV * B_DH), f32), pltpu.VMEM((n_chains, n_sel, NSA_TQ), f32),
                        pltpu.VMEM((n_chains, max(tk, min(WINDOW + NSA_TQ, t_total)), B_HPG * NSA_TQ), f32)],
        compiler_params=_cparams(("parallel", "arbitrary")), name="nsa_prompt",
    )(qb, rows, win, small)


DECODE_POOL_PAGES = 8


def _nsa_decode_cmp_kernel(pt_ref, q_ref, wnew_ref, wbuf_ref, pool_ref, cache_ref, ocmp_ref, owin_ref, idx_ref,
                           pages, acc_ref, sem, *, n_pages, past):
    b = pl.program_id(0)
    nb = pl.num_programs(0)
    slot = b % 2
    n_hist_blk = past // SEL_BLK
    n_sel = -(-(past + 1) // SEL_BLK)
    half = n_hist_blk
    lanes_sel = 2 * half
    wb = wbuf_ref.shape[0]
    n_cmp_rows = 2 * B_KV * B_DH

    def page_copy(bb, p, sl):
        return pltpu.make_async_copy(cache_ref.at[pt_ref[bb, p], pl.ds(0, 2)], pages.at[sl, p], sem.at[sl])

    def fetch(bb, sl):
        for p in range(n_pages):
            page_copy(bb, p, sl).start()

    @pl.when(b == 0)
    def _():
        fetch(0, 0)

    @pl.when(b + 1 < nb)
    def _():
        fetch(b + 1, 1 - slot)

    for p in range(n_pages):
        page_copy(b, p, slot).wait()

    acc_ref[...] = jnp.zeros(acc_ref.shape, f32)
    ppc = math.gcd(n_pages, DECODE_POOL_PAGES)

    def pool_body(c, carry):
        x = jnp.concatenate([pages[slot, c * ppc + i].reshape(n_cmp_rows, PAGE_SIZE) for i in range(ppc)], axis=1)
        hi, lo = _split2(x)
        w = pool_ref[pl.ds(pl.multiple_of(c * ppc * PAGE_SIZE, ppc * PAGE_SIZE), ppc * PAGE_SIZE), :]
        acc_ref[...] += _dot(hi, w) + _dot(lo, w)
        return carry

    lax.fori_loop(0, n_pages // ppc, pool_body, 0)

    qpos = past
    row8 = lax.broadcasted_iota(jnp.int32, (B_HEADS, 1), 0)
    q8 = jnp.concatenate([q_ref[:, h * B_DH:(h + 1) * B_DH] for h in range(B_HEADS)], axis=0) * B_SCALE
    wnew = wnew_ref[...]
    lane = lax.broadcasted_iota(jnp.int32, (1, lanes_sel), 1)
    o_cmp, o_win, imps = [], [], []
    for g in range(B_KV):
        in_g = (row8 // B_HPG) == g
        kc_t = acc_ref[g * B_DH:(g + 1) * B_DH, :]
        vc_t = acc_ref[(B_KV + g) * B_DH:(B_KV + g + 1) * B_DH, :]
        sc = _dot(q8, kc_t)
        cidx = _split_cmp_index(lane, half)
        p = _masked_softmax(sc, (cidx + 1) * CMP_BLK <= qpos + 1)
        o_cmp.append(_dot_nt(p, vc_t))
        imp = jnp.sum(jnp.where(in_g, p, 0.0), axis=0, keepdims=True)
        imp = imp[:, :half] + imp[:, half:]
        imps.append(jnp.concatenate([imp, jnp.zeros((1, lanes_sel - half), f32)], axis=1))
        kw = wbuf_ref[:, g * B_DH:(g + 1) * B_DH]
        vw = wbuf_ref[:, (B_KV + g) * B_DH:(B_KV + g + 1) * B_DH]
        kw_new = wnew[:, g * B_DH:(g + 1) * B_DH]
        vw_new = wnew[:, (B_KV + g) * B_DH:(B_KV + g + 1) * B_DH]
        kpos = qpos - wb + lax.broadcasted_iota(jnp.int32, (1, wb), 1)
        d = qpos - kpos
        wmask = (d >= 0) & (d < WINDOW) & (kpos >= 0)
        sw = jnp.where(wmask, _dot_nt(q8, kw), NEG)
        swn = jnp.sum(q8 * kw_new, -1, keepdims=True)
        m = jnp.maximum(jnp.max(sw, -1, keepdims=True), swn)
        ew = jnp.where(wmask, jnp.exp(sw - m), 0.0)
        ewn = jnp.exp(swn - m)
        o_win.append((_dot(ew, vw) + ewn * vw_new) / (jnp.sum(ew, -1, keepdims=True) + ewn))
    rank = _topk_rank(_block_scores(jnp.concatenate(imps, axis=0), qpos), n_sel)
    lane_f = lane.astype(f32)
    out_lane = lax.broadcasted_iota(jnp.int32, (1, LANES), 1)
    picked = jnp.zeros((B_KV, LANES), f32)
    for r in range(min(SEL_TOPK, n_sel)):
        blk_r = jnp.sum(jnp.where((rank == r) & (lane < n_sel), lane_f, 0.0), axis=-1, keepdims=True)
        picked = picked + jnp.where(out_lane == r, blk_r, 0.0)
    lower = row8 < B_HPG
    ocmp_ref[...] = jnp.where(lower, o_cmp[0], o_cmp[1])
    owin_ref[...] = jnp.where(lower, o_win[0], o_win[1])
    idx_ref[...] = jnp.concatenate([picked, jnp.zeros((8 - B_KV, LANES), f32)], axis=0).astype(jnp.int32)


def _nsa_decode_sel_kernel(pt_ref, sel_ref, q_ref, rnew_ref, small_ref, ocmp_ref, owin_ref, cache_ref,
                           o_ref, blocks, sem, *, past):
    b = pl.program_id(0)
    nb = pl.num_programs(0)
    slot = b % 2
    n_hist_blk = past // SEL_BLK
    n_pick = sel_ref.shape[2]
    blk_per_page = PAGE_SIZE // SEL_BLK
    tok_lane = lax.broadcasted_iota(jnp.int32, (1, PAGE_SIZE), 1)

    def hist_block(bb, g, j):
        return jnp.minimum(sel_ref[bb, g, j], n_hist_blk - 1)

    def block_copy(bb, g, j, sl):
        page = pt_ref[bb, hist_block(bb, g, j) // blk_per_page]
        return pltpu.make_async_copy(cache_ref.at[page, pl.ds(2, 2), g], blocks.at[sl, g * n_pick + j], sem.at[sl])

    def fetch(bb, sl):
        for g in range(B_KV):
            for j in range(n_pick):
                block_copy(bb, g, j, sl).start()

    @pl.when(b == 0)
    def _():
        fetch(0, 0)

    @pl.when(b + 1 < nb)
    def _():
        fetch(b + 1, 1 - slot)

    for g in range(B_KV):
        for j in range(n_pick):
            block_copy(b, g, j, slot).wait()

    row8 = lax.broadcasted_iota(jnp.int32, (B_HEADS, 1), 0)
    q8 = jnp.concatenate([q_ref[:, h * B_DH:(h + 1) * B_DH] for h in range(B_HEADS)], axis=0) * B_SCALE
    rnew = rnew_ref[...]
    o_sel = []
    for g in range(B_KV):
        ks_t = jnp.concatenate([blocks[slot, g * n_pick + j, 0] for j in range(n_pick)], axis=1)
        vs_t = jnp.concatenate([blocks[slot, g * n_pick + j, 1] for j in range(n_pick)], axis=1)
        picks = [sel_ref[b, g, j] for j in range(n_pick)]
        live = jnp.concatenate(
            [(tok_lane // SEL_BLK == hist_block(b, g, j) % blk_per_page) & (picks[j] < n_hist_blk)
             for j in range(n_pick)], axis=1)
        new_live = functools.reduce(jnp.logical_or, [pk >= n_hist_blk for pk in picks])
        k_new = rnew[:, (2 * B_KV + g) * B_DH:(2 * B_KV + g + 1) * B_DH]
        v_new = rnew[:, (3 * B_KV + g) * B_DH:(3 * B_KV + g + 1) * B_DH]
        sh = jnp.where(live, _dot(q8, ks_t), NEG)
        sn = jnp.sum(q8 * k_new, -1, keepdims=True) + jnp.where(new_live, 0.0, NEG).astype(f32)
        m = jnp.maximum(jnp.max(sh, -1, keepdims=True), sn)
        eh = jnp.where(live, jnp.exp(sh - m), 0.0)
        en = jnp.where(new_live, jnp.exp(sn - m), 0.0)
        o_sel.append((_dot_nt(eh, vs_t) + en * v_new) / (jnp.sum(eh, -1, keepdims=True) + en))
    os_ = jnp.where(row8 < B_HPG, o_sel[0], o_sel[1])
    oc = ocmp_ref[...]
    ow = owin_ref[...]
    gates = jax.nn.sigmoid(small_ref[...])
    outs = []
    for hh in range(B_HEADS):
        c0 = GB_LANE0 + hh * N_BRANCH
        outs.append(gates[:, c0:c0 + 1] * oc[hh:hh + 1] + gates[:, c0 + 1:c0 + 2] * os_[hh:hh + 1]
                    + gates[:, c0 + 2:c0 + 3] * ow[hh:hh + 1])
    o_ref[...] = jnp.concatenate(outs, axis=1)


def _block_pool_matrix(past):
    blk = jnp.arange(past) // CMP_BLK
    col = (blk % 2) * (past // SEL_BLK) + blk // 2
    return jnp.where(col[:, None] == jnp.arange(2 * (past // SEL_BLK))[None, :], 1.0 / CMP_BLK, 0.0).astype(bf16)


def _nsa_decode(page_table, qb, rows_new, win_new, small, win_buf, cache_t):
    nb, n_pages = page_table.shape
    past = n_pages * PAGE_SIZE
    assert past % SEL_BLK == 0 and cache_t.shape[1:] == (4, B_KV, B_DH, PAGE_SIZE)
    n_blk = past // SEL_BLK
    n_pick = min(SEL_TOPK, n_blk + 1)
    pool = _block_pool_matrix(past)
    one1 = lambda shape: pl.BlockSpec((None,) + shape, lambda i, pt: (i, 0, 0))
    cmp_spec = pltpu.PrefetchScalarGridSpec(
        num_scalar_prefetch=1, grid=(nb,),
        in_specs=[one1((1, B_WIDTH)), one1((1, 2 * B_KV * B_DH)), one1(win_buf.shape[1:]),
                  pl.BlockSpec(pool.shape, lambda i, pt: (0, 0)), pl.BlockSpec(memory_space=pl.ANY)],
        out_specs=[one1((B_HEADS, B_DH)), one1((B_HEADS, B_DH)), one1((8, LANES))],
        scratch_shapes=[pltpu.VMEM((2, n_pages, 2, B_KV, B_DH, PAGE_SIZE), f32),
                        pltpu.VMEM((2 * B_KV * B_DH, 2 * n_blk), f32), pltpu.SemaphoreType.DMA((2,))])
    o_cmp, o_win, idx = pl.pallas_call(
        functools.partial(_nsa_decode_cmp_kernel, n_pages=n_pages, past=past), grid_spec=cmp_spec,
        out_shape=[jax.ShapeDtypeStruct((nb, B_HEADS, B_DH), f32), jax.ShapeDtypeStruct((nb, B_HEADS, B_DH), f32),
                   jax.ShapeDtypeStruct((nb, 8, LANES), jnp.int32)],
        compiler_params=_cparams(("arbitrary",)), name="nsa_decode_cmp",
    )(page_table, qb, win_new, win_buf, pool, cache_t)
    picks = idx[:, :B_KV, :n_pick]
    one2 = lambda shape: pl.BlockSpec((None,) + shape, lambda i, pt, sel: (i, 0, 0))
    sel_spec = pltpu.PrefetchScalarGridSpec(
        num_scalar_prefetch=2, grid=(nb,),
        in_specs=[one2((1, B_WIDTH)), one2((1, 4 * B_KV * B_DH)), one2((1, LANES)),
                  one2((B_HEADS, B_DH)), one2((B_HEADS, B_DH)), pl.BlockSpec(memory_space=pl.ANY)],
        out_specs=one2((1, B_WIDTH)),
        scratch_shapes=[pltpu.VMEM((2, B_KV * n_pick, 2, B_DH, PAGE_SIZE), f32), pltpu.SemaphoreType.DMA((2,))])
    return pl.pallas_call(
        functools.partial(_nsa_decode_sel_kernel, past=past), grid_spec=sel_spec,
        out_shape=jax.ShapeDtypeStruct((nb, 1, B_WIDTH), f32),
        compiler_params=_cparams(("arbitrary",)), name="nsa_decode_sel",
    )(page_table, picks, qb, rows_new, small, o_cmp, o_win, cache_t)


SAMPLE_PAD = DELTA_CHUNK
SAMPLE_SEQS = 8
SAMPLE_ROW = 8


def _even_weights(w_in):
    o = 0
    cols = {}
    for name, n in (("qkv", A_CONV_CH), ("beta", A_HEADS), ("a", A_HEADS), ("gate", A_WIDTH), ("qb", B_WIDTH),
                    ("kv", N_BRANCH * 2 * B_KV * B_DH), ("gb", B_HEADS * N_BRANCH)):
        cols[name] = w_in[:, o:o + n]
        o += n
    n_rows = 4 * B_KV * B_DH
    small = jnp.concatenate([cols["beta"], cols["a"], cols["gb"]], axis=1)
    small = jnp.pad(small, ((0, 0), (0, LANES - small.shape[1])))
    ws = [cols["qkv"], cols["gate"], cols["qb"], cols["kv"][:, :n_rows], cols["kv"][:, n_rows:], small]
    return [w.astype(bf16) for w in ws]


def _odd_weights(w_in):
    o = 0
    ws = []
    for n in (C_QK, C_QK, C_WIDTH, C_LOWRANK, C_WIDTH):
        ws.append(w_in[:, o:o + n])
        o += n
    ws[3] = jnp.pad(ws[3], ((0, 0), (0, LANES - C_LOWRANK)))
    return [w.astype(bf16) for w in ws]


def _lane_param(v, lane0):
    return jnp.zeros((1, LANES), f32).at[0, lane0:lane0 + v.shape[0]].set(v.astype(f32))


def _pack_rows(a, nb, history=None):
    a = a.reshape(nb, 1, -1)
    before = SAMPLE_ROW
    if history is not None:
        a = jnp.concatenate([history, a], axis=1)
        before -= history.shape[1]
    a = jnp.pad(a, ((0, 0), (before, SAMPLE_PAD - before - a.shape[1]), (0, 0)))
    return a.reshape(nb // SAMPLE_SEQS, SAMPLE_SEQS * SAMPLE_PAD, -1)


def _unpack_rows(a, nb):
    return a.reshape(nb, SAMPLE_PAD, -1)[:, SAMPLE_ROW]


def _pack_state(s, nb):
    return s.reshape((nb // SAMPLE_SEQS, SAMPLE_SEQS) + s.shape[1:])


def kernel(x_prompt, x_sample, cache_nsa_kv, state_nsa_win, state_delta_conv, state_delta_S, state_gla_S,
           page_table, w_in_even, conv_w_delta, delta_A_log, delta_dt_bias, delta_norm_g, w_out_even,
           w_in_odd, w_gla_gate2, b_gla_gate2, gla_norm_g, w_out_odd, w_ffn_gate, w_ffn_up, w_ffn_down,
           ln_g, ln_b):
    bp, t, d = x_prompt.shape
    bs = x_sample.shape[0]
    assert x_sample.shape[1] == 1 and w_in_even.shape[0] == 1 and w_in_odd.shape[0] == 1
    n_pages = page_table.shape[1]
    past = n_pages * PAGE_SIZE
    xp = x_prompt.reshape(bp * t, d)
    xs = x_sample.reshape(bs, d)
    ln = lambda layer, j: (ln_g[layer, j].reshape(1, d), ln_b[layer, j].reshape(1, d))
    ffn_w = lambda layer: (w_ffn_gate[layer].astype(bf16), w_ffn_up[layer].astype(bf16),
                           w_ffn_down[layer].astype(bf16))

    ws = _even_weights(w_in_even[0])
    conv_w = conv_w_delta[0]
    alog = _lane_param(delta_A_log[0], A_HEADS)
    dtb = _lane_param(delta_dt_bias[0], A_HEADS)
    dng = delta_norm_g[0].reshape(1, A_DV)
    wo = w_out_even[0].astype(bf16)
    wo_parts = [wo[:A_WIDTH], wo[A_WIDTH:]]

    qkv_p, gate_p, qb_p, rows_p, win_p, small_p, rows_planes_p = _proj(xp, ws, planes_of=3)
    r3 = lambda a: a.reshape(bp, t, -1)
    o_a_p, ds_p = _delta(r3(qkv_p), r3(small_p), r3(gate_p), jnp.zeros((bp, 8, A_CONV_CH), f32), conv_w,
                         alog, dtb, dng, jnp.zeros((bp, A_HEADS, A_DK, A_DV), f32))
    o_b_p = _nsa_prompt(r3(qb_p), r3(rows_p), r3(win_p), r3(small_p))
    xp = _mix_ffn([o_a_p.reshape(bp * t, -1), o_b_p.reshape(bp * t, -1)], wo_parts, xp, ln(0, 0), ffn_w(0), ln(0, 1))

    qkv_s, gate_s, qb_s, rows_s, win_s, small_s = _proj(xs, ws)
    conv_s = state_delta_conv[0]
    assert bs % SAMPLE_SEQS == 0
    o_a_s, ds_s = _delta(_pack_rows(qkv_s, bs, history=conv_s), _pack_rows(small_s, bs), _pack_rows(gate_s, bs),
                         jnp.zeros((bs // SAMPLE_SEQS, 8, A_CONV_CH), f32), conv_w, alog, dtb, dng,
                         _pack_state(state_delta_S[0], bs), live_row=SAMPLE_ROW)
    o_a_s = _unpack_rows(o_a_s, bs)
    ds_s = ds_s.reshape(state_delta_S[0].shape)
    win_buf = state_nsa_win[0]
    wb = win_buf.shape[1]
    r1 = lambda a: a.reshape(bs, 1, -1)
    o_b_s = _nsa_decode(page_table, r1(qb_s), r1(rows_s), r1(win_s), r1(small_s), win_buf.reshape(bs, wb, -1),
                        cache_nsa_kv[:, 0].transpose(0, 2, 3, 4, 1))
    xs = _mix_ffn([o_a_s, o_b_s.reshape(bs, -1)], wo_parts, xs, ln(0, 0), ffn_w(0), ln(0, 1))

    kvd = (B_KV, B_DH)
    nsa_rows_p = rows_planes_p.reshape((1, bp, t, 4) + kvd)
    nsa_win_p = r3(win_p)[:, -min(WINDOW, t):].reshape((1, bp, min(WINDOW, t), 2) + kvd)
    delta_conv_p = jnp.concatenate([jnp.zeros((bp, CONV_W - 1, A_CONV_CH), f32), r3(qkv_p)], axis=1)[:, -(CONV_W - 1):][None]
    nsa_rows_s = rows_s.reshape((1, bs, 1, 4) + kvd)
    win_cat = jnp.concatenate([win_buf, win_s.reshape((bs, 1, 2) + kvd)], axis=1)
    nsa_win_s = win_cat[:, -min(WINDOW, wb + 1):][None]
    delta_conv_s = jnp.concatenate([conv_s, qkv_s[:, None, :]], axis=1)[:, -(CONV_W - 1):][None]

    wq, wk, wv, wg1, wr = _odd_weights(w_in_odd[0])
    wg2 = jnp.pad(w_gla_gate2[0], ((0, LANES - C_LOWRANK), (0, 0))).astype(bf16)
    bg2 = b_gla_gate2[0].reshape(1, C_QK)
    gng = gla_norm_g[0].reshape(1, C_DV)
    wo1 = [w_out_odd[0].astype(bf16)]

    q_p, k_p, v_p, g1_p, rr_p = _proj(xp, [wq, wk, wv, wg1, wr])
    o_c_p, gs_p = _gla(r3(q_p), r3(k_p), r3(v_p), r3(rr_p), r3(g1_p), wg2, bg2, gng,
                       jnp.zeros((bp, C_HEADS, C_DK, C_DV), f32))
    xp = _mix_ffn([o_c_p.reshape(bp * t, -1)], wo1, xp, ln(1, 0), ffn_w(1), ln(1, 1))

    q_s, k_s, v_s, g1_s, rr_s = _proj(xs, [wq, wk, wv, wg1, wr])
    o_c_s, gs_s = _gla(_pack_rows(q_s, bs), _pack_rows(k_s, bs), _pack_rows(v_s, bs), _pack_rows(rr_s, bs),
                       _pack_rows(g1_s, bs), wg2, bg2, gng, _pack_state(state_gla_S[0], bs), live_row=SAMPLE_ROW)
    gs_s = gs_s.reshape(state_gla_S[0].shape)
    xs = _mix_ffn([_unpack_rows(o_c_s, bs)], wo1, xs, ln(1, 0), ffn_w(1), ln(1, 1))

    return (xp.reshape(bp, t, d), xs.reshape(bs, 1, d),
            nsa_rows_p, nsa_win_p, delta_conv_p, ds_p[None], gs_p[None],
            nsa_rows_s, nsa_win_s, delta_conv_s, ds_s[None], gs_s[None])
```
